```python
import math
import jax
import jax.numpy as jnp
from jax import lax
import numpy as np

D_MODEL = 1024
BATCH = 2
SEQ = 8192
DEPTH = 2

ROPE_THETA = 10000.0
NORM_EPS = 1e-6
QBLK = 128
NEG = -1e30

A_HEADS = 6
A_HEAD_DIM = 64
A_CONFIGS = ((128, 1), (512, 4), (2048, 16))
B_HEADS = 6
B_NOPE = 64
B_ROPE = 32
B_V = 64
B_Q_LORA = 384
B_KV_LORA = 128
C_HEADS = 4
C_QK = 32
C_V = 2 * C_QK
C_SUBLN_EPS = 1e-5

A_WIDTH = A_HEADS * A_HEAD_DIM
B_WIDTH = B_HEADS * B_V
C_WIDTH = C_HEADS * C_V
MIX_WIDTH = A_WIDTH + B_WIDTH + C_WIDTH

IN_SPLITS = (A_WIDTH, A_WIDTH, A_WIDTH,
             B_Q_LORA, B_KV_LORA, B_ROPE,
             C_HEADS * 2 * C_QK, C_HEADS * 2 * C_QK, C_WIDTH)
IN_COLS = int(sum(IN_SPLITS))
D_FF = 4 * D_MODEL

kernel_name = "hybrid_dilated_mla_diffattn_block"


def rmsnorm(x, g, eps=NORM_EPS):
    xf = x.astype(jnp.float32)
    y = xf * lax.rsqrt(jnp.mean(xf * xf, axis=-1, keepdims=True) + eps)
    return (y * g.astype(jnp.float32)).astype(x.dtype)


def rope_tables(seq, dim):
    half = dim // 2
    inv = 1.0 / (ROPE_THETA ** (jnp.arange(half, dtype=jnp.float32) / half))
    ang = jnp.arange(seq, dtype=jnp.float32)[:, None] * inv[None, :]
    return jnp.cos(ang), jnp.sin(ang)


def apply_rope(x, cos, sin):
    half = x.shape[-1] // 2
    xf = x.astype(jnp.float32)
    x1, x2 = xf[..., :half], xf[..., half:]
    return jnp.concatenate([x1 * cos - x2 * sin, x1 * sin + x2 * cos], axis=-1).astype(x.dtype)


def to_heads(t, n_heads):
    b, s, _ = t.shape
    return t.reshape(b, s, n_heads, -1).transpose(0, 2, 1, 3)


def from_heads(t):
    b, h, s, d = t.shape
    return t.transpose(0, 2, 1, 3).reshape(b, s, h * d)


def banded_attn(q, k, v, span, scale):
    b, g, n, dh = q.shape
    nb = n // span
    qb = q.reshape(b, g, nb, span, dh)
    kb = k.reshape(b, g, nb, span, dh)
    vb = v.reshape(b, g, nb, span, v.shape[-1])

    def with_prev(t):
        prev = jnp.pad(t[:, :, :-1], ((0, 0), (0, 0), (1, 0), (0, 0), (0, 0)))
        return jnp.concatenate([prev, t], axis=3)

    kk, vv = with_prev(kb), with_prev(vb)
    s = jnp.einsum('bgnqd,bgnkd->bgnqk', qb, kk).astype(jnp.float32) * scale
    qi = jnp.arange(span)[:, None]
    ki = jnp.arange(2 * span)[None, :]
    dist = qi + span - ki
    blk = jnp.arange(nb)[:, None, None]
    valid = (dist >= 0) & (dist <= span) & ((blk > 0) | (ki >= span))
    s = jnp.where(valid, s, NEG)
    lse = jax.nn.logsumexp(s, axis=-1)
    p = jnp.exp(s - lse[..., None])
    out = jnp.einsum('bgnqk,bgnkd->bgnqd', p.astype(v.dtype), vv)
    return out.reshape(b, g, n, -1), lse.reshape(b, g, n)


def dilated_attn(q, k, v, window, dilation, scale):
    b, h, s, dh = q.shape
    span = window // dilation
    unit = dilation * span
    sp = -(-s // unit) * unit
    sub = sp // dilation

    def split(t):
        t = jnp.pad(t, ((0, 0), (0, 0), (0, sp - s), (0, 0)))
        return t.reshape(b, h, sub, dilation, t.shape[-1]).transpose(0, 1, 3, 2, 4).reshape(b, h * dilation, sub, t.shape[-1])

    out, lse = banded_attn(split(q), split(k), split(v), span, scale)
    out = out.reshape(b, h, dilation, sub, -1).transpose(0, 1, 3, 2, 4).reshape(b, h, sp, -1)[:, :, :s]
    lse = lse.reshape(b, h, dilation, sub).transpose(0, 1, 3, 2).reshape(b, h, sp)[:, :, :s]
    return out, lse


def causal_attn(q, k, v, scale, coef):
    b, h, m, s, dq = q.shape
    nb = s // QBLK
    qb = q.reshape(b, h, m, nb, QBLK, dq).transpose(3, 0, 1, 2, 4, 5)
    kpos = jnp.arange(s)

    def one(args):
        qblk, i = args
        sc = jnp.einsum('bhmqd,bhmkd->bhmqk', qblk, k).astype(jnp.float32) * scale
        qpos = i * QBLK + jnp.arange(QBLK)
        sc = jnp.where(kpos[None, :] <= qpos[:, None], sc, NEG)
        p = jnp.einsum('bhmqk,m->bhqk', jax.nn.softmax(sc, axis=-1), coef)
        return jnp.einsum('bhqk,bhkd->bhqd', p.astype(v.dtype), v)

    out = lax.map(one, (qb, jnp.arange(nb)))
    return out.transpose(1, 2, 0, 3, 4).reshape(b, h, s, -1)


def mixer_dilated(qa, ka, va, cos64, sin64):
    q = apply_rope(to_heads(qa, A_HEADS), cos64, sin64)
    k = apply_rope(to_heads(ka, A_HEADS), cos64, sin64)
    v = to_heads(va, A_HEADS)
    scale = A_HEAD_DIM ** -0.5
    outs, lses = [], []
    for window, dilation in A_CONFIGS:
        o, l = dilated_attn(q, k, v, window, dilation, scale)
        outs.append(o)
        lses.append(l)
    w = jax.nn.softmax(jnp.stack(lses, axis=0), axis=0)
    out = jnp.sum(w[..., None].astype(v.dtype) * jnp.stack(outs, axis=0), axis=0)
    return from_heads(out)


def mixer_mla(cq, ckv, kr, q_norm_g, w_uq, kv_norm_g, w_ukv, cos32, sin32):
    b, s, _ = cq.shape
    qf = to_heads(rmsnorm(cq, q_norm_g) @ w_uq, B_HEADS)
    kvf = to_heads(rmsnorm(ckv, kv_norm_g) @ w_ukv, B_HEADS)
    q = jnp.concatenate([qf[..., :B_NOPE], apply_rope(qf[..., B_NOPE:], cos32, sin32)], axis=-1)
    k_rope = jnp.broadcast_to(apply_rope(kr, cos32, sin32)[:, None], (b, B_HEADS, s, B_ROPE))
    k = jnp.concatenate([kvf[..., :B_NOPE], k_rope], axis=-1)
    v = kvf[..., B_NOPE:]
    coef = jnp.ones((1,), jnp.float32)
    out = causal_attn(q[:, :, None], k[:, :, None], v, (B_NOPE + B_ROPE) ** -0.5, coef)
    return from_heads(out)


def mixer_diff(qc, kc, vc, lam_params, subln_g, lam_init, cos32, sin32):
    b, s, _ = qc.shape
    q = apply_rope(qc.reshape(b, s, C_HEADS, 2, C_QK).transpose(0, 2, 3, 1, 4), cos32, sin32)
    k = apply_rope(kc.reshape(b, s, C_HEADS, 2, C_QK).transpose(0, 2, 3, 1, 4), cos32, sin32)
    v = to_heads(vc, C_HEADS)
    lp = lam_params.astype(jnp.float32)
    lam = jnp.exp(jnp.sum(lp[0] * lp[1])) - jnp.exp(jnp.sum(lp[2] * lp[3])) + lam_init
    coef = jnp.stack([jnp.ones((), jnp.float32), -lam])
    out = causal_attn(q, k, v, C_QK ** -0.5, coef)
    out = rmsnorm(out, subln_g, C_SUBLN_EPS) * (1.0 - lam_init)
    return from_heads(out)


def setup_inputs(seed: int = 0) -> dict:
    key = jax.random.key(seed)
    ks = jax.random.split(key, 16)

    def nrm(k, shape, fan):
        return jax.random.normal(k, shape, jnp.float32) * (fan ** -0.5)

    def gain(k, shape):
        return 1.0 + 0.02 * jax.random.normal(k, shape, jnp.float32)

    return {
        "x": jax.random.normal(ks[0], (BATCH, SEQ, D_MODEL), jnp.float32),
        "ln1_g": gain(ks[1], (DEPTH, D_MODEL)),
        "w_in": nrm(ks[2], (DEPTH, D_MODEL, IN_COLS), D_MODEL),
        "mla_q_norm_g": gain(ks[3], (DEPTH, B_Q_LORA)),
        "mla_w_uq": nrm(ks[4], (DEPTH, B_Q_LORA, B_HEADS * (B_NOPE + B_ROPE)), B_Q_LORA),
        "mla_kv_norm_g": gain(ks[5], (DEPTH, B_KV_LORA)),
        "mla_w_ukv": nrm(ks[6], (DEPTH, B_KV_LORA, B_HEADS * (B_NOPE + B_V)), B_KV_LORA),
        "diff_lambda": 0.1 * jax.random.normal(ks[7], (DEPTH, 4, C_QK), jnp.float32),
        "diff_subln_g": gain(ks[8], (DEPTH, C_V)),
        "w_o": nrm(ks[9], (DEPTH, MIX_WIDTH, D_MODEL), MIX_WIDTH),
        "ln2_g": gain(ks[10], (DEPTH, D_MODEL)),
        "w_up": nrm(ks[11], (DEPTH, D_MODEL, D_FF), D_MODEL),
        "w_down": nrm(ks[12], (DEPTH, D_FF, D_MODEL), D_FF),
        "final_g": gain(ks[13], (D_MODEL,)),
    }


def reference(x, ln1_g, w_in, mla_q_norm_g, mla_w_uq, mla_kv_norm_g, mla_w_ukv,
              diff_lambda, diff_subln_g, w_o, ln2_g, w_up, w_down, final_g):
    s = x.shape[1]
    cos64, sin64 = rope_tables(s, A_HEAD_DIM)
    cos32, sin32 = rope_tables(s, C_QK)
    split_idx = [int(v) for v in np.cumsum(IN_SPLITS)[:-1]]

    for layer in range(DEPTH):
        lam_init = 0.8 - 0.6 * math.exp(-0.3 * layer)
        h = rmsnorm(x, ln1_g[layer])
        proj = h @ w_in[layer]
        qa, ka, va, cq, ckv, kr, qc, kc, vc = jnp.split(proj, split_idx, axis=-1)

        out_a = mixer_dilated(qa, ka, va, cos64, sin64)
        out_b = mixer_mla(cq, ckv, kr, mla_q_norm_g[layer], mla_w_uq[layer],
                          mla_kv_norm_g[layer], mla_w_ukv[layer], cos32, sin32)
        out_c = mixer_diff(qc, kc, vc, diff_lambda[layer], diff_subln_g[layer], lam_init, cos32, sin32)

        mixed = jnp.concatenate([out_a, out_b, out_c], axis=-1)
        x = x + mixed @ w_o[layer]

        u = rmsnorm(x, ln2_g[layer]) @ w_up[layer]
        x = x + jnp.square(jax.nn.relu(u)) @ w_down[layer]

    return rmsnorm(x, final_g)
```

```python
import functools
import math

import jax
import jax.numpy as jnp
from jax import lax
from jax.experimental import pallas as pl
from jax.experimental.pallas import tpu as pltpu

F32 = jnp.float32
BF16 = jnp.bfloat16

LANES = 128
VMEM_LIMIT = 56 * 1024 * 1024

ROPE_THETA = 10000.0
NORM_EPS = 1e-6
SUBLN_EPS = 1e-5
NEG = -1e30

A_HEADS, A_DIM = 6, 64
A_CONFIGS = ((128, 1), (512, 4), (2048, 16))
A_SPAN = 128
B_HEADS, B_NOPE, B_ROPE, B_V = 6, 64, 32, 64
B_Q_LORA, B_KV_LORA = 384, 128
C_HEADS, C_QK = 4, 32
C_V = 2 * C_QK
A_WIDTH = A_HEADS * A_DIM
B_WIDTH = B_HEADS * B_V
C_WIDTH = C_HEADS * C_V
D_FF_CHUNK = 512


def _rms(x, g, eps):
    return x * lax.rsqrt(jnp.mean(x * x, axis=-1, keepdims=True) + eps) * g


def _rope(x, cos, sin_signed, half):
    lane = lax.broadcasted_iota(jnp.int32, (x.shape[0], LANES), 1)
    first = (lane & (2 * half - 1)) < half
    outs = []
    for c in range(x.shape[1] // LANES):
        xc = x[:, c * LANES:(c + 1) * LANES]
        partner = jnp.where(first, pltpu.roll(xc, LANES - half, 1), pltpu.roll(xc, half, 1))
        outs.append(xc * cos + partner * sin_signed)
    return outs[0] if len(outs) == 1 else jnp.concatenate(outs, axis=1)


def _nt_dot(a, b):
    return lax.dot_general(a, b, (((1,), (1,)), ((), ())), preferred_element_type=F32)


def _inproj_kernel(x_ref, g_ref, w1_ref, cosa_ref, sina_ref, cos32_ref, sin32_ref,
                   qg_ref, wuq_ref, kvg_ref, wukv_ref,
                   qa_ref, ka_ref, va_ref, qn_ref, qr_ref, kn_ref, vb_ref, kr_ref,
                   qc_ref, kc_ref, vc_ref):
    h = _rms(x_ref[...], g_ref[...], NORM_EPS).astype(BF16)

    def proj(lo, width):
        return jnp.dot(h, w1_ref[:, lo:lo + width], preferred_element_type=F32)

    cosa, sina = cosa_ref[...], sina_ref[...]
    cos32, sin32 = cos32_ref[...], sin32_ref[...]
    scale_a = A_DIM ** -0.5
    scale_b = (B_NOPE + B_ROPE) ** -0.5
    scale_c = C_QK ** -0.5

    qa_ref[...] = (_rope(proj(0, 384), cosa, sina, 32) * scale_a).astype(BF16)
    ka_ref[...] = _rope(proj(384, 384), cosa, sina, 32).astype(BF16)
    va_ref[...] = proj(768, 384).astype(BF16)

    cq = _rms(proj(1152, 384), qg_ref[...], NORM_EPS).astype(BF16)
    qf = jnp.dot(cq, wuq_ref[...], preferred_element_type=F32)
    qn_ref[...] = (qf[:, :384] * scale_b).astype(BF16)
    qr_ref[...] = (_rope(qf[:, 384:], cos32, sin32, 16) * scale_b).astype(BF16)

    ckv = _rms(proj(1536, 128), kvg_ref[...], NORM_EPS).astype(BF16)
    kvf = jnp.dot(ckv, wukv_ref[...], preferred_element_type=F32)
    kn_ref[...] = kvf[:, :384].astype(BF16)
    vb_ref[...] = kvf[:, 384:].astype(BF16)
    kr_ref[...] = _rope(proj(1664, 128), cos32, sin32, 16).astype(BF16)

    qc_ref[...] = (_rope(proj(1792, 256), cos32, sin32, 16) * scale_c).astype(BF16)
    kc_ref[...] = _rope(proj(2048, 256), cos32, sin32, 16).astype(BF16)
    vc_ref[...] = proj(2304, 256).astype(BF16)


def _inproj(x2, g, w1, tabs, qg, wuq, kvg, wukv, *, seq, tm):
    n = x2.shape[0]
    nt = n // tm
    st = seq // tm
    row = lambda i: (i, 0)
    const = lambda i: (0, 0)
    tab = lambda i: (i % st, 0)
    widths = (384, 384, 384, 384, 384, 384, 384, 128, 256, 256, 256)
    return pl.pallas_call(
        _inproj_kernel,
        grid=(nt,),
        in_specs=[
            pl.BlockSpec((tm, x2.shape[1]), row),
            pl.BlockSpec(g.shape, const),
            pl.BlockSpec(w1.shape, const),
            pl.BlockSpec((tm, LANES), tab), pl.BlockSpec((tm, LANES), tab),
            pl.BlockSpec((tm, LANES), tab), pl.BlockSpec((tm, LANES), tab),
            pl.BlockSpec(qg.shape, const), pl.BlockSpec(wuq.shape, const),
            pl.BlockSpec(kvg.shape, const), pl.BlockSpec(wukv.shape, const),
        ],
        out_specs=[pl.BlockSpec((tm, w), row) for w in widths],
        out_shape=[jax.ShapeDtypeStruct((n, w), BF16) for w in widths],
        compiler_params=pltpu.CompilerParams(
            dimension_semantics=("arbitrary",), vmem_limit_bytes=VMEM_LIMIT),
        name="inproj",
    )(x2, g, w1, *tabs, qg, wuq, kvg, wukv)


def _dilated_kernel(*refs, tq, first, last):
    if first:
        q_ref, kc_ref, kp_ref, vc_ref, vp_ref = refs[:5]
        rest = refs[5:]
    else:
        q_ref, kc_ref, kp_ref, vc_ref, vp_ref, oin_ref, lin_ref = refs[:7]
        rest = refs[7:]
    if last:
        (o_ref,) = rest
    else:
        o_ref, l_ref = rest

    nb = pl.program_id(2)
    lane = lax.broadcasted_iota(jnp.int32, (A_SPAN, LANES), 1)
    low = lane < A_DIM
    qi = lax.broadcasted_iota(jnp.int32, (2 * A_SPAN, 2 * A_SPAN), 0) & (A_SPAN - 1)
    ki = lax.broadcasted_iota(jnp.int32, (2 * A_SPAN, 2 * A_SPAN), 1)
    dist = qi + A_SPAN - ki
    band = (dist >= 0) & (dist <= A_SPAN)
    band0 = band & (ki >= jnp.where(nb > 0, 0, A_SPAN))
    zero = jnp.zeros((), BF16)

    for p in range(A_WIDTH // LANES):
        cols = slice(p * LANES, (p + 1) * LANES)
        for g in range(tq // A_SPAN):
            rows = slice(g * A_SPAN, (g + 1) * A_SPAN)
            qg = q_ref[rows, cols]
            qs = jnp.concatenate([jnp.where(low, qg, zero), jnp.where(low, zero, qg)], axis=0)
            if g == 0:
                kk = jnp.concatenate([kp_ref[:, cols], kc_ref[0:A_SPAN, cols]], axis=0)
                vv = jnp.concatenate([vp_ref[:, cols], vc_ref[0:A_SPAN, cols]], axis=0)
                valid = band0
            else:
                kk = kc_ref[(g - 1) * A_SPAN:(g + 1) * A_SPAN, cols]
                vv = vc_ref[(g - 1) * A_SPAN:(g + 1) * A_SPAN, cols]
                valid = band
            s = jnp.where(valid, _nt_dot(qs, kk), NEG)
            m = jnp.max(s, axis=1, keepdims=True)
            e = jnp.exp(s - m)
            l = jnp.sum(e, axis=1, keepdims=True)
            pv = jnp.dot(e.astype(BF16), vv, preferred_element_type=F32)
            o2 = pv / l
            lse2 = m + jnp.log(l)
            o = jnp.where(low, o2[:A_SPAN], o2[A_SPAN:])
            lse = jnp.where(low, lse2[:A_SPAN], lse2[A_SPAN:])
            if not first:
                o_prev = oin_ref[rows, cols]
                l_prev = lin_ref[rows, cols]
                mx = jnp.maximum(lse, l_prev)
                wa = jnp.exp(lse - mx)
                wb = jnp.exp(l_prev - mx)
                tot = wa + wb
                o = (wa * o + wb * o_prev) / tot
                lse = mx + jnp.log(tot)
            o_ref[rows, cols] = o.astype(o_ref.dtype)
            if not last:
                l_ref[rows, cols] = lse


def _dilated_call(q, k, v, state, *, dilation, first, last, tq):
    bsz, seq, width = q.shape
    sub = seq // dilation
    view = lambda t: t.reshape(bsz, sub, dilation * width)
    nblk = sub // tq
    per = tq // A_SPAN
    cur = lambda b, r, nb: (b, nb, r)
    prev = lambda b, r, nb: (b, jnp.maximum(nb * per - 1, 0), r)
    blk = pl.BlockSpec((None, tq, width), cur)
    pblk = pl.BlockSpec((None, A_SPAN, width), prev)
    in_specs = [blk, blk, pblk, blk, pblk]
    args = [view(q), view(k), view(k), view(v), view(v)]
    if not first:
        in_specs += [blk, blk]
        args += [view(state[0]), view(state[1])]
    if last:
        out_specs = [blk]
        out_shape = [jax.ShapeDtypeStruct((bsz, sub, dilation * width), BF16)]
    else:
        out_specs = [blk, blk]
        out_shape = [jax.ShapeDtypeStruct((bsz, sub, dilation * width), F32)] * 2
    outs = pl.pallas_call(
        functools.partial(_dilated_kernel, tq=tq, first=first, last=last),
        grid=(bsz, dilation, nblk),
        in_specs=in_specs,
        out_specs=out_specs,
        out_shape=out_shape,
        compiler_params=pltpu.CompilerParams(
            dimension_semantics=("arbitrary",) * 3, vmem_limit_bytes=VMEM_LIMIT),
        name=f"dilated_d{dilation}",
    )(*args)
    return [o.reshape(bsz, seq, width) for o in outs]


def _dilated_mixer(q, k, v):
    state = None
    n = len(A_CONFIGS)
    for idx, (window, dilation) in enumerate(A_CONFIGS):
        assert window // dilation == A_SPAN
        sub = q.shape[1] // dilation
        tq = min(512, sub)
        state = _dilated_call(q, k, v, state, dilation=dilation, first=idx == 0,
                              last=idx == n - 1, tq=tq)
    return state[0]


def _flash_rows(qs, k_block, v_block, m_ref, l_ref, acc_ref, *, nfull, tq, tk, stack):
    m_ref[...] = jnp.full(m_ref.shape, NEG, F32)
    l_ref[...] = jnp.zeros(l_ref.shape, F32)
    acc_ref[...] = jnp.zeros(acc_ref.shape, F32)

    def step(i, masked):
        s = _nt_dot(qs, k_block(i))
        if masked:
            r = lax.broadcasted_iota(jnp.int32, s.shape, 0) & (tq - 1)
            c = lax.broadcasted_iota(jnp.int32, s.shape, 1)
            s = jnp.where(c <= r, s, NEG)
        m_prev = m_ref[...]
        m_new = jnp.maximum(m_prev, jnp.max(s, axis=1, keepdims=True))
        alpha = jnp.exp(m_prev - m_new)
        p = jnp.exp(s - m_new)
        l_ref[...] = alpha * l_ref[...] + jnp.sum(p, axis=1, keepdims=True)
        acc_ref[...] = alpha * acc_ref[...] + jnp.dot(
            p.astype(BF16), v_block(i), preferred_element_type=F32)
        m_ref[...] = m_new

    def body(i, carry):
        step(i, False)
        return carry

    lax.fori_loop(0, nfull, body, 0)
    step(nfull, True)


def _mla_kernel(qn_ref, qr_ref, kn_ref, kr_ref, v_ref, o_ref, m_ref, l_ref, acc_ref, *, tq):
    qi = pl.program_id(2)
    lane = lax.broadcasted_iota(jnp.int32, (tq, LANES), 1)
    zero = jnp.zeros((), BF16)
    qn, qr = qn_ref[...], qr_ref[...]
    q0 = jnp.concatenate([jnp.where(lane < B_NOPE, qn, zero),
                          jnp.where(lane < B_ROPE, qr, zero)], axis=1)
    q1 = jnp.concatenate([jnp.where(lane >= B_NOPE, qn, zero),
                          jnp.where((lane >= B_ROPE) & (lane < 2 * B_ROPE), qr, zero)], axis=1)
    qs = jnp.concatenate([q0, q1], axis=0)

    def k_block(i):
        rows = pl.ds(pl.multiple_of(i * tq, tq), tq)
        return jnp.concatenate([kn_ref[rows, :], kr_ref[rows, :]], axis=1)

    def v_block(i):
        return v_ref[pl.ds(pl.multiple_of(i * tq, tq), tq), :]

    _flash_rows(qs, k_block, v_block, m_ref, l_ref, acc_ref, nfull=qi, tq=tq, tk=tq, stack=2)
    o2 = acc_ref[...] / l_ref[...]
    o_ref[...] = jnp.where(lane < B_V, o2[:tq], o2[tq:]).astype(o_ref.dtype)


def _mla_attn(qn, qr, kn, kr, v, *, tq):
    bsz, seq, width = qn.shape
    pairs = width // LANES
    qblk = pl.BlockSpec((None, tq, LANES), lambda b, p, i: (b, i, p))
    kblk = pl.BlockSpec((None, seq, LANES), lambda b, p, i: (b, 0, p))
    krblk = pl.BlockSpec((None, seq, LANES), lambda b, p, i: (b, 0, 0))
    return pl.pallas_call(
        functools.partial(_mla_kernel, tq=tq),
        grid=(bsz, pairs, seq // tq),
        in_specs=[qblk, qblk, kblk, krblk, kblk],
        out_specs=qblk,
        out_shape=jax.ShapeDtypeStruct((bsz, seq, width), BF16),
        scratch_shapes=[pltpu.VMEM((2 * tq, 1), F32), pltpu.VMEM((2 * tq, 1), F32),
                        pltpu.VMEM((2 * tq, LANES), F32)],
        compiler_params=pltpu.CompilerParams(
            dimension_semantics=("arbitrary",) * 3, vmem_limit_bytes=VMEM_LIMIT),
        name="mla_attn",
    )(qn, qr, kn, kr, v)


def _diff_kernel(lam_ref, g_ref, q_ref, k_ref, v_ref, o_ref, m_ref, l_ref, acc_ref,
                 *, tq, lam_init):
    qi = pl.program_id(2)
    lane = lax.broadcasted_iota(jnp.int32, (tq, LANES), 1)
    zero = jnp.zeros((), BF16)
    q = q_ref[...]
    qs = jnp.concatenate(
        [jnp.where((lane >= j * C_QK) & (lane < (j + 1) * C_QK), q, zero) for j in range(4)],
        axis=0)

    def k_block(i):
        return k_ref[pl.ds(pl.multiple_of(i * tq, tq), tq), :]

    def v_block(i):
        return v_ref[pl.ds(pl.multiple_of(i * tq, tq), tq), :]

    _flash_rows(qs, k_block, v_block, m_ref, l_ref, acc_ref, nfull=qi, tq=tq, tk=tq, stack=4)

    lp = lam_ref[...]
    lam = (jnp.exp(jnp.sum(lp[0:1] * lp[1:2], axis=1, keepdims=True))
           - jnp.exp(jnp.sum(lp[2:3] * lp[3:4], axis=1, keepdims=True)) + lam_init)
    o4 = acc_ref[...] / l_ref[...]
    low = lane < C_V
    d = jnp.where(low, o4[0:tq] - lam * o4[tq:2 * tq], o4[2 * tq:3 * tq] - lam * o4[3 * tq:])
    sq = d * d
    ms_lo = jnp.sum(jnp.where(low, sq, 0.0), axis=1, keepdims=True) * (1.0 / C_V)
    ms_hi = jnp.sum(jnp.where(low, 0.0, sq), axis=1, keepdims=True) * (1.0 / C_V)
    rs = jnp.where(low, lax.rsqrt(ms_lo + SUBLN_EPS), lax.rsqrt(ms_hi + SUBLN_EPS))
    o_ref[...] = ((d * rs * g_ref[...]) * (1.0 - lam_init)).astype(o_ref.dtype)


def _diff_attn(lam_params, g2, q, k, v, *, tq, lam_init):
    bsz, seq, width = q.shape
    pairs = width // LANES
    qblk = pl.BlockSpec((None, tq, LANES), lambda b, p, i: (b, i, p))
    kblk = pl.BlockSpec((None, seq, LANES), lambda b, p, i: (b, 0, p))
    const = lambda b, p, i: (0, 0)
    return pl.pallas_call(
        functools.partial(_diff_kernel, tq=tq, lam_init=lam_init),
        grid=(bsz, pairs, seq // tq),
        in_specs=[pl.BlockSpec(lam_params.shape, const), pl.BlockSpec(g2.shape, const),
                  qblk, kblk, kblk],
        out_specs=qblk,
        out_shape=jax.ShapeDtypeStruct((bsz, seq, width), BF16),
        scratch_shapes=[pltpu.VMEM((4 * tq, 1), F32), pltpu.VMEM((4 * tq, 1), F32),
                        pltpu.VMEM((4 * tq, LANES), F32)],
        compiler_params=pltpu.CompilerParams(
            dimension_semantics=("arbitrary",) * 3, vmem_limit_bytes=VMEM_LIMIT),
        name="diff_attn",
    )(lam_params, g2, q, k, v)


def _post_kernel(x_ref, oa_ref, ob_ref, oc_ref, wo_ref, g2_ref, wup_ref, wdn_ref, gf_ref,
                 y_ref, *, final):
    mixed = jnp.concatenate([oa_ref[...], ob_ref[...], oc_ref[...]], axis=1)
    x1 = x_ref[...] + jnp.dot(mixed, wo_ref[...], preferred_element_type=F32)
    hn = _rms(x1, g2_ref[...], NORM_EPS).astype(BF16)
    y_ref[...] = x1
    for c in range(wup_ref.shape[1] // D_FF_CHUNK):
        cs = slice(c * D_FF_CHUNK, (c + 1) * D_FF_CHUNK)
        u = jnp.dot(hn, wup_ref[:, cs], preferred_element_type=F32)
        a = jnp.square(jnp.maximum(u, 0.0)).astype(BF16)
        y_ref[...] += jnp.dot(a, wdn_ref[cs, :], preferred_element_type=F32)
    if final:
        y_ref[...] = _rms(y_ref[...], gf_ref[...], NORM_EPS)


def _post(x2, oa, ob, oc, wo, g2, wup, wdn, gf, *, tm, final):
    n, d = x2.shape
    row = lambda i: (i, 0)
    const = lambda i: (0, 0)
    return pl.pallas_call(
        functools.partial(_post_kernel, final=final),
        grid=(n // tm,),
        in_specs=[pl.BlockSpec((tm, d), row),
                  pl.BlockSpec((tm, oa.shape[1]), row),
                  pl.BlockSpec((tm, ob.shape[1]), row),
                  pl.BlockSpec((tm, oc.shape[1]), row),
                  pl.BlockSpec(wo.shape, const), pl.BlockSpec(g2.shape, const),
                  pl.BlockSpec(wup.shape, const), pl.BlockSpec(wdn.shape, const),
                  pl.BlockSpec(gf.shape, const)],
        out_specs=pl.BlockSpec((tm, d), row),
        out_shape=jax.ShapeDtypeStruct((n, d), F32),
        compiler_params=pltpu.CompilerParams(
            dimension_semantics=("arbitrary",), vmem_limit_bytes=VMEM_LIMIT),
        name="post_final" if final else "post",
    )(x2, oa, ob, oc, wo, g2, wup, wdn, gf)


def _rope_tables(seq):
    def tables(dim):
        half = dim // 2
        inv = 1.0 / (ROPE_THETA ** (jnp.arange(half, dtype=F32) / half))
        ang = jnp.arange(seq, dtype=F32)[:, None] * inv[None, :]
        cos, sin = jnp.cos(ang), jnp.sin(ang)
        reps = LANES // dim
        cos_t = jnp.tile(jnp.concatenate([cos, cos], axis=1), (1, reps))
        sin_t = jnp.tile(jnp.concatenate([-sin, sin], axis=1), (1, reps))
        return cos_t, sin_t
    cosa, sina = tables(A_DIM)
    cos32, sin32 = tables(C_QK)
    return cosa, sina, cos32, sin32


def _prep_layer_weights(w_in, w_uq, w_ukv):
    cols = jnp.split(w_in, [384, 768, 1152, 1536, 1664, 1696, 1952, 2208], axis=1)
    qa, ka, va, cq, ckv, kr, qc, kc, vc = cols
    w1 = jnp.concatenate([qa, ka, va, cq, ckv, jnp.tile(kr, (1, LANES // B_ROPE)), qc, kc, vc],
                         axis=1).astype(BF16)
    uq = w_uq.reshape(B_Q_LORA, B_HEADS, B_NOPE + B_ROPE)
    nope = uq[:, :, :B_NOPE].reshape(B_Q_LORA, B_HEADS * B_NOPE)
    rope = uq[:, :, B_NOPE:].reshape(B_Q_LORA, B_HEADS // 2, 2 * B_ROPE)
    rope = jnp.pad(rope, ((0, 0), (0, 0), (0, LANES - 2 * B_ROPE))).reshape(B_Q_LORA, -1)
    wuq = jnp.concatenate([nope, rope], axis=1).astype(BF16)
    ukv = w_ukv.reshape(B_KV_LORA, B_HEADS, B_NOPE + B_V)
    wukv = jnp.concatenate([ukv[:, :, :B_NOPE].reshape(B_KV_LORA, -1),
                            ukv[:, :, B_NOPE:].reshape(B_KV_LORA, -1)], axis=1).astype(BF16)
    return w1, wuq, wukv


def kernel(x, ln1_g, w_in, mla_q_norm_g, mla_w_uq, mla_kv_norm_g, mla_w_ukv, diff_lambda,
           diff_subln_g, w_o, ln2_g, w_up, w_down, final_g):
    bsz, seq, d_model = x.shape
    depth = w_in.shape[0]
    tm = 512
    tq = 512
    tabs = _rope_tables(seq)
    x2 = x.reshape(bsz * seq, d_model)
    gf = final_g.reshape(1, d_model)
    sh = lambda t, w: t.reshape(bsz, seq, w)

    for layer in range(depth):
        lam_init = 0.8 - 0.6 * math.exp(-0.3 * layer)
        w1, wuq, wukv = _prep_layer_weights(w_in[layer], mla_w_uq[layer], mla_w_ukv[layer])
        qa, ka, va, qn, qr, kn, vb, kr, qc, kc, vc = _inproj(
            x2, ln1_g[layer].reshape(1, -1), w1, tabs,
            mla_q_norm_g[layer].reshape(1, -1), wuq,
            mla_kv_norm_g[layer].reshape(1, -1), wukv, seq=seq, tm=tm)

        out_a = _dilated_mixer(sh(qa, 384), sh(ka, 384), sh(va, 384))
        out_b = _mla_attn(sh(qn, 384), sh(qr, 384), sh(kn, 384), sh(kr, 128), sh(vb, 384), tq=tq)
        g_sub = jnp.tile(diff_subln_g[layer], LANES // C_V).reshape(1, LANES)
        out_c = _diff_attn(diff_lambda[layer], g_sub, sh(qc, 256), sh(kc, 256), sh(vc, 256),
                           tq=tq // 2, lam_init=lam_init)

        x2 = _post(x2, out_a.reshape(-1, A_WIDTH), out_b.reshape(-1, B_WIDTH),
                   out_c.reshape(-1, C_WIDTH), w_o[layer].astype(BF16),
                   ln2_g[layer].reshape(1, -1), w_up[layer].astype(BF16),
                   w_down[layer].astype(BF16), gf, tm=tm, final=layer == depth - 1)

    return x2.reshape(bsz, seq, d_model)
```

```python
import functools
import math

import jax
import jax.numpy as jnp
from jax import lax
from jax.experimental import pallas as pl
from jax.experimental.pallas import tpu as pltpu

F32 = jnp.float32
BF16 = jnp.bfloat16

LANES = 128
VMEM_LIMIT = 56 * 1024 * 1024

ROPE_THETA = 10000.0
NORM_EPS = 1e-6
SUBLN_EPS = 1e-5
NEG = -1e30
LOG2E = math.log2(math.e)

A_HEADS, A_DIM = 6, 64
A_CONFIGS = ((128, 1), (512, 4), (2048, 16))
A_SPAN = 128
B_HEADS, B_NOPE, B_ROPE, B_V = 6, 64, 32, 64
B_Q_LORA, B_KV_LORA = 384, 128
C_HEADS, C_QK = 4, 32
C_V = 2 * C_QK
A_WIDTH = A_HEADS * A_DIM
B_WIDTH = B_HEADS * B_V
C_WIDTH = C_HEADS * C_V
D_FF_CHUNK = 512


def _rms(x, g, eps):
    return x * lax.rsqrt(jnp.mean(x * x, axis=-1, keepdims=True) + eps) * g


def _rope(x, cos, sin_signed, half):
    lane = lax.broadcasted_iota(jnp.int32, (x.shape[0], LANES), 1)
    first = (lane & (2 * half - 1)) < half
    outs = []
    for c in range(x.shape[1] // LANES):
        xc = x[:, c * LANES:(c + 1) * LANES]
        partner = jnp.where(first, pltpu.roll(xc, LANES - half, 1), pltpu.roll(xc, half, 1))
        outs.append(xc * cos + partner * sin_signed)
    return outs[0] if len(outs) == 1 else jnp.concatenate(outs, axis=1)


def _nt_dot(a, b):
    return lax.dot_general(a, b, (((1,), (1,)), ((), ())), preferred_element_type=F32)


def _inproj_kernel(x_ref, g_ref, w1_ref, cosa_ref, sina_ref, cos32_ref, sin32_ref,
                   qg_ref, wuq_ref, kvg_ref, wukv_ref,
                   qa_ref, ka_ref, va_ref, qn_ref, qr_ref, kn_ref, vb_ref, kr_ref,
                   qc_ref, kc_ref, vc_ref):
    h = _rms(x_ref[...], g_ref[...], NORM_EPS).astype(BF16)

    def proj(lo, width):
        return jnp.dot(h, w1_ref[:, lo:lo + width], preferred_element_type=F32)

    cosa, sina = cosa_ref[...], sina_ref[...]
    cos32, sin32 = cos32_ref[...], sin32_ref[...]
    scale_a = A_DIM ** -0.5
    scale_b = (B_NOPE + B_ROPE) ** -0.5 * LOG2E
    scale_c = C_QK ** -0.5 * LOG2E

    qa_ref[...] = (_rope(proj(0, 384), cosa, sina, 32) * scale_a).astype(BF16)
    ka_ref[...] = _rope(proj(384, 384), cosa, sina, 32).astype(BF16)
    va_ref[...] = proj(768, 384).astype(BF16)

    cq = _rms(proj(1152, 384), qg_ref[...], NORM_EPS).astype(BF16)
    qf = jnp.dot(cq, wuq_ref[...], preferred_element_type=F32)
    qn_ref[...] = (qf[:, :384] * scale_b).astype(BF16)
    qr_ref[...] = (_rope(qf[:, 384:], cos32, sin32, 16) * scale_b).astype(BF16)

    ckv = _rms(proj(1536, 128), kvg_ref[...], NORM_EPS).astype(BF16)
    kvf = jnp.dot(ckv, wukv_ref[...], preferred_element_type=F32)
    kn_ref[...] = kvf[:, :384].astype(BF16)
    vb_ref[...] = kvf[:, 384:].astype(BF16)
    kr_ref[...] = _rope(proj(1664, 128), cos32, sin32, 16).astype(BF16)

    qc_ref[...] = (_rope(proj(1792, 256), cos32, sin32, 16) * scale_c).astype(BF16)
    kc_ref[...] = _rope(proj(2048, 256), cos32, sin32, 16).astype(BF16)
    vc_ref[...] = proj(2304, 256).astype(BF16)


def _inproj(x2, g, w1, tabs, qg, wuq, kvg, wukv, *, seq, tm):
    n = x2.shape[0]
    nt = n // tm
    st = seq // tm
    row = lambda i: (i, 0)
    const = lambda i: (0, 0)
    tab = lambda i: (i % st, 0)
    widths = (384, 384, 384, 384, 384, 384, 384, 128, 256, 256, 256)
    return pl.pallas_call(
        _inproj_kernel,
        grid=(nt,),
        in_specs=[
            pl.BlockSpec((tm, x2.shape[1]), row),
            pl.BlockSpec(g.shape, const),
            pl.BlockSpec(w1.shape, const),
            pl.BlockSpec((tm, LANES), tab), pl.BlockSpec((tm, LANES), tab),
            pl.BlockSpec((tm, LANES), tab), pl.BlockSpec((tm, LANES), tab),
            pl.BlockSpec(qg.shape, const), pl.BlockSpec(wuq.shape, const),
            pl.BlockSpec(kvg.shape, const), pl.BlockSpec(wukv.shape, const),
        ],
        out_specs=[pl.BlockSpec((tm, w), row) for w in widths],
        out_shape=[jax.ShapeDtypeStruct((n, w), BF16) for w in widths],
        compiler_params=pltpu.CompilerParams(
            dimension_semantics=("arbitrary",), vmem_limit_bytes=VMEM_LIMIT),
        name="inproj",
    )(x2, g, w1, *tabs, qg, wuq, kvg, wukv)


def _dilated_kernel(*refs, tq, first, last):
    if first:
        q_ref, kc_ref, kp_ref, vc_ref, vp_ref = refs[:5]
        rest = refs[5:]
    else:
        q_ref, kc_ref, kp_ref, vc_ref, vp_ref, oin_ref, lin_ref = refs[:7]
        rest = refs[7:]
    if last:
        (o_ref,) = rest
    else:
        o_ref, l_ref = rest

    nb = pl.program_id(2)
    lane = lax.broadcasted_iota(jnp.int32, (A_SPAN, LANES), 1)
    low = lane < A_DIM
    qi = lax.broadcasted_iota(jnp.int32, (2 * A_SPAN, 2 * A_SPAN), 0) & (A_SPAN - 1)
    ki = lax.broadcasted_iota(jnp.int32, (2 * A_SPAN, 2 * A_SPAN), 1)
    dist = qi + A_SPAN - ki
    band = (dist >= 0) & (dist <= A_SPAN)
    band0 = band & (ki >= jnp.where(nb > 0, 0, A_SPAN))
    zero = jnp.zeros((), BF16)

    for p in range(A_WIDTH // LANES):
        cols = slice(p * LANES, (p + 1) * LANES)
        for g in range(tq // A_SPAN):
            rows = slice(g * A_SPAN, (g + 1) * A_SPAN)
            qg = q_ref[rows, cols]
            qs = jnp.concatenate([jnp.where(low, qg, zero), jnp.where(low, zero, qg)], axis=0)
            if g == 0:
                kk = jnp.concatenate([kp_ref[:, cols], kc_ref[0:A_SPAN, cols]], axis=0)
                vv = jnp.concatenate([vp_ref[:, cols], vc_ref[0:A_SPAN, cols]], axis=0)
                valid = band0
            else:
                kk = kc_ref[(g - 1) * A_SPAN:(g + 1) * A_SPAN, cols]
                vv = vc_ref[(g - 1) * A_SPAN:(g + 1) * A_SPAN, cols]
                valid = band
            s = jnp.where(valid, _nt_dot(qs, kk), NEG)
            m = jnp.max(s, axis=1, keepdims=True)
            e = jnp.exp(s - m)
            l = jnp.sum(e, axis=1, keepdims=True)
            pv = jnp.dot(e.astype(BF16), vv, preferred_element_type=F32)
            o2 = pv / l
            lse2 = m + jnp.log(l)
            o = jnp.where(low, o2[:A_SPAN], o2[A_SPAN:])
            lse = jnp.where(low, lse2[:A_SPAN], lse2[A_SPAN:])
            if not first:
                o_prev = oin_ref[rows, cols]
                l_prev = lin_ref[rows, cols]
                mx = jnp.maximum(lse, l_prev)
                wa = jnp.exp(lse - mx)
                wb = jnp.exp(l_prev - mx)
                tot = wa + wb
                o = (wa * o + wb * o_prev) / tot
                lse = mx + jnp.log(tot)
            o_ref[rows, cols] = o.astype(o_ref.dtype)
            if not last:
                l_ref[rows, cols] = lse


def _dilated_call(q, k, v, state, *, dilation, first, last, tq):
    bsz, seq, width = q.shape
    sub = seq // dilation
    view = lambda t: t.reshape(bsz, sub, dilation * width)
    nblk = sub // tq
    per = tq // A_SPAN
    cur = lambda b, r, nb: (b, nb, r)
    prev = lambda b, r, nb: (b, jnp.maximum(nb * per - 1, 0), r)
    blk = pl.BlockSpec((None, tq, width), cur)
    pblk = pl.BlockSpec((None, A_SPAN, width), prev)
    in_specs = [blk, blk, pblk, blk, pblk]
    args = [view(q), view(k), view(k), view(v), view(v)]
    if not first:
        in_specs += [blk, blk]
        args += [view(state[0]), view(state[1])]
    if last:
        out_specs = [blk]
        out_shape = [jax.ShapeDtypeStruct((bsz, sub, dilation * width), BF16)]
    else:
        out_specs = [blk, blk]
        out_shape = [jax.ShapeDtypeStruct((bsz, sub, dilation * width), F32)] * 2
    outs = pl.pallas_call(
        functools.partial(_dilated_kernel, tq=tq, first=first, last=last),
        grid=(bsz, dilation, nblk),
        in_specs=in_specs,
        out_specs=out_specs,
        out_shape=out_shape,
        compiler_params=pltpu.CompilerParams(
            dimension_semantics=("arbitrary",) * 3, vmem_limit_bytes=VMEM_LIMIT),
        name=f"dilated_d{dilation}",
    )(*args)
    return [o.reshape(bsz, seq, width) for o in outs]


def _dilated_mixer(q, k, v):
    state = None
    n = len(A_CONFIGS)
    for idx, (window, dilation) in enumerate(A_CONFIGS):
        assert window // dilation == A_SPAN
        sub = q.shape[1] // dilation
        tq = min(512, sub)
        state = _dilated_call(q, k, v, state, dilation=dilation, first=idx == 0,
                              last=idx == n - 1, tq=tq)
    return state[0]


def _flash_cols(qs, k_block, vt_block, m_ref, l_ref, acc_ref, *, nfull, tq):
    m_ref[...] = jnp.full(m_ref.shape, NEG, F32)
    l_ref[...] = jnp.zeros(l_ref.shape, F32)
    acc_ref[...] = jnp.zeros(acc_ref.shape, F32)

    def step(i, masked):
        st = _nt_dot(k_block(i), qs)
        if masked:
            key = lax.broadcasted_iota(jnp.int32, st.shape, 0)
            qry = lax.broadcasted_iota(jnp.int32, st.shape, 1) & (tq - 1)
            st = jnp.where(key <= qry, st, NEG)
        m_prev = m_ref[...]
        m_new = jnp.maximum(m_prev, jnp.max(st, axis=0, keepdims=True))
        alpha = jnp.exp2(m_prev - m_new)
        pt = jnp.exp2(st - m_new)
        l_ref[...] = alpha * l_ref[...] + jnp.sum(pt, axis=0, keepdims=True)
        acc_ref[...] = alpha * acc_ref[...] + jnp.dot(
            vt_block(i), pt.astype(BF16), preferred_element_type=F32)
        m_ref[...] = m_new

    def body(j, carry):
        step(2 * j, False)
        step(2 * j + 1, False)
        return carry

    lax.fori_loop(0, nfull // 2, body, 0)

    @pl.when(nfull % 2 == 1)
    def _():
        step(nfull - 1, False)

    step(nfull, True)


def _blocked_vt(v, tk):
    bsz, seq, width = v.shape
    v5 = v.reshape(bsz, seq // tk, tk, width // LANES, LANES)
    return jnp.transpose(v5, (0, 3, 1, 4, 2))


def _mla_kernel(qn_ref, qr_ref, kn_ref, kr_ref, vt_ref, o_ref, m_ref, l_ref, acc_ref, *, tq):
    qi = pl.program_id(2)
    lane = lax.broadcasted_iota(jnp.int32, (tq, LANES), 1)
    zero = jnp.zeros((), BF16)
    qn, qr = qn_ref[...], qr_ref[...]
    q0 = jnp.concatenate([jnp.where(lane < B_NOPE, qn, zero),
                          jnp.where(lane < B_ROPE, qr, zero)], axis=1)
    q1 = jnp.concatenate([jnp.where(lane >= B_NOPE, qn, zero),
                          jnp.where((lane >= B_ROPE) & (lane < 2 * B_ROPE), qr, zero)], axis=1)
    qs = jnp.concatenate([q0, q1], axis=0)

    def k_block(i):
        rows = pl.ds(pl.multiple_of(i * tq, tq), tq)
        return jnp.concatenate([kn_ref[rows, :], kr_ref[rows, :]], axis=1)

    def vt_block(i):
        return vt_ref[i]

    _flash_cols(qs, k_block, vt_block, m_ref, l_ref, acc_ref, nfull=qi, tq=tq)
    ot = acc_ref[...] / l_ref[...]
    pair_t = jnp.concatenate([ot[:B_V, :tq], ot[B_V:, tq:]], axis=0)
    o_ref[...] = pair_t.T.astype(o_ref.dtype)


def _mla_attn(qn, qr, kn, kr, v, *, tq):
    bsz, seq, width = qn.shape
    pairs = width // LANES
    nblk = seq // tq
    vt = _blocked_vt(v, tq)
    qblk = pl.BlockSpec((None, tq, LANES), lambda b, p, i: (b, i, p))
    kblk = pl.BlockSpec((None, seq, LANES), lambda b, p, i: (b, 0, p))
    krblk = pl.BlockSpec((None, seq, LANES), lambda b, p, i: (b, 0, 0))
    vtblk = pl.BlockSpec((None, None, nblk, LANES, tq), lambda b, p, i: (b, p, 0, 0, 0))
    return pl.pallas_call(
        functools.partial(_mla_kernel, tq=tq),
        grid=(bsz, pairs, nblk),
        in_specs=[qblk, qblk, kblk, krblk, vtblk],
        out_specs=qblk,
        out_shape=jax.ShapeDtypeStruct((bsz, seq, width), BF16),
        scratch_shapes=[pltpu.VMEM((1, 2 * tq), F32), pltpu.VMEM((1, 2 * tq), F32),
                        pltpu.VMEM((LANES, 2 * tq), F32)],
        compiler_params=pltpu.CompilerParams(
            dimension_semantics=("arbitrary",) * 3, vmem_limit_bytes=VMEM_LIMIT),
        name="mla_attn",
    )(qn, qr, kn, kr, vt)


def _diff_kernel(lam_ref, g_ref, q_ref, k_ref, vt_ref, o_ref, m_ref, l_ref, acc_ref,
                 *, tq, lam_init):
    qi = pl.program_id(2)
    lane = lax.broadcasted_iota(jnp.int32, (tq, LANES), 1)
    zero = jnp.zeros((), BF16)
    q = q_ref[...]
    qs = jnp.concatenate(
        [jnp.where((lane >= j * C_QK) & (lane < (j + 1) * C_QK), q, zero) for j in range(4)],
        axis=0)

    def k_block(i):
        return k_ref[pl.ds(pl.multiple_of(i * tq, tq), tq), :]

    def vt_block(i):
        return vt_ref[i]

    _flash_cols(qs, k_block, vt_block, m_ref, l_ref, acc_ref, nfull=qi, tq=tq)

    lp = lam_ref[...]
    lam = (jnp.exp(jnp.sum(lp[0:1] * lp[1:2], axis=1, keepdims=True))
           - jnp.exp(jnp.sum(lp[2:3] * lp[3:4], axis=1, keepdims=True)) + lam_init)
    ot = acc_ref[...] / l_ref[...]
    d0 = ot[:C_V, 0:tq] - lam * ot[:C_V, tq:2 * tq]
    d1 = ot[C_V:, 2 * tq:3 * tq] - lam * ot[C_V:, 3 * tq:]
    d = jnp.concatenate([d0, d1], axis=0).T
    low = lane < C_V
    sq = d * d
    ms_lo = jnp.sum(jnp.where(low, sq, 0.0), axis=1, keepdims=True) * (1.0 / C_V)
    ms_hi = jnp.sum(jnp.where(low, 0.0, sq), axis=1, keepdims=True) * (1.0 / C_V)
    rs = jnp.where(low, lax.rsqrt(ms_lo + SUBLN_EPS), lax.rsqrt(ms_hi + SUBLN_EPS))
    o_ref[...] = ((d * rs * g_ref[...]) * (1.0 - lam_init)).astype(o_ref.dtype)


def _diff_attn(lam_params, g2, q, k, v, *, tq, lam_init):
    bsz, seq, width = q.shape
    pairs = width // LANES
    nblk = seq // tq
    vt = _blocked_vt(v, tq)
    qblk = pl.BlockSpec((None, tq, LANES), lambda b, p, i: (b, i, p))
    kblk = pl.BlockSpec((None, seq, LANES), lambda b, p, i: (b, 0, p))
    vtblk = pl.BlockSpec((None, None, nblk, LANES, tq), lambda b, p, i: (b, p, 0, 0, 0))
    const = lambda b, p, i: (0, 0)
    return pl.pallas_call(
        functools.partial(_diff_kernel, tq=tq, lam_init=lam_init),
        grid=(bsz, pairs, nblk),
        in_specs=[pl.BlockSpec(lam_params.shape, const), pl.BlockSpec(g2.shape, const),
                  qblk, kblk, vtblk],
        out_specs=qblk,
        out_shape=jax.ShapeDtypeStruct((bsz, seq, width), BF16),
        scratch_shapes=[pltpu.VMEM((1, 4 * tq), F32), pltpu.VMEM((1, 4 * tq), F32),
                        pltpu.VMEM((LANES, 4 * tq), F32)],
        compiler_params=pltpu.CompilerParams(
            dimension_semantics=("arbitrary",) * 3, vmem_limit_bytes=VMEM_LIMIT),
        name="diff_attn",
    )(lam_params, g2, q, k, vt)


def _post_kernel(x_ref, oa_ref, ob_ref, oc_ref, wo_ref, g2_ref, wup_ref, wdn_ref, gf_ref,
                 y_ref, *, final):
    mixed = jnp.concatenate([oa_ref[...], ob_ref[...], oc_ref[...]], axis=1)
    x1 = x_ref[...] + jnp.dot(mixed, wo_ref[...], preferred_element_type=F32)
    hn = _rms(x1, g2_ref[...], NORM_EPS).astype(BF16)
    y_ref[...] = x1
    for c in range(wup_ref.shape[1] // D_FF_CHUNK):
        cs = slice(c * D_FF_CHUNK, (c + 1) * D_FF_CHUNK)
        u = jnp.dot(hn, wup_ref[:, cs], preferred_element_type=F32)
        a = jnp.square(jnp.maximum(u, 0.0)).astype(BF16)
        y_ref[...] += jnp.dot(a, wdn_ref[cs, :], preferred_element_type=F32)
    if final:
        y_ref[...] = _rms(y_ref[...], gf_ref[...], NORM_EPS)


def _post(x2, oa, ob, oc, wo, g2, wup, wdn, gf, *, tm, final):
    n, d = x2.shape
    row = lambda i: (i, 0)
    const = lambda i: (0, 0)
    return pl.pallas_call(
        functools.partial(_post_kernel, final=final),
        grid=(n // tm,),
        in_specs=[pl.BlockSpec((tm, d), row),
                  pl.BlockSpec((tm, oa.shape[1]), row),
                  pl.BlockSpec((tm, ob.shape[1]), row),
                  pl.BlockSpec((tm, oc.shape[1]), row),
                  pl.BlockSpec(wo.shape, const), pl.BlockSpec(g2.shape, const),
                  pl.BlockSpec(wup.shape, const), pl.BlockSpec(wdn.shape, const),
                  pl.BlockSpec(gf.shape, const)],
        out_specs=pl.BlockSpec((tm, d), row),
        out_shape=jax.ShapeDtypeStruct((n, d), F32),
        compiler_params=pltpu.CompilerParams(
            dimension_semantics=("arbitrary",), vmem_limit_bytes=VMEM_LIMIT),
        name="post_final" if final else "post",
    )(x2, oa, ob, oc, wo, g2, wup, wdn, gf)


def _rope_tables(seq):
    def tables(dim):
        half = dim // 2
        inv = 1.0 / (ROPE_THETA ** (jnp.arange(half, dtype=F32) / half))
        ang = jnp.arange(seq, dtype=F32)[:, None] * inv[None, :]
        cos, sin = jnp.cos(ang), jnp.sin(ang)
        reps = LANES // dim
        cos_t = jnp.tile(jnp.concatenate([cos, cos], axis=1), (1, reps))
        sin_t = jnp.tile(jnp.concatenate([-sin, sin], axis=1), (1, reps))
        return cos_t, sin_t
    cosa, sina = tables(A_DIM)
    cos32, sin32 = tables(C_QK)
    return cosa, sina, cos32, sin32


def _prep_layer_weights(w_in, w_uq, w_ukv):
    cols = jnp.split(w_in, [384, 768, 1152, 1536, 1664, 1696, 1952, 2208], axis=1)
    qa, ka, va, cq, ckv, kr, qc, kc, vc = cols
    w1 = jnp.concatenate([qa, ka, va, cq, ckv, jnp.tile(kr, (1, LANES // B_ROPE)), qc, kc, vc],
                         axis=1).astype(BF16)
    uq = w_uq.reshape(B_Q_LORA, B_HEADS, B_NOPE + B_ROPE)
    nope = uq[:, :, :B_NOPE].reshape(B_Q_LORA, B_HEADS * B_NOPE)
    rope = uq[:, :, B_NOPE:].reshape(B_Q_LORA, B_HEADS // 2, 2 * B_ROPE)
    rope = jnp.pad(rope, ((0, 0), (0, 0), (0, LANES - 2 * B_ROPE))).reshape(B_Q_LORA, -1)
    wuq = jnp.concatenate([nope, rope], axis=1).astype(BF16)
    ukv = w_ukv.reshape(B_KV_LORA, B_HEADS, B_NOPE + B_V)
    wukv = jnp.concatenate([ukv[:, :, :B_NOPE].reshape(B_KV_LORA, -1),
                            ukv[:, :, B_NOPE:].reshape(B_KV_LORA, -1)], axis=1).astype(BF16)
    return w1, wuq, wukv


def kernel(x, ln1_g, w_in, mla_q_norm_g, mla_w_uq, mla_kv_norm_g, mla_w_ukv, diff_lambda,
           diff_subln_g, w_o, ln2_g, w_up, w_down, final_g):
    bsz, seq, d_model = x.shape
    depth = w_in.shape[0]
    tm = 512
    tq = 512
    tabs = _rope_tables(seq)
    x2 = x.reshape(bsz * seq, d_model)
    gf = final_g.reshape(1, d_model)
    sh = lambda t, w: t.reshape(bsz, seq, w)

    for layer in range(depth):
        lam_init = 0.8 - 0.6 * math.exp(-0.3 * layer)
        w1, wuq, wukv = _prep_layer_weights(w_in[layer], mla_w_uq[layer], mla_w_ukv[layer])
        qa, ka, va, qn, qr, kn, vb, kr, qc, kc, vc = _inproj(
            x2, ln1_g[layer].reshape(1, -1), w1, tabs,
            mla_q_norm_g[layer].reshape(1, -1), wuq,
            mla_kv_norm_g[layer].reshape(1, -1), wukv, seq=seq, tm=tm)

        out_a = _dilated_mixer(sh(qa, 384), sh(ka, 384), sh(va, 384))
        out_b = _mla_attn(sh(qn, 384), sh(qr, 384), sh(kn, 384), sh(kr, 128), sh(vb, 384), tq=tq)
        g_sub = jnp.tile(diff_subln_g[layer], LANES // C_V).reshape(1, LANES)
        out_c = _diff_attn(diff_lambda[layer], g_sub, sh(qc, 256), sh(kc, 256), sh(vc, 256),
                           tq=tq // 2, lam_init=lam_init)

        x2 = _post(x2, out_a.reshape(-1, A_WIDTH), out_b.reshape(-1, B_WIDTH),
                   out_c.reshape(-1, C_WIDTH), w_o[layer].astype(BF16),
                   ln2_g[layer].reshape(1, -1), w_up[layer].astype(BF16),
                   w_down[layer].astype(BF16), gf, tm=tm, final=layer == depth - 1)

    return x2.reshape(bsz, seq, d_model)
```

```python
import functools
import math

import jax
import jax.numpy as jnp
from jax import lax
from jax.experimental import pallas as pl
from jax.experimental.pallas import tpu as pltpu

F32 = jnp.float32
BF16 = jnp.bfloat16

LANES = 128
ACC_ROWS = LANES + 16
VMEM_LIMIT = 56 * 1024 * 1024

ROPE_THETA = 10000.0
NORM_EPS = 1e-6
SUBLN_EPS = 1e-5
NEG = -1e30
LOG2E = math.log2(math.e)

A_HEADS, A_DIM = 6, 64
A_CONFIGS = ((128, 1), (512, 4), (2048, 16))
A_SPAN = 128
B_HEADS, B_NOPE, B_ROPE, B_V = 6, 64, 32, 64
B_Q_LORA, B_KV_LORA = 384, 128
C_HEADS, C_QK = 4, 32
C_V = 2 * C_QK
A_WIDTH = A_HEADS * A_DIM
B_WIDTH = B_HEADS * B_V
C_WIDTH = C_HEADS * C_V
D_FF_CHUNK = 512


def _rms(x, g, eps):
    return x * lax.rsqrt(jnp.mean(x * x, axis=-1, keepdims=True) + eps) * g


def _rope(x, cos, sin_signed, half):
    lane = lax.broadcasted_iota(jnp.int32, (x.shape[0], LANES), 1)
    first = (lane & (2 * half - 1)) < half
    outs = []
    for c in range(x.shape[1] // LANES):
        xc = x[:, c * LANES:(c + 1) * LANES]
        partner = jnp.where(first, pltpu.roll(xc, LANES - half, 1), pltpu.roll(xc, half, 1))
        outs.append(xc * cos + partner * sin_signed)
    return outs[0] if len(outs) == 1 else jnp.concatenate(outs, axis=1)


def _nt_dot(a, b):
    return lax.dot_general(a, b, (((1,), (1,)), ((), ())), preferred_element_type=F32)


def _inproj_kernel(x_ref, g_ref, w1_ref, cosa_ref, sina_ref, cos32_ref, sin32_ref,
                   qg_ref, wuq_ref, kvg_ref, wukv_ref,
                   qa_ref, ka_ref, va_ref, qn_ref, qr_ref, kn_ref, vb_ref, kr_ref,
                   qc_ref, kc_ref, vc_ref):
    h = _rms(x_ref[...], g_ref[...], NORM_EPS).astype(BF16)

    def proj(lo, width):
        return jnp.dot(h, w1_ref[:, lo:lo + width], preferred_element_type=F32)

    cosa, sina = cosa_ref[...], sina_ref[...]
    cos32, sin32 = cos32_ref[...], sin32_ref[...]
    scale_a = A_DIM ** -0.5
    scale_b = (B_NOPE + B_ROPE) ** -0.5 * LOG2E
    scale_c = C_QK ** -0.5 * LOG2E

    qa_ref[...] = (_rope(proj(0, 384), cosa, sina, 32) * scale_a).astype(BF16)
    ka_ref[...] = _rope(proj(384, 384), cosa, sina, 32).astype(BF16)
    va_ref[...] = proj(768, 384).astype(BF16)

    cq = _rms(proj(1152, 384), qg_ref[...], NORM_EPS).astype(BF16)
    qf = jnp.dot(cq, wuq_ref[...], preferred_element_type=F32)
    qn_ref[...] = (qf[:, :384] * scale_b).astype(BF16)
    qr_ref[...] = (_rope(qf[:, 384:], cos32, sin32, 16) * scale_b).astype(BF16)

    ckv = _rms(proj(1536, 128), kvg_ref[...], NORM_EPS).astype(BF16)
    kvf = jnp.dot(ckv, wukv_ref[...], preferred_element_type=F32)
    kn_ref[...] = kvf[:, :384].astype(BF16)
    vb_ref[...] = kvf[:, 384:].astype(BF16)
    kr_ref[...] = _rope(proj(1664, 128), cos32, sin32, 16).astype(BF16)

    qc_ref[...] = (_rope(proj(1792, 256), cos32, sin32, 16) * scale_c).astype(BF16)
    kc_ref[...] = _rope(proj(2048, 256), cos32, sin32, 16).astype(BF16)
    vc_ref[...] = proj(2304, 256).astype(BF16)


def _inproj(x2, g, w1, tabs, qg, wuq, kvg, wukv, *, seq, tm):
    n = x2.shape[0]
    nt = n // tm
    st = seq // tm
    row = lambda i: (i, 0)
    const = lambda i: (0, 0)
    tab = lambda i: (i % st, 0)
    widths = (384, 384, 384, 384, 384, 384, 384, 128, 256, 256, 256)
    return pl.pallas_call(
        _inproj_kernel,
        grid=(nt,),
        in_specs=[
            pl.BlockSpec((tm, x2.shape[1]), row),
            pl.BlockSpec(g.shape, const),
            pl.BlockSpec(w1.shape, const),
            pl.BlockSpec((tm, LANES), tab), pl.BlockSpec((tm, LANES), tab),
            pl.BlockSpec((tm, LANES), tab), pl.BlockSpec((tm, LANES), tab),
            pl.BlockSpec(qg.shape, const), pl.BlockSpec(wuq.shape, const),
            pl.BlockSpec(kvg.shape, const), pl.BlockSpec(wukv.shape, const),
        ],
        out_specs=[pl.BlockSpec((tm, w), row) for w in widths],
        out_shape=[jax.ShapeDtypeStruct((n, w), BF16) for w in widths],
        compiler_params=pltpu.CompilerParams(
            dimension_semantics=("arbitrary",), vmem_limit_bytes=VMEM_LIMIT),
        name="inproj",
    )(x2, g, w1, *tabs, qg, wuq, kvg, wukv)


def _dilated_kernel(*refs, tq, first, last):
    if first:
        q_ref, kc_ref, kp_ref, vc_ref, vp_ref = refs[:5]
        rest = refs[5:]
    else:
        q_ref, kc_ref, kp_ref, vc_ref, vp_ref, oin_ref, lin_ref = refs[:7]
        rest = refs[7:]
    if last:
        (o_ref,) = rest
    else:
        o_ref, l_ref = rest

    nb = pl.program_id(2)
    lane = lax.broadcasted_iota(jnp.int32, (A_SPAN, LANES), 1)
    low = lane < A_DIM
    qi = lax.broadcasted_iota(jnp.int32, (2 * A_SPAN, 2 * A_SPAN), 0) & (A_SPAN - 1)
    ki = lax.broadcasted_iota(jnp.int32, (2 * A_SPAN, 2 * A_SPAN), 1)
    dist = qi + A_SPAN - ki
    band = (dist >= 0) & (dist <= A_SPAN)
    band0 = band & (ki >= jnp.where(nb > 0, 0, A_SPAN))
    zero = jnp.zeros((), BF16)

    for p in range(A_WIDTH // LANES):
        cols = slice(p * LANES, (p + 1) * LANES)
        for g in range(tq // A_SPAN):
            rows = slice(g * A_SPAN, (g + 1) * A_SPAN)
            qg = q_ref[rows, cols]
            qs = jnp.concatenate([jnp.where(low, qg, zero), jnp.where(low, zero, qg)], axis=0)
            if g == 0:
                kk = jnp.concatenate([kp_ref[:, cols], kc_ref[0:A_SPAN, cols]], axis=0)
                vv = jnp.concatenate([vp_ref[:, cols], vc_ref[0:A_SPAN, cols]], axis=0)
                valid = band0
            else:
                kk = kc_ref[(g - 1) * A_SPAN:(g + 1) * A_SPAN, cols]
                vv = vc_ref[(g - 1) * A_SPAN:(g + 1) * A_SPAN, cols]
                valid = band
            s = jnp.where(valid, _nt_dot(qs, kk), NEG)
            m = jnp.max(s, axis=1, keepdims=True)
            e = jnp.exp(s - m)
            l = jnp.sum(e, axis=1, keepdims=True)
            pv = jnp.dot(e.astype(BF16), vv, preferred_element_type=F32)
            o2 = pv / l
            lse2 = m + jnp.log(l)
            o = jnp.where(low, o2[:A_SPAN], o2[A_SPAN:])
            lse = jnp.where(low, lse2[:A_SPAN], lse2[A_SPAN:])
            if not first:
                o_prev = oin_ref[rows, cols]
                l_prev = lin_ref[rows, cols]
                mx = jnp.maximum(lse, l_prev)
                wa = jnp.exp(lse - mx)
                wb = jnp.exp(l_prev - mx)
                tot = wa + wb
                o = (wa * o + wb * o_prev) / tot
                lse = mx + jnp.log(tot)
            o_ref[rows, cols] = o.astype(o_ref.dtype)
            if not last:
                l_ref[rows, cols] = lse


def _dilated_call(q, k, v, state, *, dilation, first, last, tq):
    bsz, seq, width = q.shape
    sub = seq // dilation
    view = lambda t: t.reshape(bsz, sub, dilation * width)
    nblk = sub // tq
    per = tq // A_SPAN
    cur = lambda b, r, nb: (b, nb, r)
    prev = lambda b, r, nb: (b, jnp.maximum(nb * per - 1, 0), r)
    blk = pl.BlockSpec((None, tq, width), cur)
    pblk = pl.BlockSpec((None, A_SPAN, width), prev)
    in_specs = [blk, blk, pblk, blk, pblk]
    args = [view(q), view(k), view(k), view(v), view(v)]
    if not first:
        in_specs += [blk, blk]
        args += [view(state[0]), view(state[1])]
    if last:
        out_specs = [blk]
        out_shape = [jax.ShapeDtypeStruct((bsz, sub, dilation * width), BF16)]
    else:
        out_specs = [blk, blk]
        out_shape = [jax.ShapeDtypeStruct((bsz, sub, dilation * width), F32)] * 2
    outs = pl.pallas_call(
        functools.partial(_dilated_kernel, tq=tq, first=first, last=last),
        grid=(bsz, dilation, nblk),
        in_specs=in_specs,
        out_specs=out_specs,
        out_shape=out_shape,
        compiler_params=pltpu.CompilerParams(
            dimension_semantics=("arbitrary",) * 3, vmem_limit_bytes=VMEM_LIMIT),
        name=f"dilated_d{dilation}",
    )(*args)
    return [o.reshape(bsz, seq, width) for o in outs]


def _dilated_mixer(q, k, v):
    state = None
    n = len(A_CONFIGS)
    for idx, (window, dilation) in enumerate(A_CONFIGS):
        assert window // dilation == A_SPAN
        sub = q.shape[1] // dilation
        tq = min(512, sub)
        state = _dilated_call(q, k, v, state, dilation=dilation, first=idx == 0,
                              last=idx == n - 1, tq=tq)
    return state[0]


def _flash_scratch(rows, tq):
    tk = tq // 2
    return [pltpu.VMEM((1, rows), F32), pltpu.VMEM((ACC_ROWS, rows), F32),
            pltpu.VMEM((tk, rows), F32), pltpu.VMEM((tk, rows), F32),
            pltpu.VMEM((tk, rows), BF16), pltpu.VMEM((tk, rows), BF16)]


def _flash_cols(qs, k_block, vt_block, scratch, *, qi, tq):
    m_ref, acc_ref, st_a, st_b, pt_a, pt_b = scratch
    tk = tq // 2
    ones = jnp.ones((ACC_ROWS - LANES, tk), BF16)
    m_ref[...] = jnp.full(m_ref.shape, NEG, F32)

    def scores(blk, key_offset=None):
        st = _nt_dot(k_block(blk), qs)
        if key_offset is not None:
            key = lax.broadcasted_iota(jnp.int32, st.shape, 0) + key_offset
            qry = lax.broadcasted_iota(jnp.int32, st.shape, 1) & (tq - 1)
            st = jnp.where(key <= qry, st, NEG)
        return st

    def softmax(st_ref, pt_ref):
        st = st_ref[...]
        m_prev = m_ref[...]
        m_new = jnp.maximum(m_prev, jnp.max(st, axis=0, keepdims=True))
        m_ref[...] = m_new
        pt_ref[...] = jnp.exp2(st - m_new).astype(BF16)
        return jnp.exp2(m_prev - m_new)

    def values(blk, pt_ref):
        lhs = jnp.concatenate([vt_block(blk), ones], axis=0)
        return jnp.dot(lhs, pt_ref[...], preferred_element_type=F32)

    diag0 = 2 * qi
    st_a[...] = scores(diag0, 0)
    st_b[...] = scores(diag0 + 1, tk)
    softmax(st_a, pt_a)
    acc_ref[...] = jnp.zeros(acc_ref.shape, F32)

    def body(j, carry):
        prev = jnp.where(j == 0, diag0, 2 * j - 2)
        st_a[...] = scores(2 * j)
        alpha = softmax(st_b, pt_b)
        acc_ref[...] = alpha * (acc_ref[...] + values(prev, pt_a))
        st_b[...] = scores(2 * j + 1)
        alpha = softmax(st_a, pt_a)
        acc_ref[...] = alpha * (acc_ref[...] + values(prev + 1, pt_b))
        return carry

    lax.fori_loop(0, qi, body, 0)

    last = jnp.where(qi == 0, 0, 2 * qi - 2)
    alpha = softmax(st_b, pt_b)
    acc_ref[...] = alpha * (acc_ref[...] + values(last, pt_a))
    acc_ref[...] += values(last + 1, pt_b)
    return acc_ref[:LANES, :] / acc_ref[LANES:LANES + 1, :]


def _blocked_vt(v, tk):
    bsz, seq, width = v.shape
    v5 = v.reshape(bsz, seq // tk, tk, width // LANES, LANES)
    return jnp.transpose(v5, (0, 3, 1, 4, 2))


def _mla_kernel(qn_ref, qr_ref, kn_ref, kr_ref, vt_ref, o_ref, *scratch, tq):
    qi = pl.program_id(2)
    tk = tq // 2
    lane = lax.broadcasted_iota(jnp.int32, (tq, LANES), 1)
    zero = jnp.zeros((), BF16)
    qn, qr = qn_ref[...], qr_ref[...]
    q0 = jnp.concatenate([jnp.where(lane < B_NOPE, qn, zero),
                          jnp.where(lane < B_ROPE, qr, zero)], axis=1)
    q1 = jnp.concatenate([jnp.where(lane >= B_NOPE, qn, zero),
                          jnp.where((lane >= B_ROPE) & (lane < 2 * B_ROPE), qr, zero)], axis=1)
    qs = jnp.concatenate([q0, q1], axis=0)

    def k_block(i):
        rows = pl.ds(pl.multiple_of(i * tk, tk), tk)
        return jnp.concatenate([kn_ref[rows, :], kr_ref[rows, :]], axis=1)

    def vt_block(i):
        return vt_ref[i]

    ot = _flash_cols(qs, k_block, vt_block, scratch, qi=qi, tq=tq)
    pair_t = jnp.concatenate([ot[:B_V, :tq], ot[B_V:, tq:]], axis=0)
    o_ref[...] = pair_t.T.astype(o_ref.dtype)


def _mla_attn(qn, qr, kn, kr, v, *, tq):
    bsz, seq, width = qn.shape
    pairs = width // LANES
    tk = tq // 2
    vt = _blocked_vt(v, tk)
    qblk = pl.BlockSpec((None, tq, LANES), lambda b, p, i: (b, i, p))
    kblk = pl.BlockSpec((None, seq, LANES), lambda b, p, i: (b, 0, p))
    krblk = pl.BlockSpec((None, seq, LANES), lambda b, p, i: (b, 0, 0))
    vtblk = pl.BlockSpec((None, None, seq // tk, LANES, tk), lambda b, p, i: (b, p, 0, 0, 0))
    return pl.pallas_call(
        functools.partial(_mla_kernel, tq=tq),
        grid=(bsz, pairs, seq // tq),
        in_specs=[qblk, qblk, kblk, krblk, vtblk],
        out_specs=qblk,
        out_shape=jax.ShapeDtypeStruct((bsz, seq, width), BF16),
        scratch_shapes=_flash_scratch(2 * tq, tq),
        compiler_params=pltpu.CompilerParams(
            dimension_semantics=("arbitrary",) * 3, vmem_limit_bytes=VMEM_LIMIT),
        name="mla_attn",
    )(qn, qr, kn, kr, vt)


def _diff_kernel(lam_ref, g_ref, q_ref, k_ref, vt_ref, o_ref, *scratch, tq, lam_init):
    qi = pl.program_id(2)
    tk = tq // 2
    lane = lax.broadcasted_iota(jnp.int32, (tq, LANES), 1)
    zero = jnp.zeros((), BF16)
    q = q_ref[...]
    qs = jnp.concatenate(
        [jnp.where((lane >= j * C_QK) & (lane < (j + 1) * C_QK), q, zero) for j in range(4)],
        axis=0)

    def k_block(i):
        return k_ref[pl.ds(pl.multiple_of(i * tk, tk), tk), :]

    def vt_block(i):
        return vt_ref[i]

    ot = _flash_cols(qs, k_block, vt_block, scratch, qi=qi, tq=tq)

    lp = lam_ref[...]
    lam = (jnp.exp(jnp.sum(lp[0:1] * lp[1:2], axis=1, keepdims=True))
           - jnp.exp(jnp.sum(lp[2:3] * lp[3:4], axis=1, keepdims=True)) + lam_init)
    d0 = ot[:C_V, 0:tq] - lam * ot[:C_V, tq:2 * tq]
    d1 = ot[C_V:, 2 * tq:3 * tq] - lam * ot[C_V:, 3 * tq:]
    d = jnp.concatenate([d0, d1], axis=0).T
    low = lane < C_V
    sq = d * d
    ms_lo = jnp.sum(jnp.where(low, sq, 0.0), axis=1, keepdims=True) * (1.0 / C_V)
    ms_hi = jnp.sum(jnp.where(low, 0.0, sq), axis=1, keepdims=True) * (1.0 / C_V)
    rs = jnp.where(low, lax.rsqrt(ms_lo + SUBLN_EPS), lax.rsqrt(ms_hi + SUBLN_EPS))
    o_ref[...] = ((d * rs * g_ref[...]) * (1.0 - lam_init)).astype(o_ref.dtype)


def _diff_attn(lam_params, g2, q, k, v, *, tq, lam_init):
    bsz, seq, width = q.shape
    pairs = width // LANES
    tk = tq // 2
    vt = _blocked_vt(v, tk)
    qblk = pl.BlockSpec((None, tq, LANES), lambda b, p, i: (b, i, p))
    kblk = pl.BlockSpec((None, seq, LANES), lambda b, p, i: (b, 0, p))
    vtblk = pl.BlockSpec((None, None, seq // tk, LANES, tk), lambda b, p, i: (b, p, 0, 0, 0))
    const = lambda b, p, i: (0, 0)
    return pl.pallas_call(
        functools.partial(_diff_kernel, tq=tq, lam_init=lam_init),
        grid=(bsz, pairs, seq // tq),
        in_specs=[pl.BlockSpec(lam_params.shape, const), pl.BlockSpec(g2.shape, const),
                  qblk, kblk, vtblk],
        out_specs=qblk,
        out_shape=jax.ShapeDtypeStruct((bsz, seq, width), BF16),
        scratch_shapes=_flash_scratch(4 * tq, tq),
        compiler_params=pltpu.CompilerParams(
            dimension_semantics=("arbitrary",) * 3, vmem_limit_bytes=VMEM_LIMIT),
        name="diff_attn",
    )(lam_params, g2, q, k, vt)


def _post_kernel(x_ref, oa_ref, ob_ref, oc_ref, wo_ref, g2_ref, wup_ref, wdn_ref, gf_ref,
                 y_ref, *, final):
    mixed = jnp.concatenate([oa_ref[...], ob_ref[...], oc_ref[...]], axis=1)
    x1 = x_ref[...] + jnp.dot(mixed, wo_ref[...], preferred_element_type=F32)
    hn = _rms(x1, g2_ref[...], NORM_EPS).astype(BF16)
    y_ref[...] = x1
    for c in range(wup_ref.shape[1] // D_FF_CHUNK):
        cs = slice(c * D_FF_CHUNK, (c + 1) * D_FF_CHUNK)
        u = jnp.dot(hn, wup_ref[:, cs], preferred_element_type=F32)
        a = jnp.square(jnp.maximum(u, 0.0)).astype(BF16)
        y_ref[...] += jnp.dot(a, wdn_ref[cs, :], preferred_element_type=F32)
    if final:
        y_ref[...] = _rms(y_ref[...], gf_ref[...], NORM_EPS)


def _post(x2, oa, ob, oc, wo, g2, wup, wdn, gf, *, tm, final):
    n, d = x2.shape
    row = lambda i: (i, 0)
    const = lambda i: (0, 0)
    return pl.pallas_call(
        functools.partial(_post_kernel, final=final),
        grid=(n // tm,),
        in_specs=[pl.BlockSpec((tm, d), row),
                  pl.BlockSpec((tm, oa.shape[1]), row),
                  pl.BlockSpec((tm, ob.shape[1]), row),
                  pl.BlockSpec((tm, oc.shape[1]), row),
                  pl.BlockSpec(wo.shape, const), pl.BlockSpec(g2.shape, const),
                  pl.BlockSpec(wup.shape, const), pl.BlockSpec(wdn.shape, const),
                  pl.BlockSpec(gf.shape, const)],
        out_specs=pl.BlockSpec((tm, d), row),
        out_shape=jax.ShapeDtypeStruct((n, d), F32),
        compiler_params=pltpu.CompilerParams(
            dimension_semantics=("arbitrary",), vmem_limit_bytes=VMEM_LIMIT),
        name="post_final" if final else "post",
    )(x2, oa, ob, oc, wo, g2, wup, wdn, gf)


def _rope_tables(seq):
    def tables(dim):
        half = dim // 2
        inv = 1.0 / (ROPE_THETA ** (jnp.arange(half, dtype=F32) / half))
        ang = jnp.arange(seq, dtype=F32)[:, None] * inv[None, :]
        cos, sin = jnp.cos(ang), jnp.sin(ang)
        reps = LANES // dim
        cos_t = jnp.tile(jnp.concatenate([cos, cos], axis=1), (1, reps))
        sin_t = jnp.tile(jnp.concatenate([-sin, sin], axis=1), (1, reps))
        return cos_t, sin_t
    cosa, sina = tables(A_DIM)
    cos32, sin32 = tables(C_QK)
    return cosa, sina, cos32, sin32


def _prep_layer_weights(w_in, w_uq, w_ukv):
    cols = jnp.split(w_in, [384, 768, 1152, 1536, 1664, 1696, 1952, 2208], axis=1)
    qa, ka, va, cq, ckv, kr, qc, kc, vc = cols
    w1 = jnp.concatenate([qa, ka, va, cq, ckv, jnp.tile(kr, (1, LANES // B_ROPE)), qc, kc, vc],
                         axis=1).astype(BF16)
    uq = w_uq.reshape(B_Q_LORA, B_HEADS, B_NOPE + B_ROPE)
    nope = uq[:, :, :B_NOPE].reshape(B_Q_LORA, B_HEADS * B_NOPE)
    rope = uq[:, :, B_NOPE:].reshape(B_Q_LORA, B_HEADS // 2, 2 * B_ROPE)
    rope = jnp.pad(rope, ((0, 0), (0, 0), (0, LANES - 2 * B_ROPE))).reshape(B_Q_LORA, -1)
    wuq = jnp.concatenate([nope, rope], axis=1).astype(BF16)
    ukv = w_ukv.reshape(B_KV_LORA, B_HEADS, B_NOPE + B_V)
    wukv = jnp.concatenate([ukv[:, :, :B_NOPE].reshape(B_KV_LORA, -1),
                            ukv[:, :, B_NOPE:].reshape(B_KV_LORA, -1)], axis=1).astype(BF16)
    return w1, wuq, wukv


def kernel(x, ln1_g, w_in, mla_q_norm_g, mla_w_uq, mla_kv_norm_g, mla_w_ukv, diff_lambda,
           diff_subln_g, w_o, ln2_g, w_up, w_down, final_g):
    bsz, seq, d_model = x.shape
    depth = w_in.shape[0]
    tm = 512
    tq = 512
    tabs = _rope_tables(seq)
    x2 = x.reshape(bsz * seq, d_model)
    gf = final_g.reshape(1, d_model)
    sh = lambda t, w: t.reshape(bsz, seq, w)

    for layer in range(depth):
        lam_init = 0.8 - 0.6 * math.exp(-0.3 * layer)
        w1, wuq, wukv = _prep_layer_weights(w_in[layer], mla_w_uq[layer], mla_w_ukv[layer])
        qa, ka, va, qn, qr, kn, vb, kr, qc, kc, vc = _inproj(
            x2, ln1_g[layer].reshape(1, -1), w1, tabs,
            mla_q_norm_g[layer].reshape(1, -1), wuq,
            mla_kv_norm_g[layer].reshape(1, -1), wukv, seq=seq, tm=tm)

        out_a = _dilated_mixer(sh(qa, 384), sh(ka, 384), sh(va, 384))
        out_b = _mla_attn(sh(qn, 384), sh(qr, 384), sh(kn, 384), sh(kr, 128), sh(vb, 384), tq=tq)
        g_sub = jnp.tile(diff_subln_g[layer], LANES // C_V).reshape(1, LANES)
        out_c = _diff_attn(diff_lambda[layer], g_sub, sh(qc, 256), sh(kc, 256), sh(vc, 256),
                           tq=tq, lam_init=lam_init)

        x2 = _post(x2, out_a.reshape(-1, A_WIDTH), out_b.reshape(-1, B_WIDTH),
                   out_c.reshape(-1, C_WIDTH), w_o[layer].astype(BF16),
                   ln2_g[layer].reshape(1, -1), w_up[layer].astype(BF16),
                   w_down[layer].astype(BF16), gf, tm=tm, final=layer == depth - 1)

    return x2.reshape(bsz, seq, d_model)
```

```python
import functools
import math

import jax
import jax.numpy as jnp
from jax import lax
from jax.experimental import pallas as pl
from jax.experimental.pallas import tpu as pltpu

F32 = jnp.float32
BF16 = jnp.bfloat16

LANES = 128
ACC_ROWS = LANES + 16
VMEM_LIMIT = 56 * 1024 * 1024

ROPE_THETA = 10000.0
NORM_EPS = 1e-6
SUBLN_EPS = 1e-5
NEG = -1e30
LOG2E = math.log2(math.e)

A_HEADS, A_DIM = 6, 64
A_CONFIGS = ((128, 1), (512, 4), (2048, 16))
A_SPAN = 128
B_HEADS, B_NOPE, B_ROPE, B_V = 6, 64, 32, 64
B_Q_LORA, B_KV_LORA = 384, 128
C_HEADS, C_QK = 4, 32
C_V = 2 * C_QK
A_WIDTH = A_HEADS * A_DIM
B_WIDTH = B_HEADS * B_V
C_WIDTH = C_HEADS * C_V
D_FF_CHUNK = 512


def _rms(x, g, eps):
    return x * lax.rsqrt(jnp.mean(x * x, axis=-1, keepdims=True) + eps) * g


def _rope(x, cos, sin_signed, half):
    lane = lax.broadcasted_iota(jnp.int32, (x.shape[0], LANES), 1)
    first = (lane & (2 * half - 1)) < half
    outs = []
    for c in range(x.shape[1] // LANES):
        xc = x[:, c * LANES:(c + 1) * LANES]
        partner = jnp.where(first, pltpu.roll(xc, LANES - half, 1), pltpu.roll(xc, half, 1))
        outs.append(xc * cos + partner * sin_signed)
    return outs[0] if len(outs) == 1 else jnp.concatenate(outs, axis=1)


def _nt_dot(a, b):
    return lax.dot_general(a, b, (((1,), (1,)), ((), ())), preferred_element_type=F32)


def _inproj_kernel(x_ref, g_ref, w1_ref, cosa_ref, sina_ref, cos32_ref, sin32_ref,
                   qg_ref, wuq_ref, kvg_ref, wukv_ref,
                   qa_ref, ka_ref, va_ref, qn_ref, qr_ref, kn_ref, vb_ref, kr_ref,
                   qc_ref, kc_ref, vc_ref):
    h = _rms(x_ref[...], g_ref[...], NORM_EPS).astype(BF16)

    def proj(lo, width):
        return jnp.dot(h, w1_ref[:, lo:lo + width], preferred_element_type=F32)

    cosa, sina = cosa_ref[...], sina_ref[...]
    cos32, sin32 = cos32_ref[...], sin32_ref[...]
    scale_a = A_DIM ** -0.5
    scale_b = (B_NOPE + B_ROPE) ** -0.5 * LOG2E
    scale_c = C_QK ** -0.5 * LOG2E

    qa_ref[...] = (_rope(proj(0, 384), cosa, sina, 32) * scale_a).astype(BF16)
    ka_ref[...] = _rope(proj(384, 384), cosa, sina, 32).astype(BF16)
    va_ref[...] = proj(768, 384).astype(BF16)

    cq = _rms(proj(1152, 384), qg_ref[...], NORM_EPS).astype(BF16)
    qf = jnp.dot(cq, wuq_ref[...], preferred_element_type=F32)
    qn_ref[...] = (qf[:, :384] * scale_b).T.astype(BF16)
    qr_ref[...] = (_rope(qf[:, 384:], cos32, sin32, 16) * scale_b).T.astype(BF16)

    ckv = _rms(proj(1536, 128), kvg_ref[...], NORM_EPS).astype(BF16)
    kvf = jnp.dot(ckv, wukv_ref[...], preferred_element_type=F32)
    kn_ref[...] = kvf[:, :384].astype(BF16)
    vb_ref[...] = kvf[:, 384:].T.astype(BF16)
    kr_ref[...] = _rope(proj(1664, 128), cos32, sin32, 16).astype(BF16)

    qc_ref[...] = (_rope(proj(1792, 256), cos32, sin32, 16) * scale_c).T.astype(BF16)
    kc_ref[...] = _rope(proj(2048, 256), cos32, sin32, 16).astype(BF16)
    vc_ref[...] = proj(2304, 256).T.astype(BF16)


def _inproj(x2, g, w1, tabs, qg, wuq, kvg, wukv, *, seq, tm):
    n = x2.shape[0]
    nt = n // tm
    st = seq // tm
    row = lambda i: (i, 0)
    const = lambda i: (0, 0)
    tab = lambda i: (i % st, 0)
    widths = (384, 384, 384, 384, 384, 384, 384, 128, 256, 256, 256)
    major = (False, False, False, True, True, False, True, False, True, False, True)
    bsz = n // seq
    col = lambda i: (i // st, 0, i % st)
    out_specs = [pl.BlockSpec((None, w, tm), col) if t else pl.BlockSpec((tm, w), row)
                 for w, t in zip(widths, major)]
    out_shape = [jax.ShapeDtypeStruct((bsz, w, seq) if t else (n, w), BF16)
                 for w, t in zip(widths, major)]
    return pl.pallas_call(
        _inproj_kernel,
        grid=(nt,),
        in_specs=[
            pl.BlockSpec((tm, x2.shape[1]), row),
            pl.BlockSpec(g.shape, const),
            pl.BlockSpec(w1.shape, const),
            pl.BlockSpec((tm, LANES), tab), pl.BlockSpec((tm, LANES), tab),
            pl.BlockSpec((tm, LANES), tab), pl.BlockSpec((tm, LANES), tab),
            pl.BlockSpec(qg.shape, const), pl.BlockSpec(wuq.shape, const),
            pl.BlockSpec(kvg.shape, const), pl.BlockSpec(wukv.shape, const),
        ],
        out_specs=out_specs,
        out_shape=out_shape,
        compiler_params=pltpu.CompilerParams(
            dimension_semantics=("arbitrary",), vmem_limit_bytes=VMEM_LIMIT),
        name="inproj",
    )(x2, g, w1, *tabs, qg, wuq, kvg, wukv)


def _dilated_kernel(*refs, tq, first, last):
    if first:
        q_ref, kc_ref, kp_ref, vc_ref, vp_ref = refs[:5]
        rest = refs[5:]
    else:
        q_ref, kc_ref, kp_ref, vc_ref, vp_ref, oin_ref, lin_ref = refs[:7]
        rest = refs[7:]
    if last:
        (o_ref,) = rest
    else:
        o_ref, l_ref = rest

    nb = pl.program_id(2)
    lane = lax.broadcasted_iota(jnp.int32, (A_SPAN, LANES), 1)
    low = lane < A_DIM
    qi = lax.broadcasted_iota(jnp.int32, (2 * A_SPAN, 2 * A_SPAN), 0) & (A_SPAN - 1)
    ki = lax.broadcasted_iota(jnp.int32, (2 * A_SPAN, 2 * A_SPAN), 1)
    dist = qi + A_SPAN - ki
    band = (dist >= 0) & (dist <= A_SPAN)
    band0 = band & (ki >= jnp.where(nb > 0, 0, A_SPAN))
    zero = jnp.zeros((), BF16)

    for p in range(A_WIDTH // LANES):
        cols = slice(p * LANES, (p + 1) * LANES)
        for g in range(tq // A_SPAN):
            rows = slice(g * A_SPAN, (g + 1) * A_SPAN)
            qg = q_ref[rows, cols]
            qs = jnp.concatenate([jnp.where(low, qg, zero), jnp.where(low, zero, qg)], axis=0)
            if g == 0:
                kk = jnp.concatenate([kp_ref[:, cols], kc_ref[0:A_SPAN, cols]], axis=0)
                vv = jnp.concatenate([vp_ref[:, cols], vc_ref[0:A_SPAN, cols]], axis=0)
                valid = band0
            else:
                kk = kc_ref[(g - 1) * A_SPAN:(g + 1) * A_SPAN, cols]
                vv = vc_ref[(g - 1) * A_SPAN:(g + 1) * A_SPAN, cols]
                valid = band
            s = jnp.where(valid, _nt_dot(qs, kk), NEG)
            m = jnp.max(s, axis=1, keepdims=True)
            e = jnp.exp(s - m)
            l = jnp.sum(e, axis=1, keepdims=True)
            pv = jnp.dot(e.astype(BF16), vv, preferred_element_type=F32)
            o2 = pv / l
            lse2 = m + jnp.log(l)
            o = jnp.where(low, o2[:A_SPAN], o2[A_SPAN:])
            lse = jnp.where(low, lse2[:A_SPAN], lse2[A_SPAN:])
            if not first:
                o_prev = oin_ref[rows, cols]
                l_prev = lin_ref[rows, cols]
                mx = jnp.maximum(lse, l_prev)
                wa = jnp.exp(lse - mx)
                wb = jnp.exp(l_prev - mx)
                tot = wa + wb
                o = (wa * o + wb * o_prev) / tot
                lse = mx + jnp.log(tot)
            o_ref[rows, cols] = o.astype(o_ref.dtype)
            if not last:
                l_ref[rows, cols] = lse


def _dilated_call(q, k, v, state, *, dilation, first, last, tq):
    bsz, seq, width = q.shape
    sub = seq // dilation
    view = lambda t: t.reshape(bsz, sub, dilation * width)
    nblk = sub // tq
    per = tq // A_SPAN
    cur = lambda b, r, nb: (b, nb, r)
    prev = lambda b, r, nb: (b, jnp.maximum(nb * per - 1, 0), r)
    blk = pl.BlockSpec((None, tq, width), cur)
    pblk = pl.BlockSpec((None, A_SPAN, width), prev)
    in_specs = [blk, blk, pblk, blk, pblk]
    args = [view(q), view(k), view(k), view(v), view(v)]
    if not first:
        in_specs += [blk, blk]
        args += [view(state[0]), view(state[1])]
    if last:
        out_specs = [blk]
        out_shape = [jax.ShapeDtypeStruct((bsz, sub, dilation * width), BF16)]
    else:
        out_specs = [blk, blk]
        out_shape = [jax.ShapeDtypeStruct((bsz, sub, dilation * width), F32)] * 2
    outs = pl.pallas_call(
        functools.partial(_dilated_kernel, tq=tq, first=first, last=last),
        grid=(bsz, dilation, nblk),
        in_specs=in_specs,
        out_specs=out_specs,
        out_shape=out_shape,
        compiler_params=pltpu.CompilerParams(
            dimension_semantics=("arbitrary",) * 3, vmem_limit_bytes=VMEM_LIMIT),
        name=f"dilated_d{dilation}",
    )(*args)
    return [o.reshape(bsz, seq, width) for o in outs]


def _dilated_mixer(q, k, v):
    state = None
    n = len(A_CONFIGS)
    for idx, (window, dilation) in enumerate(A_CONFIGS):
        assert window // dilation == A_SPAN
        sub = q.shape[1] // dilation
        tq = min(512, sub)
        state = _dilated_call(q, k, v, state, dilation=dilation, first=idx == 0,
                              last=idx == n - 1, tq=tq)
    return state[0]


def _flash_scratch(rows, tq):
    tk = tq // 2
    score_bufs = [pltpu.VMEM((tk, rows), F32), pltpu.VMEM((1, rows), F32),
                  pltpu.VMEM((tk, rows), BF16)]
    return [pltpu.VMEM((1, rows), F32), pltpu.VMEM((ACC_ROWS, rows), F32)] + 2 * score_bufs


def _flash_cols(qst, k_block, vt_block, scratch, *, qi, tq):
    m_ref, acc_ref, st_a, cm_a, pt_a, st_b, cm_b, pt_b = scratch
    tk = tq // 2
    ones = jnp.ones((ACC_ROWS - LANES, tk), BF16)
    m_ref[...] = jnp.full(m_ref.shape, NEG, F32)

    def scores(blk, st_ref, cm_ref, key_offset=None):
        st = jnp.dot(k_block(blk), qst, preferred_element_type=F32)
        if key_offset is not None:
            key = lax.broadcasted_iota(jnp.int32, st.shape, 0) + key_offset
            qry = lax.broadcasted_iota(jnp.int32, st.shape, 1) & (tq - 1)
            st = jnp.where(key <= qry, st, NEG)
        st_ref[...] = st
        cm_ref[...] = jnp.max(st, axis=0, keepdims=True)

    def softmax(st_ref, cm_ref, pt_ref):
        m_prev = m_ref[...]
        m_new = jnp.maximum(m_prev, cm_ref[...])
        m_ref[...] = m_new
        pt_ref[...] = jnp.exp2(st_ref[...] - m_new).astype(BF16)
        return jnp.exp2(m_prev - m_new)

    def values(blk, pt_ref):
        lhs = jnp.concatenate([vt_block(blk), ones], axis=0)
        return jnp.dot(lhs, pt_ref[...], preferred_element_type=F32)

    diag0 = 2 * qi
    scores(diag0, st_a, cm_a, 0)
    scores(diag0 + 1, st_b, cm_b, tk)
    softmax(st_a, cm_a, pt_a)
    acc_ref[...] = jnp.zeros(acc_ref.shape, F32)

    def body(j, carry):
        prev = jnp.where(j == 0, diag0, 2 * j - 2)
        alpha = softmax(st_b, cm_b, pt_b)
        scores(2 * j, st_a, cm_a)
        acc_ref[...] = alpha * (acc_ref[...] + values(prev, pt_a))
        alpha = softmax(st_a, cm_a, pt_a)
        scores(2 * j + 1, st_b, cm_b)
        acc_ref[...] = alpha * (acc_ref[...] + values(prev + 1, pt_b))
        return carry

    lax.fori_loop(0, qi, body, 0)

    last = jnp.where(qi == 0, 0, 2 * qi - 2)
    alpha = softmax(st_b, cm_b, pt_b)
    acc_ref[...] = alpha * (acc_ref[...] + values(last, pt_a))
    acc_ref[...] += values(last + 1, pt_b)
    return acc_ref[:LANES, :] / acc_ref[LANES:LANES + 1, :]


def _mla_kernel(qn_ref, qr_ref, kn_ref, kr_ref, vt_ref, o_ref, *scratch, tq):
    qi = pl.program_id(2)
    tk = tq // 2
    row = lax.broadcasted_iota(jnp.int32, (LANES, tq), 0)
    zero = jnp.zeros((), BF16)
    qn, qr = qn_ref[...], qr_ref[...]
    q0 = jnp.concatenate([jnp.where(row < B_NOPE, qn, zero),
                          jnp.where(row < B_ROPE, qr, zero)], axis=0)
    q1 = jnp.concatenate([jnp.where(row >= B_NOPE, qn, zero),
                          jnp.where((row >= B_ROPE) & (row < 2 * B_ROPE), qr, zero)], axis=0)
    qst = jnp.concatenate([q0, q1], axis=1)

    def k_block(i):
        rows = pl.ds(pl.multiple_of(i * tk, tk), tk)
        return jnp.concatenate([kn_ref[rows, :], kr_ref[rows, :]], axis=1)

    def vt_block(i):
        return vt_ref[:, pl.ds(pl.multiple_of(i * tk, tk), tk)]

    ot = _flash_cols(qst, k_block, vt_block, scratch, qi=qi, tq=tq)
    pair_t = jnp.concatenate([ot[:B_V, :tq], ot[B_V:, tq:]], axis=0)
    o_ref[...] = pair_t.T.astype(o_ref.dtype)


def _mla_attn(qnt, qrt, kn, kr, vt, *, tq):
    bsz, seq, width = kn.shape
    pairs = width // LANES
    qblk = pl.BlockSpec((None, LANES, tq), lambda b, p, i: (b, p, i))
    kblk = pl.BlockSpec((None, seq, LANES), lambda b, p, i: (b, 0, p))
    krblk = pl.BlockSpec((None, seq, LANES), lambda b, p, i: (b, 0, 0))
    vtblk = pl.BlockSpec((None, LANES, seq), lambda b, p, i: (b, p, 0))
    return pl.pallas_call(
        functools.partial(_mla_kernel, tq=tq),
        grid=(bsz, pairs, seq // tq),
        in_specs=[qblk, qblk, kblk, krblk, vtblk],
        out_specs=pl.BlockSpec((None, tq, LANES), lambda b, p, i: (b, i, p)),
        out_shape=jax.ShapeDtypeStruct((bsz, seq, width), BF16),
        scratch_shapes=_flash_scratch(2 * tq, tq),
        compiler_params=pltpu.CompilerParams(
            dimension_semantics=("arbitrary",) * 3, vmem_limit_bytes=VMEM_LIMIT),
        name="mla_attn",
    )(qnt, qrt, kn, kr, vt)


def _diff_kernel(lam_ref, g_ref, q_ref, k_ref, vt_ref, o_ref, *scratch, tq, lam_init):
    qi = pl.program_id(2)
    tk = tq // 2
    lane = lax.broadcasted_iota(jnp.int32, (tq, LANES), 1)
    row = lax.broadcasted_iota(jnp.int32, (LANES, tq), 0)
    zero = jnp.zeros((), BF16)
    q = q_ref[...]
    qst = jnp.concatenate(
        [jnp.where((row >= j * C_QK) & (row < (j + 1) * C_QK), q, zero) for j in range(4)],
        axis=1)

    def k_block(i):
        return k_ref[pl.ds(pl.multiple_of(i * tk, tk), tk), :]

    def vt_block(i):
        return vt_ref[:, pl.ds(pl.multiple_of(i * tk, tk), tk)]

    ot = _flash_cols(qst, k_block, vt_block, scratch, qi=qi, tq=tq)

    lp = lam_ref[...]
    lam = (jnp.exp(jnp.sum(lp[0:1] * lp[1:2], axis=1, keepdims=True))
           - jnp.exp(jnp.sum(lp[2:3] * lp[3:4], axis=1, keepdims=True)) + lam_init)
    d0 = ot[:C_V, 0:tq] - lam * ot[:C_V, tq:2 * tq]
    d1 = ot[C_V:, 2 * tq:3 * tq] - lam * ot[C_V:, 3 * tq:]
    d = jnp.concatenate([d0, d1], axis=0).T
    low = lane < C_V
    sq = d * d
    ms_lo = jnp.sum(jnp.where(low, sq, 0.0), axis=1, keepdims=True) * (1.0 / C_V)
    ms_hi = jnp.sum(jnp.where(low, 0.0, sq), axis=1, keepdims=True) * (1.0 / C_V)
    rs = jnp.where(low, lax.rsqrt(ms_lo + SUBLN_EPS), lax.rsqrt(ms_hi + SUBLN_EPS))
    o_ref[...] = ((d * rs * g_ref[...]) * (1.0 - lam_init)).astype(o_ref.dtype)


def _diff_attn(lam_params, g2, qt, k, vt, *, tq, lam_init):
    bsz, seq, width = k.shape
    pairs = width // LANES
    qblk = pl.BlockSpec((None, LANES, tq), lambda b, p, i: (b, p, i))
    kblk = pl.BlockSpec((None, seq, LANES), lambda b, p, i: (b, 0, p))
    vtblk = pl.BlockSpec((None, LANES, seq), lambda b, p, i: (b, p, 0))
    const = lambda b, p, i: (0, 0)
    return pl.pallas_call(
        functools.partial(_diff_kernel, tq=tq, lam_init=lam_init),
        grid=(bsz, pairs, seq // tq),
        in_specs=[pl.BlockSpec(lam_params.shape, const), pl.BlockSpec(g2.shape, const),
                  qblk, kblk, vtblk],
        out_specs=pl.BlockSpec((None, tq, LANES), lambda b, p, i: (b, i, p)),
        out_shape=jax.ShapeDtypeStruct((bsz, seq, width), BF16),
        scratch_shapes=_flash_scratch(4 * tq, tq),
        compiler_params=pltpu.CompilerParams(
            dimension_semantics=("arbitrary",) * 3, vmem_limit_bytes=VMEM_LIMIT),
        name="diff_attn",
    )(lam_params, g2, qt, k, vt)


def _post_kernel(x_ref, oa_ref, ob_ref, oc_ref, wo_ref, g2_ref, wup_ref, wdn_ref, gf_ref,
                 y_ref, *, final):
    mixed = jnp.concatenate([oa_ref[...], ob_ref[...], oc_ref[...]], axis=1)
    x1 = x_ref[...] + jnp.dot(mixed, wo_ref[...], preferred_element_type=F32)
    hn = _rms(x1, g2_ref[...], NORM_EPS).astype(BF16)
    y_ref[...] = x1
    for c in range(wup_ref.shape[1] // D_FF_CHUNK):
        cs = slice(c * D_FF_CHUNK, (c + 1) * D_FF_CHUNK)
        u = jnp.dot(hn, wup_ref[:, cs], preferred_element_type=F32)
        a = jnp.square(jnp.maximum(u, 0.0)).astype(BF16)
        y_ref[...] += jnp.dot(a, wdn_ref[cs, :], preferred_element_type=F32)
    if final:
        y_ref[...] = _rms(y_ref[...], gf_ref[...], NORM_EPS)


def _post(x2, oa, ob, oc, wo, g2, wup, wdn, gf, *, tm, final):
    n, d = x2.shape
    row = lambda i: (i, 0)
    const = lambda i: (0, 0)
    return pl.pallas_call(
        functools.partial(_post_kernel, final=final),
        grid=(n // tm,),
        in_specs=[pl.BlockSpec((tm, d), row),
                  pl.BlockSpec((tm, oa.shape[1]), row),
                  pl.BlockSpec((tm, ob.shape[1]), row),
                  pl.BlockSpec((tm, oc.shape[1]), row),
                  pl.BlockSpec(wo.shape, const), pl.BlockSpec(g2.shape, const),
                  pl.BlockSpec(wup.shape, const), pl.BlockSpec(wdn.shape, const),
                  pl.BlockSpec(gf.shape, const)],
        out_specs=pl.BlockSpec((tm, d), row),
        out_shape=jax.ShapeDtypeStruct((n, d), F32),
        compiler_params=pltpu.CompilerParams(
            dimension_semantics=("arbitrary",), vmem_limit_bytes=VMEM_LIMIT),
        name="post_final" if final else "post",
    )(x2, oa, ob, oc, wo, g2, wup, wdn, gf)


def _rope_tables(seq):
    def tables(dim):
        half = dim // 2
        inv = 1.0 / (ROPE_THETA ** (jnp.arange(half, dtype=F32) / half))
        ang = jnp.arange(seq, dtype=F32)[:, None] * inv[None, :]
        cos, sin = jnp.cos(ang), jnp.sin(ang)
        reps = LANES // dim
        cos_t = jnp.tile(jnp.concatenate([cos, cos], axis=1), (1, reps))
        sin_t = jnp.tile(jnp.concatenate([-sin, sin], axis=1), (1, reps))
        return cos_t, sin_t
    cosa, sina = tables(A_DIM)
    cos32, sin32 = tables(C_QK)
    return cosa, sina, cos32, sin32


def _prep_layer_weights(w_in, w_uq, w_ukv):
    cols = jnp.split(w_in, [384, 768, 1152, 1536, 1664, 1696, 1952, 2208], axis=1)
    qa, ka, va, cq, ckv, kr, qc, kc, vc = cols
    w1 = jnp.concatenate([qa, ka, va, cq, ckv, jnp.tile(kr, (1, LANES // B_ROPE)), qc, kc, vc],
                         axis=1).astype(BF16)
    uq = w_uq.reshape(B_Q_LORA, B_HEADS, B_NOPE + B_ROPE)
    nope = uq[:, :, :B_NOPE].reshape(B_Q_LORA, B_HEADS * B_NOPE)
    rope = uq[:, :, B_NOPE:].reshape(B_Q_LORA, B_HEADS // 2, 2 * B_ROPE)
    rope = jnp.pad(rope, ((0, 0), (0, 0), (0, LANES - 2 * B_ROPE))).reshape(B_Q_LORA, -1)
    wuq = jnp.concatenate([nope, rope], axis=1).astype(BF16)
    ukv = w_ukv.reshape(B_KV_LORA, B_HEADS, B_NOPE + B_V)
    wukv = jnp.concatenate([ukv[:, :, :B_NOPE].reshape(B_KV_LORA, -1),
                            ukv[:, :, B_NOPE:].reshape(B_KV_LORA, -1)], axis=1).astype(BF16)
    return w1, wuq, wukv


def kernel(x, ln1_g, w_in, mla_q_norm_g, mla_w_uq, mla_kv_norm_g, mla_w_ukv, diff_lambda,
           diff_subln_g, w_o, ln2_g, w_up, w_down, final_g):
    bsz, seq, d_model = x.shape
    depth = w_in.shape[0]
    tm = 512
    tq = 512
    tabs = _rope_tables(seq)
    x2 = x.reshape(bsz * seq, d_model)
    gf = final_g.reshape(1, d_model)
    sh = lambda t, w: t.reshape(bsz, seq, w)

    for layer in range(depth):
        lam_init = 0.8 - 0.6 * math.exp(-0.3 * layer)
        w1, wuq, wukv = _prep_layer_weights(w_in[layer], mla_w_uq[layer], mla_w_ukv[layer])
        qa, ka, va, qn, qr, kn, vb, kr, qc, kc, vc = _inproj(
            x2, ln1_g[layer].reshape(1, -1), w1, tabs,
            mla_q_norm_g[layer].reshape(1, -1), wuq,
            mla_kv_norm_g[layer].reshape(1, -1), wukv, seq=seq, tm=tm)

        out_a = _dilated_mixer(sh(qa, 384), sh(ka, 384), sh(va, 384))
        out_b = _mla_attn(qn, qr, sh(kn, 384), sh(kr, 128), vb, tq=min(2 * tq, seq))
        g_sub = jnp.tile(diff_subln_g[layer], LANES // C_V).reshape(1, LANES)
        out_c = _diff_attn(diff_lambda[layer], g_sub, qc, sh(kc, 256), vc,
                           tq=min(2 * tq, seq), lam_init=lam_init)

        x2 = _post(x2, out_a.reshape(-1, A_WIDTH), out_b.reshape(-1, B_WIDTH),
                   out_c.reshape(-1, C_WIDTH), w_o[layer].astype(BF16),
                   ln2_g[layer].reshape(1, -1), w_up[layer].astype(BF16),
                   w_down[layer].astype(BF16), gf, tm=tm, final=layer == depth - 1)

    return x2.reshape(bsz, seq, d_model)
```

```python
import functools
import math

import jax
import jax.numpy as jnp
from jax import lax
from jax.experimental import pallas as pl
from jax.experimental.pallas import tpu as pltpu

F32 = jnp.float32
BF16 = jnp.bfloat16

LANES = 128
ACC_ROWS = LANES + 16
VMEM_LIMIT = 56 * 1024 * 1024

ROPE_THETA = 10000.0
NORM_EPS = 1e-6
SUBLN_EPS = 1e-5
NEG = -1e30
LOG2E = math.log2(math.e)

A_HEADS, A_DIM = 6, 64
A_CONFIGS = ((128, 1), (512, 4), (2048, 16))
A_SPAN = 128
B_HEADS, B_NOPE, B_ROPE, B_V = 6, 64, 32, 64
B_Q_LORA, B_KV_LORA = 384, 128
C_HEADS, C_QK = 4, 32
C_V = 2 * C_QK
A_WIDTH = A_HEADS * A_DIM
B_WIDTH = B_HEADS * B_V
C_WIDTH = C_HEADS * C_V
D_FF_CHUNK = 512


def _rms(x, g, eps):
    return x * lax.rsqrt(jnp.mean(x * x, axis=-1, keepdims=True) + eps) * g


def _rope(x, cos, sin_signed, half):
    lane = lax.broadcasted_iota(jnp.int32, (x.shape[0], LANES), 1)
    first = (lane & (2 * half - 1)) < half
    outs = []
    for c in range(x.shape[1] // LANES):
        xc = x[:, c * LANES:(c + 1) * LANES]
        partner = jnp.where(first, pltpu.roll(xc, LANES - half, 1), pltpu.roll(xc, half, 1))
        outs.append(xc * cos + partner * sin_signed)
    return outs[0] if len(outs) == 1 else jnp.concatenate(outs, axis=1)


def _nt_dot(a, b):
    return lax.dot_general(a, b, (((1,), (1,)), ((), ())), preferred_element_type=F32)


def _inproj_kernel(x_ref, g_ref, w1_ref, cosa_ref, sina_ref, cos32_ref, sin32_ref,
                   qg_ref, wuq_ref, kvg_ref, wukv_ref,
                   qa_ref, ka_ref, va_ref, qn_ref, qr_ref, kn_ref, vb_ref, kr_ref,
                   qc_ref, kc_ref, vc_ref):
    h = _rms(x_ref[...], g_ref[...], NORM_EPS).astype(BF16)

    def proj(lo, width):
        return jnp.dot(h, w1_ref[:, lo:lo + width], preferred_element_type=F32)

    cosa, sina = cosa_ref[...], sina_ref[...]
    cos32, sin32 = cos32_ref[...], sin32_ref[...]
    scale_a = A_DIM ** -0.5 * LOG2E
    scale_b = (B_NOPE + B_ROPE) ** -0.5 * LOG2E
    scale_c = C_QK ** -0.5 * LOG2E

    qa_ref[...] = _rope(proj(0, 384), cosa, sina, 32) * scale_a
    ka_ref[...] = _rope(proj(384, 384), cosa, sina, 32)
    va_ref[...] = proj(768, 384)

    cq = _rms(proj(1152, 384), qg_ref[...], NORM_EPS).astype(BF16)
    qf = jnp.dot(cq, wuq_ref[...], preferred_element_type=F32)
    qn_ref[...] = (qf[:, :384] * scale_b).T.astype(BF16)
    qr_ref[...] = (_rope(qf[:, 384:], cos32, sin32, 16) * scale_b).T.astype(BF16)

    ckv = _rms(proj(1536, 128), kvg_ref[...], NORM_EPS).astype(BF16)
    kvf = jnp.dot(ckv, wukv_ref[...], preferred_element_type=F32)
    kn_ref[...] = kvf[:, :384].astype(BF16)
    vb_ref[...] = kvf[:, 384:].T.astype(BF16)
    kr_ref[...] = _rope(proj(1664, 128), cos32, sin32, 16).astype(BF16)

    qc_ref[...] = (_rope(proj(1792, 256), cos32, sin32, 16) * scale_c).T.astype(BF16)
    kc_ref[...] = _rope(proj(2048, 256), cos32, sin32, 16).astype(BF16)
    vc_ref[...] = proj(2304, 256).T.astype(BF16)


def _inproj(x2, g, w1, tabs, qg, wuq, kvg, wukv, *, seq, tm):
    n = x2.shape[0]
    nt = n // tm
    st = seq // tm
    row = lambda i: (i, 0)
    const = lambda i: (0, 0)
    tab = lambda i: (i % st, 0)
    widths = (384, 384, 384, 384, 384, 384, 384, 128, 256, 256, 256)
    major = (False, False, False, True, True, False, True, False, True, False, True)
    bsz = n // seq
    col = lambda i: (i // st, 0, i % st)
    out_specs = [pl.BlockSpec((None, w, tm), col) if t else pl.BlockSpec((tm, w), row)
                 for w, t in zip(widths, major)]
    dtypes = (F32,) * 3 + (BF16,) * 8
    out_shape = [jax.ShapeDtypeStruct((bsz, w, seq) if t else (n, w), dt)
                 for w, t, dt in zip(widths, major, dtypes)]
    return pl.pallas_call(
        _inproj_kernel,
        grid=(nt,),
        in_specs=[
            pl.BlockSpec((tm, x2.shape[1]), row),
            pl.BlockSpec(g.shape, const),
            pl.BlockSpec(w1.shape, const),
            pl.BlockSpec((tm, LANES), tab), pl.BlockSpec((tm, LANES), tab),
            pl.BlockSpec((tm, LANES), tab), pl.BlockSpec((tm, LANES), tab),
            pl.BlockSpec(qg.shape, const), pl.BlockSpec(wuq.shape, const),
            pl.BlockSpec(kvg.shape, const), pl.BlockSpec(wukv.shape, const),
        ],
        out_specs=out_specs,
        out_shape=out_shape,
        compiler_params=pltpu.CompilerParams(
            dimension_semantics=("arbitrary",), vmem_limit_bytes=VMEM_LIMIT),
        name="inproj",
    )(x2, g, w1, *tabs, qg, wuq, kvg, wukv)


def _dilated_kernel(q_ref, k_ref, kp_ref, v_ref, vp_ref, o_ref,
                    q4_ref, k4_ref, kp4_ref, v4_ref, vp4_ref, o4_ref, l4_ref, ot_ref, lt_ref,
                    *, tile, nt):
    quarter = tile // 4
    t = pl.program_id(0) % nt
    lane = lax.broadcasted_iota(jnp.int32, (A_SPAN, LANES), 1)
    low = lane < A_DIM
    qi = lax.broadcasted_iota(jnp.int32, (2 * A_SPAN, 2 * A_SPAN), 0) & (A_SPAN - 1)
    ki = lax.broadcasted_iota(jnp.int32, (2 * A_SPAN, 2 * A_SPAN), 1)
    dist = qi + A_SPAN - ki
    band = (dist >= 0) & (dist <= A_SPAN)
    band0 = band & (ki >= jnp.where(t > 0, 0, A_SPAN))
    ones = jnp.ones((2 * A_SPAN, LANES), BF16)
    zero = jnp.zeros((), BF16)

    for src, dst in ((q_ref, q4_ref), (k_ref, k4_ref), (kp_ref, kp4_ref),
                     (v_ref, v4_ref), (vp_ref, vp4_ref)):
        for c in range(4):
            dst[c * quarter:(c + 1) * quarter, :] = src[pl.ds(c, quarter, stride=4), :]

    def attend(qg, k_prev, k_cur, v_prev, v_cur, valid):
        qg = qg.astype(BF16)
        qs = jnp.concatenate([jnp.where(low, qg, zero), jnp.where(low, zero, qg)], axis=0)
        kk = jnp.concatenate([k_prev, k_cur], axis=0).astype(BF16)
        vv = jnp.concatenate([v_prev, v_cur], axis=0).astype(BF16)
        s = jnp.where(valid, _nt_dot(qs, kk), NEG)
        m = jnp.max(s, axis=1, keepdims=True)
        p = jnp.exp2(s - m).astype(BF16)
        pv = jnp.dot(p, jnp.concatenate([vv, ones], axis=1), preferred_element_type=F32)
        num = jnp.where(low, pv[:A_SPAN, :LANES], pv[A_SPAN:, :LANES])
        den = jnp.where(low, pv[:A_SPAN, LANES:], pv[A_SPAN:, LANES:])
        mm = jnp.where(low, m[:A_SPAN], m[A_SPAN:])
        return num / den, mm + jnp.log2(den)

    def merge(o, lse, o_prev, l_prev):
        mx = jnp.maximum(lse, l_prev)
        wa = jnp.exp2(lse - mx)
        wb = jnp.exp2(l_prev - mx)
        tot = wa + wb
        return (wa * o + wb * o_prev) / tot, mx + jnp.log2(tot)

    for c in range(4):
        for g in range(quarter // A_SPAN):
            cur = pl.ds(c * quarter + g * A_SPAN, A_SPAN)
            if g == 0:
                prev = pl.ds((c + 1) * quarter - A_SPAN, A_SPAN)
                k_prev, v_prev, valid = kp4_ref[prev, :], vp4_ref[prev, :], band0
            else:
                prev = pl.ds(c * quarter + (g - 1) * A_SPAN, A_SPAN)
                k_prev, v_prev, valid = k4_ref[prev, :], v4_ref[prev, :], band
            o, lse = attend(q4_ref[cur, :], k_prev, k4_ref[cur, :], v_prev, v4_ref[cur, :], valid)
            o4_ref[cur, :] = o
            l4_ref[cur, :] = lse

    assert tile == A_SPAN * 16
    for c in range(4):
        for a in range(4):
            cur = pl.ds(c * quarter + a, A_SPAN, stride=4)
            o, lse = attend(q4_ref[cur, :], kp4_ref[cur, :], k4_ref[cur, :],
                            vp4_ref[cur, :], v4_ref[cur, :], band0)
            o, lse = merge(o, lse, o4_ref[cur, :], l4_ref[cur, :])
            o4_ref[cur, :] = o
            l4_ref[cur, :] = lse

    for c in range(4):
        ot_ref[pl.ds(c, quarter, stride=4), :] = o4_ref[c * quarter:(c + 1) * quarter, :]
        lt_ref[pl.ds(c, quarter, stride=4), :] = l4_ref[c * quarter:(c + 1) * quarter, :]

    for g in range(tile // A_SPAN):
        cur = pl.ds(g * A_SPAN, A_SPAN)
        if g == 0:
            prev = pl.ds(tile - A_SPAN, A_SPAN)
            k_prev, v_prev, valid = kp_ref[prev, :], vp_ref[prev, :], band0
        else:
            prev = pl.ds((g - 1) * A_SPAN, A_SPAN)
            k_prev, v_prev, valid = k_ref[prev, :], v_ref[prev, :], band
        o, lse = attend(q_ref[cur, :], k_prev, k_ref[cur, :], v_prev, v_ref[cur, :], valid)
        o, _ = merge(o, lse, ot_ref[cur, :], lt_ref[cur, :])
        o_ref[cur, :] = o.astype(o_ref.dtype)


def _dilated_mixer(q, k, v, *, seq, tile):
    n, width = q.shape
    nt = seq // tile
    for window, dilation in A_CONFIGS:
        assert window // dilation == A_SPAN and tile % window == 0
    cur = lambda i, p: (i, p)
    prev = lambda i, p: (jnp.where(i % nt == 0, i, i - 1), p)
    blk = pl.BlockSpec((tile, LANES), cur)
    pblk = pl.BlockSpec((tile, LANES), prev)
    return pl.pallas_call(
        functools.partial(_dilated_kernel, tile=tile, nt=nt),
        grid=(n // tile, width // LANES),
        in_specs=[blk, blk, pblk, blk, pblk],
        out_specs=blk,
        out_shape=jax.ShapeDtypeStruct((n, width), BF16),
        scratch_shapes=[pltpu.VMEM((tile, LANES), F32)] * 9,
        compiler_params=pltpu.CompilerParams(
            dimension_semantics=("arbitrary",) * 2, vmem_limit_bytes=VMEM_LIMIT),
        name="dilated",
    )(q, k, k, v, v)


def _flash_scratch(rows, tq):
    tk = tq // 2
    score_bufs = [pltpu.VMEM((tk, rows), F32), pltpu.VMEM((1, rows), F32),
                  pltpu.VMEM((tk, rows), BF16)]
    return [pltpu.VMEM((1, rows), F32), pltpu.VMEM((ACC_ROWS, rows), F32)] + 2 * score_bufs


def _flash_cols(qst, k_block, vt_block, scratch, *, qi, tq):
    m_ref, acc_ref, st_a, cm_a, pt_a, st_b, cm_b, pt_b = scratch
    tk = tq // 2
    ones = jnp.ones((ACC_ROWS - LANES, tk), BF16)
    m_ref[...] = jnp.full(m_ref.shape, NEG, F32)

    def scores(blk, st_ref, cm_ref, key_offset=None):
        st = jnp.dot(k_block(blk), qst, preferred_element_type=F32)
        if key_offset is not None:
            key = lax.broadcasted_iota(jnp.int32, st.shape, 0) + key_offset
            qry = lax.broadcasted_iota(jnp.int32, st.shape, 1) & (tq - 1)
            st = jnp.where(key <= qry, st, NEG)
        st_ref[...] = st
        cm_ref[...] = jnp.max(st, axis=0, keepdims=True)

    def softmax(st_ref, cm_ref, pt_ref):
        m_prev = m_ref[...]
        m_new = jnp.maximum(m_prev, cm_ref[...])
        m_ref[...] = m_new
        pt_ref[...] = jnp.exp2(st_ref[...] - m_new).astype(BF16)
        return jnp.exp2(m_prev - m_new)

    def values(blk, pt_ref):
        lhs = jnp.concatenate([vt_block(blk), ones], axis=0)
        return jnp.dot(lhs, pt_ref[...], preferred_element_type=F32)

    diag0 = 2 * qi
    scores(diag0, st_a, cm_a, 0)
    scores(diag0 + 1, st_b, cm_b, tk)
    softmax(st_a, cm_a, pt_a)
    acc_ref[...] = jnp.zeros(acc_ref.shape, F32)

    def body(j, carry):
        prev = jnp.where(j == 0, diag0, 2 * j - 2)
        alpha = softmax(st_b, cm_b, pt_b)
        scores(2 * j, st_a, cm_a)
        acc_ref[...] = alpha * (acc_ref[...] + values(prev, pt_a))
        alpha = softmax(st_a, cm_a, pt_a)
        scores(2 * j + 1, st_b, cm_b)
        acc_ref[...] = alpha * (acc_ref[...] + values(prev + 1, pt_b))
        return carry

    lax.fori_loop(0, qi, body, 0)

    last = jnp.where(qi == 0, 0, 2 * qi - 2)
    alpha = softmax(st_b, cm_b, pt_b)
    acc_ref[...] = alpha * (acc_ref[...] + values(last, pt_a))
    acc_ref[...] += values(last + 1, pt_b)
    return acc_ref[:LANES, :] / acc_ref[LANES:LANES + 1, :]


def _mla_kernel(qn_ref, qr_ref, kn_ref, kr_ref, vt_ref, o_ref, *scratch, tq):
    qi = pl.program_id(2)
    tk = tq // 2
    row = lax.broadcasted_iota(jnp.int32, (LANES, tq), 0)
    zero = jnp.zeros((), BF16)
    qn, qr = qn_ref[...], qr_ref[...]
    q0 = jnp.concatenate([jnp.where(row < B_NOPE, qn, zero),
                          jnp.where(row < B_ROPE, qr, zero)], axis=0)
    q1 = jnp.concatenate([jnp.where(row >= B_NOPE, qn, zero),
                          jnp.where((row >= B_ROPE) & (row < 2 * B_ROPE), qr, zero)], axis=0)
    qst = jnp.concatenate([q0, q1], axis=1)

    def k_block(i):
        rows = pl.ds(pl.multiple_of(i * tk, tk), tk)
        return jnp.concatenate([kn_ref[rows, :], kr_ref[rows, :]], axis=1)

    def vt_block(i):
        return vt_ref[:, pl.ds(pl.multiple_of(i * tk, tk), tk)]

    ot = _flash_cols(qst, k_block, vt_block, scratch, qi=qi, tq=tq)
    pair_t = jnp.concatenate([ot[:B_V, :tq], ot[B_V:, tq:]], axis=0)
    o_ref[...] = pair_t.T.astype(o_ref.dtype)


def _mla_attn(qnt, qrt, kn, kr, vt, *, tq):
    bsz, seq, width = kn.shape
    pairs = width // LANES
    qblk = pl.BlockSpec((None, LANES, tq), lambda b, p, i: (b, p, i))
    kblk = pl.BlockSpec((None, seq, LANES), lambda b, p, i: (b, 0, p))
    krblk = pl.BlockSpec((None, seq, LANES), lambda b, p, i: (b, 0, 0))
    vtblk = pl.BlockSpec((None, LANES, seq), lambda b, p, i: (b, p, 0))
    return pl.pallas_call(
        functools.partial(_mla_kernel, tq=tq),
        grid=(bsz, pairs, seq // tq),
        in_specs=[qblk, qblk, kblk, krblk, vtblk],
        out_specs=pl.BlockSpec((None, tq, LANES), lambda b, p, i: (b, i, p)),
        out_shape=jax.ShapeDtypeStruct((bsz, seq, width), BF16),
        scratch_shapes=_flash_scratch(2 * tq, tq),
        compiler_params=pltpu.CompilerParams(
            dimension_semantics=("arbitrary",) * 3, vmem_limit_bytes=VMEM_LIMIT),
        name="mla_attn",
    )(qnt, qrt, kn, kr, vt)


def _diff_kernel(lam_ref, g_ref, q_ref, k_ref, vt_ref, o_ref, *scratch, tq, lam_init):
    qi = pl.program_id(2)
    tk = tq // 2
    lane = lax.broadcasted_iota(jnp.int32, (tq, LANES), 1)
    row = lax.broadcasted_iota(jnp.int32, (LANES, tq), 0)
    zero = jnp.zeros((), BF16)
    q = q_ref[...]
    qst = jnp.concatenate(
        [jnp.where((row >= j * C_QK) & (row < (j + 1) * C_QK), q, zero) for j in range(4)],
        axis=1)

    def k_block(i):
        return k_ref[pl.ds(pl.multiple_of(i * tk, tk), tk), :]

    def vt_block(i):
        return vt_ref[:, pl.ds(pl.multiple_of(i * tk, tk), tk)]

    ot = _flash_cols(qst, k_block, vt_block, scratch, qi=qi, tq=tq)

    lp = lam_ref[...]
    lam = (jnp.exp(jnp.sum(lp[0:1] * lp[1:2], axis=1, keepdims=True))
           - jnp.exp(jnp.sum(lp[2:3] * lp[3:4], axis=1, keepdims=True)) + lam_init)
    d0 = ot[:C_V, 0:tq] - lam * ot[:C_V, tq:2 * tq]
    d1 = ot[C_V:, 2 * tq:3 * tq] - lam * ot[C_V:, 3 * tq:]
    d = jnp.concatenate([d0, d1], axis=0).T
    low = lane < C_V
    sq = d * d
    ms_lo = jnp.sum(jnp.where(low, sq, 0.0), axis=1, keepdims=True) * (1.0 / C_V)
    ms_hi = jnp.sum(jnp.where(low, 0.0, sq), axis=1, keepdims=True) * (1.0 / C_V)
    rs = jnp.where(low, lax.rsqrt(ms_lo + SUBLN_EPS), lax.rsqrt(ms_hi + SUBLN_EPS))
    o_ref[...] = ((d * rs * g_ref[...]) * (1.0 - lam_init)).astype(o_ref.dtype)


def _diff_attn(lam_params, g2, qt, k, vt, *, tq, lam_init):
    bsz, seq, width = k.shape
    pairs = width // LANES
    qblk = pl.BlockSpec((None, LANES, tq), lambda b, p, i: (b, p, i))
    kblk = pl.BlockSpec((None, seq, LANES), lambda b, p, i: (b, 0, p))
    vtblk = pl.BlockSpec((None, LANES, seq), lambda b, p, i: (b, p, 0))
    const = lambda b, p, i: (0, 0)
    return pl.pallas_call(
        functools.partial(_diff_kernel, tq=tq, lam_init=lam_init),
        grid=(bsz, pairs, seq // tq),
        in_specs=[pl.BlockSpec(lam_params.shape, const), pl.BlockSpec(g2.shape, const),
                  qblk, kblk, vtblk],
        out_specs=pl.BlockSpec((None, tq, LANES), lambda b, p, i: (b, i, p)),
        out_shape=jax.ShapeDtypeStruct((bsz, seq, width), BF16),
        scratch_shapes=_flash_scratch(4 * tq, tq),
        compiler_params=pltpu.CompilerParams(
            dimension_semantics=("arbitrary",) * 3, vmem_limit_bytes=VMEM_LIMIT),
        name="diff_attn",
    )(lam_params, g2, qt, k, vt)


def _post_kernel(x_ref, oa_ref, ob_ref, oc_ref, wo_ref, g2_ref, wup_ref, wdn_ref, gf_ref,
                 y_ref, *, final):
    mixed = jnp.concatenate([oa_ref[...], ob_ref[...], oc_ref[...]], axis=1)
    x1 = x_ref[...] + jnp.dot(mixed, wo_ref[...], preferred_element_type=F32)
    hn = _rms(x1, g2_ref[...], NORM_EPS).astype(BF16)
    y_ref[...] = x1
    for c in range(wup_ref.shape[1] // D_FF_CHUNK):
        cs = slice(c * D_FF_CHUNK, (c + 1) * D_FF_CHUNK)
        u = jnp.dot(hn, wup_ref[:, cs], preferred_element_type=F32)
        a = jnp.square(jnp.maximum(u, 0.0)).astype(BF16)
        y_ref[...] += jnp.dot(a, wdn_ref[cs, :], preferred_element_type=F32)
    if final:
        y_ref[...] = _rms(y_ref[...], gf_ref[...], NORM_EPS)


def _post(x2, oa, ob, oc, wo, g2, wup, wdn, gf, *, tm, final):
    n, d = x2.shape
    row = lambda i: (i, 0)
    const = lambda i: (0, 0)
    return pl.pallas_call(
        functools.partial(_post_kernel, final=final),
        grid=(n // tm,),
        in_specs=[pl.BlockSpec((tm, d), row),
                  pl.BlockSpec((tm, oa.shape[1]), row),
                  pl.BlockSpec((tm, ob.shape[1]), row),
                  pl.BlockSpec((tm, oc.shape[1]), row),
                  pl.BlockSpec(wo.shape, const), pl.BlockSpec(g2.shape, const),
                  pl.BlockSpec(wup.shape, const), pl.BlockSpec(wdn.shape, const),
                  pl.BlockSpec(gf.shape, const)],
        out_specs=pl.BlockSpec((tm, d), row),
        out_shape=jax.ShapeDtypeStruct((n, d), F32),
        compiler_params=pltpu.CompilerParams(
            dimension_semantics=("arbitrary",), vmem_limit_bytes=VMEM_LIMIT),
        name="post_final" if final else "post",
    )(x2, oa, ob, oc, wo, g2, wup, wdn, gf)


def _rope_tables(seq):
    def tables(dim):
        half = dim // 2
        inv = 1.0 / (ROPE_THETA ** (jnp.arange(half, dtype=F32) / half))
        ang = jnp.arange(seq, dtype=F32)[:, None] * inv[None, :]
        cos, sin = jnp.cos(ang), jnp.sin(ang)
        reps = LANES // dim
        cos_t = jnp.tile(jnp.concatenate([cos, cos], axis=1), (1, reps))
        sin_t = jnp.tile(jnp.concatenate([-sin, sin], axis=1), (1, reps))
        return cos_t, sin_t
    cosa, sina = tables(A_DIM)
    cos32, sin32 = tables(C_QK)
    return cosa, sina, cos32, sin32


def _prep_layer_weights(w_in, w_uq, w_ukv):
    cols = jnp.split(w_in, [384, 768, 1152, 1536, 1664, 1696, 1952, 2208], axis=1)
    qa, ka, va, cq, ckv, kr, qc, kc, vc = cols
    w1 = jnp.concatenate([qa, ka, va, cq, ckv, jnp.tile(kr, (1, LANES // B_ROPE)), qc, kc, vc],
                         axis=1).astype(BF16)
    uq = w_uq.reshape(B_Q_LORA, B_HEADS, B_NOPE + B_ROPE)
    nope = uq[:, :, :B_NOPE].reshape(B_Q_LORA, B_HEADS * B_NOPE)
    rope = uq[:, :, B_NOPE:].reshape(B_Q_LORA, B_HEADS // 2, 2 * B_ROPE)
    rope = jnp.pad(rope, ((0, 0), (0, 0), (0, LANES - 2 * B_ROPE))).reshape(B_Q_LORA, -1)
    wuq = jnp.concatenate([nope, rope], axis=1).astype(BF16)
    ukv = w_ukv.reshape(B_KV_LORA, B_HEADS, B_NOPE + B_V)
    wukv = jnp.concatenate([ukv[:, :, :B_NOPE].reshape(B_KV_LORA, -1),
                            ukv[:, :, B_NOPE:].reshape(B_KV_LORA, -1)], axis=1).astype(BF16)
    return w1, wuq, wukv


def kernel(x, ln1_g, w_in, mla_q_norm_g, mla_w_uq, mla_kv_norm_g, mla_w_ukv, diff_lambda,
           diff_subln_g, w_o, ln2_g, w_up, w_down, final_g):
    bsz, seq, d_model = x.shape
    depth = w_in.shape[0]
    tm = 512
    tq = 512
    tabs = _rope_tables(seq)
    x2 = x.reshape(bsz * seq, d_model)
    gf = final_g.reshape(1, d_model)
    sh = lambda t, w: t.reshape(bsz, seq, w)

    for layer in range(depth):
        lam_init = 0.8 - 0.6 * math.exp(-0.3 * layer)
        w1, wuq, wukv = _prep_layer_weights(w_in[layer], mla_w_uq[layer], mla_w_ukv[layer])
        qa, ka, va, qn, qr, kn, vb, kr, qc, kc, vc = _inproj(
            x2, ln1_g[layer].reshape(1, -1), w1, tabs,
            mla_q_norm_g[layer].reshape(1, -1), wuq,
            mla_kv_norm_g[layer].reshape(1, -1), wukv, seq=seq, tm=tm)

        out_a = _dilated_mixer(qa, ka, va, seq=seq, tile=2048)
        out_b = _mla_attn(qn, qr, sh(kn, 384), sh(kr, 128), vb, tq=min(2 * tq, seq))
        g_sub = jnp.tile(diff_subln_g[layer], LANES // C_V).reshape(1, LANES)
        out_c = _diff_attn(diff_lambda[layer], g_sub, qc, sh(kc, 256), vc,
                           tq=min(2 * tq, seq), lam_init=lam_init)

        x2 = _post(x2, out_a, out_b.reshape(-1, B_WIDTH),
                   out_c.reshape(-1, C_WIDTH), w_o[layer].astype(BF16),
                   ln2_g[layer].reshape(1, -1), w_up[layer].astype(BF16),
                   w_down[layer].astype(BF16), gf, tm=tm, final=layer == depth - 1)

    return x2.reshape(bsz, seq, d_model)
```

```python
import functools
import math

import jax
import jax.numpy as jnp
from jax import lax
from jax.experimental import pallas as pl
from jax.experimental.pallas import tpu as pltpu

F32 = jnp.float32
BF16 = jnp.bfloat16

LANES = 128
ACC_ROWS = LANES + 16
VMEM_LIMIT = 56 * 1024 * 1024

ROPE_THETA = 10000.0
NORM_EPS = 1e-6
SUBLN_EPS = 1e-5
NEG = -1e30
LOG2E = math.log2(math.e)

A_HEADS, A_DIM = 6, 64
A_CONFIGS = ((128, 1), (512, 4), (2048, 16))
A_SPAN = 128
B_HEADS, B_NOPE, B_ROPE, B_V = 6, 64, 32, 64
B_Q_LORA, B_KV_LORA = 384, 128
C_HEADS, C_QK = 4, 32
C_V = 2 * C_QK
A_WIDTH = A_HEADS * A_DIM
B_WIDTH = B_HEADS * B_V
C_WIDTH = C_HEADS * C_V
D_FF_CHUNK = 512
INPROJ_SUB = 256


def _rms(x, g, eps):
    return x * lax.rsqrt(jnp.mean(x * x, axis=-1, keepdims=True) + eps) * g


def _rope(x, cos, sin_signed, half):
    lane = lax.broadcasted_iota(jnp.int32, (x.shape[0], LANES), 1)
    first = (lane & (2 * half - 1)) < half
    outs = []
    for c in range(x.shape[1] // LANES):
        xc = x[:, c * LANES:(c + 1) * LANES]
        partner = jnp.where(first, pltpu.roll(xc, LANES - half, 1), pltpu.roll(xc, half, 1))
        outs.append(xc * cos + partner * sin_signed)
    return outs[0] if len(outs) == 1 else jnp.concatenate(outs, axis=1)


def _nt_dot(a, b):
    return lax.dot_general(a, b, (((1,), (1,)), ((), ())), preferred_element_type=F32)


def _inproj_kernel(x_ref, g_ref, w1_ref, cosa_ref, sina_ref, cos32_ref, sin32_ref,
                   qg_ref, wuq_ref, kvg_ref, wukv_ref,
                   qa_ref, ka_ref, va_ref, qn_ref, qr_ref, kn_ref, vb_ref, kr_ref,
                   qc_ref, kc_ref, vc_ref):
    scale_a = A_DIM ** -0.5 * LOG2E
    scale_b = (B_NOPE + B_ROPE) ** -0.5 * LOG2E
    scale_c = C_QK ** -0.5 * LOG2E

    for r0 in range(0, x_ref.shape[0], INPROJ_SUB):
        rows = slice(r0, r0 + INPROJ_SUB)
        h = _rms(x_ref[rows, :], g_ref[...], NORM_EPS).astype(BF16)
        p_all = jnp.dot(h, w1_ref[...], preferred_element_type=F32)

        def proj(lo, width):
            return p_all[:, lo:lo + width]

        cosa, sina = cosa_ref[rows, :], sina_ref[rows, :]
        cos32, sin32 = cos32_ref[rows, :], sin32_ref[rows, :]

        qa_ref[rows, :] = _rope(proj(0, 384), cosa, sina, 32) * scale_a
        ka_ref[rows, :] = _rope(proj(384, 384), cosa, sina, 32)
        va_ref[rows, :] = proj(768, 384)

        cq = _rms(proj(1152, 384), qg_ref[...], NORM_EPS).astype(BF16)
        qf = jnp.dot(cq, wuq_ref[...], preferred_element_type=F32)
        qn_ref[:, rows] = (qf[:, :384] * scale_b).T.astype(BF16)
        qr_ref[:, rows] = (_rope(qf[:, 384:], cos32, sin32, 16) * scale_b).T.astype(BF16)

        ckv = _rms(proj(1536, 128), kvg_ref[...], NORM_EPS).astype(BF16)
        kvf = jnp.dot(ckv, wukv_ref[...], preferred_element_type=F32)
        kn_ref[rows, :] = kvf[:, :384].astype(BF16)
        vb_ref[:, rows] = kvf[:, 384:].T.astype(BF16)
        kr_ref[rows, :] = _rope(proj(1664, 128), cos32, sin32, 16).astype(BF16)

        qc_ref[:, rows] = (_rope(proj(1792, 256), cos32, sin32, 16) * scale_c).T.astype(BF16)
        kc_ref[rows, :] = _rope(proj(2048, 256), cos32, sin32, 16).astype(BF16)
        vc_ref[:, rows] = proj(2304, 256).T.astype(BF16)


def _inproj(x2, g, w1, tabs, qg, wuq, kvg, wukv, *, seq, tm):
    n = x2.shape[0]
    nt = n // tm
    st = seq // tm
    row = lambda i: (i, 0)
    const = lambda i: (0, 0)
    tab = lambda i: (i % st, 0)
    widths = (384, 384, 384, 384, 384, 384, 384, 128, 256, 256, 256)
    major = (False, False, False, True, True, False, True, False, True, False, True)
    bsz = n // seq
    col = lambda i: (i // st, 0, i % st)
    out_specs = [pl.BlockSpec((None, w, tm), col) if t else pl.BlockSpec((tm, w), row)
                 for w, t in zip(widths, major)]
    dtypes = (F32,) * 3 + (BF16,) * 8
    out_shape = [jax.ShapeDtypeStruct((bsz, w, seq) if t else (n, w), dt)
                 for w, t, dt in zip(widths, major, dtypes)]
    return pl.pallas_call(
        _inproj_kernel,
        grid=(nt,),
        in_specs=[
            pl.BlockSpec((tm, x2.shape[1]), row),
            pl.BlockSpec(g.shape, const),
            pl.BlockSpec(w1.shape, const),
            pl.BlockSpec((tm, LANES), tab), pl.BlockSpec((tm, LANES), tab),
            pl.BlockSpec((tm, LANES), tab), pl.BlockSpec((tm, LANES), tab),
            pl.BlockSpec(qg.shape, const), pl.BlockSpec(wuq.shape, const),
            pl.BlockSpec(kvg.shape, const), pl.BlockSpec(wukv.shape, const),
        ],
        out_specs=out_specs,
        out_shape=out_shape,
        compiler_params=pltpu.CompilerParams(
            dimension_semantics=("arbitrary",), vmem_limit_bytes=VMEM_LIMIT),
        name="inproj",
    )(x2, g, w1, *tabs, qg, wuq, kvg, wukv)


def _dilated_kernel(q_ref, k_ref, kp_ref, v_ref, vp_ref, o_ref,
                    q4_ref, k4_ref, kp4_ref, v4_ref, vp4_ref, o4_ref, l4_ref, ot_ref, lt_ref,
                    *, tile, nt):
    quarter = tile // 4
    t = pl.program_id(0) % nt
    lane = lax.broadcasted_iota(jnp.int32, (A_SPAN, LANES), 1)
    low = lane < A_DIM
    qi = lax.broadcasted_iota(jnp.int32, (2 * A_SPAN, 2 * A_SPAN), 0) & (A_SPAN - 1)
    ki = lax.broadcasted_iota(jnp.int32, (2 * A_SPAN, 2 * A_SPAN), 1)
    dist = qi + A_SPAN - ki
    band = (dist >= 0) & (dist <= A_SPAN)
    band0 = band & (ki >= jnp.where(t > 0, 0, A_SPAN))
    ones = jnp.ones((2 * A_SPAN, LANES), BF16)
    zero = jnp.zeros((), BF16)

    for src, dst in ((q_ref, q4_ref), (k_ref, k4_ref), (kp_ref, kp4_ref),
                     (v_ref, v4_ref), (vp_ref, vp4_ref)):
        for c in range(4):
            dst[c * quarter:(c + 1) * quarter, :] = src[pl.ds(c, quarter, stride=4), :]

    def attend(qg, k_prev, k_cur, v_prev, v_cur, valid):
        qg = qg.astype(BF16)
        qs = jnp.concatenate([jnp.where(low, qg, zero), jnp.where(low, zero, qg)], axis=0)
        kk = jnp.concatenate([k_prev, k_cur], axis=0).astype(BF16)
        vv = jnp.concatenate([v_prev, v_cur], axis=0).astype(BF16)
        s = jnp.where(valid, _nt_dot(qs, kk), NEG)
        m = jnp.max(s, axis=1, keepdims=True)
        p = jnp.exp2(s - m).astype(BF16)
        pv = jnp.dot(p, jnp.concatenate([vv, ones], axis=1), preferred_element_type=F32)
        num = jnp.where(low, pv[:A_SPAN, :LANES], pv[A_SPAN:, :LANES])
        den = jnp.where(low, pv[:A_SPAN, LANES:], pv[A_SPAN:, LANES:])
        mm = jnp.where(low, m[:A_SPAN], m[A_SPAN:])
        return num / den, mm + jnp.log2(den)

    def merge(o, lse, o_prev, l_prev):
        mx = jnp.maximum(lse, l_prev)
        wa = jnp.exp2(lse - mx)
        wb = jnp.exp2(l_prev - mx)
        tot = wa + wb
        return (wa * o + wb * o_prev) / tot, mx + jnp.log2(tot)

    for c in range(4):
        for g in range(quarter // A_SPAN):
            cur = pl.ds(c * quarter + g * A_SPAN, A_SPAN)
            if g == 0:
                prev = pl.ds((c + 1) * quarter - A_SPAN, A_SPAN)
                k_prev, v_prev, valid = kp4_ref[prev, :], vp4_ref[prev, :], band0
            else:
                prev = pl.ds(c * quarter + (g - 1) * A_SPAN, A_SPAN)
                k_prev, v_prev, valid = k4_ref[prev, :], v4_ref[prev, :], band
            o, lse = attend(q4_ref[cur, :], k_prev, k4_ref[cur, :], v_prev, v4_ref[cur, :], valid)
            o4_ref[cur, :] = o
            l4_ref[cur, :] = lse

    assert tile == A_SPAN * 16
    for c in range(4):
        for a in range(4):
            cur = pl.ds(c * quarter + a, A_SPAN, stride=4)
            o, lse = attend(q4_ref[cur, :], kp4_ref[cur, :], k4_ref[cur, :],
                            vp4_ref[cur, :], v4_ref[cur, :], band0)
            o, lse = merge(o, lse, o4_ref[cur, :], l4_ref[cur, :])
            o4_ref[cur, :] = o
            l4_ref[cur, :] = lse

    for c in range(4):
        ot_ref[pl.ds(c, quarter, stride=4), :] = o4_ref[c * quarter:(c + 1) * quarter, :]
        lt_ref[pl.ds(c, quarter, stride=4), :] = l4_ref[c * quarter:(c + 1) * quarter, :]

    for g in range(tile // A_SPAN):
        cur = pl.ds(g * A_SPAN, A_SPAN)
        if g == 0:
            prev = pl.ds(tile - A_SPAN, A_SPAN)
            k_prev, v_prev, valid = kp_ref[prev, :], vp_ref[prev, :], band0
        else:
            prev = pl.ds((g - 1) * A_SPAN, A_SPAN)
            k_prev, v_prev, valid = k_ref[prev, :], v_ref[prev, :], band
        o, lse = attend(q_ref[cur, :], k_prev, k_ref[cur, :], v_prev, v_ref[cur, :], valid)
        o, _ = merge(o, lse, ot_ref[cur, :], lt_ref[cur, :])
        o_ref[cur, :] = o.astype(o_ref.dtype)


def _dilated_mixer(q, k, v, *, seq, tile):
    n, width = q.shape
    nt = seq // tile
    for window, dilation in A_CONFIGS:
        assert window // dilation == A_SPAN and tile % window == 0
    cur = lambda i, p: (i, p)
    prev = lambda i, p: (jnp.where(i % nt == 0, i, i - 1), p)
    blk = pl.BlockSpec((tile, LANES), cur)
    pblk = pl.BlockSpec((tile, LANES), prev)
    return pl.pallas_call(
        functools.partial(_dilated_kernel, tile=tile, nt=nt),
        grid=(n // tile, width // LANES),
        in_specs=[blk, blk, pblk, blk, pblk],
        out_specs=blk,
        out_shape=jax.ShapeDtypeStruct((n, width), BF16),
        scratch_shapes=[pltpu.VMEM((tile, LANES), F32)] * 9,
        compiler_params=pltpu.CompilerParams(
            dimension_semantics=("arbitrary",) * 2, vmem_limit_bytes=VMEM_LIMIT),
        name="dilated",
    )(q, k, k, v, v)


def _flash_scratch(rows, tq):
    tk = tq // 2
    score_bufs = [pltpu.VMEM((tk, rows), F32), pltpu.VMEM((1, rows), F32),
                  pltpu.VMEM((tk, rows), BF16)]
    return [pltpu.VMEM((1, rows), F32), pltpu.VMEM((ACC_ROWS, rows), F32)] + 2 * score_bufs


def _flash_cols(qst, k_block, vt_block, scratch, *, qi, tq):
    m_ref, acc_ref, st_a, cm_a, pt_a, st_b, cm_b, pt_b = scratch
    tk = tq // 2
    segs = range(qst.shape[1] // tq)
    ones = jnp.ones((ACC_ROWS - LANES, tk), BF16)
    tri = (lax.broadcasted_iota(jnp.int32, (tk, tk), 0)
           <= lax.broadcasted_iota(jnp.int32, (tk, tk), 1))

    def scores(blk, st_ref, cm_ref):
        st = jnp.dot(k_block(blk), qst, preferred_element_type=F32)
        st_ref[...] = st
        cm_ref[...] = jnp.max(st, axis=0, keepdims=True)

    def softmax(st_ref, cm_ref, pt_ref):
        m_prev = m_ref[...]
        m_new = jnp.maximum(m_prev, cm_ref[...])
        m_ref[...] = m_new
        pt_ref[...] = jnp.exp2(st_ref[...] - m_new).astype(BF16)
        return jnp.exp2(m_prev - m_new)

    def values(blk, pt_ref):
        lhs = jnp.concatenate([vt_block(blk), ones], axis=0)
        return jnp.dot(lhs, pt_ref[...], preferred_element_type=F32)

    diag0 = 2 * qi
    q_late = jnp.concatenate([qst[:, s * tq + tk:(s + 1) * tq] for s in segs], axis=1)
    s_late = jnp.dot(k_block(diag0 + 1), q_late, preferred_element_type=F32)
    for s in segs:
        early = slice(s * tq, s * tq + tk)
        late = slice(s * tq + tk, (s + 1) * tq)
        st = jnp.where(tri, s_late[:, s * tk:(s + 1) * tk], NEG)
        cm = jnp.max(st, axis=0, keepdims=True)
        m_ref[:, early] = jnp.full((1, tk), NEG, F32)
        m_ref[:, late] = cm
        pt_a[:, early] = jnp.zeros((tk, tk), BF16)
        pt_a[:, late] = jnp.exp2(st - cm).astype(BF16)
    s_full = jnp.dot(k_block(diag0), qst, preferred_element_type=F32)
    st = jnp.concatenate(
        [x for s in segs for x in (jnp.where(tri, s_full[:, s * tq:s * tq + tk], NEG),
                                   s_full[:, s * tq + tk:(s + 1) * tq])], axis=1)
    st_b[...] = st
    cm_b[...] = jnp.max(st, axis=0, keepdims=True)
    acc_ref[...] = jnp.zeros(acc_ref.shape, F32)

    def body(j, carry):
        first = j == 0
        alpha = softmax(st_b, cm_b, pt_b)
        scores(2 * j, st_a, cm_a)
        acc_ref[...] = alpha * (acc_ref[...] + values(jnp.where(first, diag0 + 1, 2 * j - 2), pt_a))
        alpha = softmax(st_a, cm_a, pt_a)
        scores(2 * j + 1, st_b, cm_b)
        acc_ref[...] = alpha * (acc_ref[...] + values(jnp.where(first, diag0, 2 * j - 1), pt_b))
        return carry

    lax.fori_loop(0, qi, body, 0)

    first = qi == 0
    alpha = softmax(st_b, cm_b, pt_b)
    acc_ref[...] = alpha * (acc_ref[...] + values(jnp.where(first, 1, 2 * qi - 2), pt_a))
    acc_ref[...] += values(jnp.where(first, 0, 2 * qi - 1), pt_b)
    return acc_ref[:LANES, :] / acc_ref[LANES:LANES + 1, :]


def _mla_kernel(qn_ref, qr_ref, kn_ref, kr_ref, vt_ref, o_ref, *scratch, tq):
    qi = pl.program_id(2)
    tk = tq // 2
    row = lax.broadcasted_iota(jnp.int32, (LANES, tq), 0)
    zero = jnp.zeros((), BF16)
    qn, qr = qn_ref[...], qr_ref[...]
    q0 = jnp.concatenate([jnp.where(row < B_NOPE, qn, zero),
                          jnp.where(row < B_ROPE, qr, zero)], axis=0)
    q1 = jnp.concatenate([jnp.where(row >= B_NOPE, qn, zero),
                          jnp.where((row >= B_ROPE) & (row < 2 * B_ROPE), qr, zero)], axis=0)
    qst = jnp.concatenate([q0, q1], axis=1)

    def k_block(i):
        rows = pl.ds(pl.multiple_of(i * tk, tk), tk)
        return jnp.concatenate([kn_ref[rows, :], kr_ref[rows, :]], axis=1)

    def vt_block(i):
        return vt_ref[:, pl.ds(pl.multiple_of(i * tk, tk), tk)]

    ot = _flash_cols(qst, k_block, vt_block, scratch, qi=qi, tq=tq)
    pair_t = jnp.concatenate([ot[:B_V, :tq], ot[B_V:, tq:]], axis=0)
    o_ref[...] = pair_t.T.astype(o_ref.dtype)


def _mla_attn(qnt, qrt, kn, kr, vt, *, tq):
    bsz, seq, width = kn.shape
    pairs = width // LANES
    qblk = pl.BlockSpec((None, LANES, tq), lambda b, p, i: (b, p, i))
    kblk = pl.BlockSpec((None, seq, LANES), lambda b, p, i: (b, 0, p))
    krblk = pl.BlockSpec((None, seq, LANES), lambda b, p, i: (b, 0, 0))
    vtblk = pl.BlockSpec((None, LANES, seq), lambda b, p, i: (b, p, 0))
    return pl.pallas_call(
        functools.partial(_mla_kernel, tq=tq),
        grid=(bsz, pairs, seq // tq),
        in_specs=[qblk, qblk, kblk, krblk, vtblk],
        out_specs=pl.BlockSpec((None, tq, LANES), lambda b, p, i: (b, i, p)),
        out_shape=jax.ShapeDtypeStruct((bsz, seq, width), BF16),
        scratch_shapes=_flash_scratch(2 * tq, tq),
        compiler_params=pltpu.CompilerParams(
            dimension_semantics=("arbitrary",) * 3, vmem_limit_bytes=VMEM_LIMIT),
        name="mla_attn",
    )(qnt, qrt, kn, kr, vt)


def _diff_kernel(lam_ref, g_ref, q_ref, k_ref, vt_ref, o_ref, *scratch, tq, lam_init):
    qi = pl.program_id(2)
    tk = tq // 2
    lane = lax.broadcasted_iota(jnp.int32, (tq, LANES), 1)
    row = lax.broadcasted_iota(jnp.int32, (LANES, tq), 0)
    zero = jnp.zeros((), BF16)
    q = q_ref[...]
    qst = jnp.concatenate(
        [jnp.where((row >= j * C_QK) & (row < (j + 1) * C_QK), q, zero) for j in range(4)],
        axis=1)

    def k_block(i):
        return k_ref[pl.ds(pl.multiple_of(i * tk, tk), tk), :]

    def vt_block(i):
        return vt_ref[:, pl.ds(pl.multiple_of(i * tk, tk), tk)]

    ot = _flash_cols(qst, k_block, vt_block, scratch, qi=qi, tq=tq)

    lp = lam_ref[...]
    lam = (jnp.exp(jnp.sum(lp[0:1] * lp[1:2], axis=1, keepdims=True))
           - jnp.exp(jnp.sum(lp[2:3] * lp[3:4], axis=1, keepdims=True)) + lam_init)
    d0 = ot[:C_V, 0:tq] - lam * ot[:C_V, tq:2 * tq]
    d1 = ot[C_V:, 2 * tq:3 * tq] - lam * ot[C_V:, 3 * tq:]
    d = jnp.concatenate([d0, d1], axis=0).T
    low = lane < C_V
    sq = d * d
    ms_lo = jnp.sum(jnp.where(low, sq, 0.0), axis=1, keepdims=True) * (1.0 / C_V)
    ms_hi = jnp.sum(jnp.where(low, 0.0, sq), axis=1, keepdims=True) * (1.0 / C_V)
    rs = jnp.where(low, lax.rsqrt(ms_lo + SUBLN_EPS), lax.rsqrt(ms_hi + SUBLN_EPS))
    o_ref[...] = ((d * rs * g_ref[...]) * (1.0 - lam_init)).astype(o_ref.dtype)


def _diff_attn(lam_params, g2, qt, k, vt, *, tq, lam_init):
    bsz, seq, width = k.shape
    pairs = width // LANES
    qblk = pl.BlockSpec((None, LANES, tq), lambda b, p, i: (b, p, i))
    kblk = pl.BlockSpec((None, seq, LANES), lambda b, p, i: (b, 0, p))
    vtblk = pl.BlockSpec((None, LANES, seq), lambda b, p, i: (b, p, 0))
    const = lambda b, p, i: (0, 0)
    return pl.pallas_call(
        functools.partial(_diff_kernel, tq=tq, lam_init=lam_init),
        grid=(bsz, pairs, seq // tq),
        in_specs=[pl.BlockSpec(lam_params.shape, const), pl.BlockSpec(g2.shape, const),
                  qblk, kblk, vtblk],
        out_specs=pl.BlockSpec((None, tq, LANES), lambda b, p, i: (b, i, p)),
        out_shape=jax.ShapeDtypeStruct((bsz, seq, width), BF16),
        scratch_shapes=_flash_scratch(4 * tq, tq),
        compiler_params=pltpu.CompilerParams(
            dimension_semantics=("arbitrary",) * 3, vmem_limit_bytes=VMEM_LIMIT),
        name="diff_attn",
    )(lam_params, g2, qt, k, vt)


def _post_kernel(x_ref, oa_ref, ob_ref, oc_ref, wo_ref, g2_ref, wup_ref, wdn_ref, gf_ref,
                 y_ref, *, final):
    mixed = jnp.concatenate([oa_ref[...], ob_ref[...], oc_ref[...]], axis=1)
    x1 = x_ref[...] + jnp.dot(mixed, wo_ref[...], preferred_element_type=F32)
    hn = _rms(x1, g2_ref[...], NORM_EPS).astype(BF16)
    y_ref[...] = x1
    for c in range(wup_ref.shape[1] // D_FF_CHUNK):
        cs = slice(c * D_FF_CHUNK, (c + 1) * D_FF_CHUNK)
        u = jnp.dot(hn, wup_ref[:, cs], preferred_element_type=F32)
        a = jnp.square(jnp.maximum(u, 0.0)).astype(BF16)
        y_ref[...] += jnp.dot(a, wdn_ref[cs, :], preferred_element_type=F32)
    if final:
        y_ref[...] = _rms(y_ref[...], gf_ref[...], NORM_EPS)


def _post(x2, oa, ob, oc, wo, g2, wup, wdn, gf, *, tm, final):
    n, d = x2.shape
    row = lambda i: (i, 0)
    const = lambda i: (0, 0)
    return pl.pallas_call(
        functools.partial(_post_kernel, final=final),
        grid=(n // tm,),
        in_specs=[pl.BlockSpec((tm, d), row),
                  pl.BlockSpec((tm, oa.shape[1]), row),
                  pl.BlockSpec((tm, ob.shape[1]), row),
                  pl.BlockSpec((tm, oc.shape[1]), row),
                  pl.BlockSpec(wo.shape, const), pl.BlockSpec(g2.shape, const),
                  pl.BlockSpec(wup.shape, const), pl.BlockSpec(wdn.shape, const),
                  pl.BlockSpec(gf.shape, const)],
        out_specs=pl.BlockSpec((tm, d), row),
        out_shape=jax.ShapeDtypeStruct((n, d), F32),
        compiler_params=pltpu.CompilerParams(
            dimension_semantics=("arbitrary",), vmem_limit_bytes=VMEM_LIMIT),
        name="post_final" if final else "post",
    )(x2, oa, ob, oc, wo, g2, wup, wdn, gf)


def _rope_tables(seq):
    def tables(dim):
        half = dim // 2
        inv = 1.0 / (ROPE_THETA ** (jnp.arange(half, dtype=F32) / half))
        ang = jnp.arange(seq, dtype=F32)[:, None] * inv[None, :]
        cos, sin = jnp.cos(ang), jnp.sin(ang)
        reps = LANES // dim
        cos_t = jnp.tile(jnp.concatenate([cos, cos], axis=1), (1, reps))
        sin_t = jnp.tile(jnp.concatenate([-sin, sin], axis=1), (1, reps))
        return cos_t, sin_t
    cosa, sina = tables(A_DIM)
    cos32, sin32 = tables(C_QK)
    return cosa, sina, cos32, sin32


def _prep_layer_weights(w_in, w_uq, w_ukv):
    cols = jnp.split(w_in, [384, 768, 1152, 1536, 1664, 1696, 1952, 2208], axis=1)
    qa, ka, va, cq, ckv, kr, qc, kc, vc = cols
    w1 = jnp.concatenate([qa, ka, va, cq, ckv, jnp.tile(kr, (1, LANES // B_ROPE)), qc, kc, vc],
                         axis=1).astype(BF16)
    uq = w_uq.reshape(B_Q_LORA, B_HEADS, B_NOPE + B_ROPE)
    nope = uq[:, :, :B_NOPE].reshape(B_Q_LORA, B_HEADS * B_NOPE)
    rope = uq[:, :, B_NOPE:].reshape(B_Q_LORA, B_HEADS // 2, 2 * B_ROPE)
    rope = jnp.pad(rope, ((0, 0), (0, 0), (0, LANES - 2 * B_ROPE))).reshape(B_Q_LORA, -1)
    wuq = jnp.concatenate([nope, rope], axis=1).astype(BF16)
    ukv = w_ukv.reshape(B_KV_LORA, B_HEADS, B_NOPE + B_V)
    wukv = jnp.concatenate([ukv[:, :, :B_NOPE].reshape(B_KV_LORA, -1),
                            ukv[:, :, B_NOPE:].reshape(B_KV_LORA, -1)], axis=1).astype(BF16)
    return w1, wuq, wukv


def kernel(x, ln1_g, w_in, mla_q_norm_g, mla_w_uq, mla_kv_norm_g, mla_w_ukv, diff_lambda,
           diff_subln_g, w_o, ln2_g, w_up, w_down, final_g):
    bsz, seq, d_model = x.shape
    depth = w_in.shape[0]
    tm = 512
    tq = 512
    tabs = _rope_tables(seq)
    x2 = x.reshape(bsz * seq, d_model)
    gf = final_g.reshape(1, d_model)
    sh = lambda t, w: t.reshape(bsz, seq, w)

    for layer in range(depth):
        lam_init = 0.8 - 0.6 * math.exp(-0.3 * layer)
        w1, wuq, wukv = _prep_layer_weights(w_in[layer], mla_w_uq[layer], mla_w_ukv[layer])
        qa, ka, va, qn, qr, kn, vb, kr, qc, kc, vc = _inproj(
            x2, ln1_g[layer].reshape(1, -1), w1, tabs,
            mla_q_norm_g[layer].reshape(1, -1), wuq,
            mla_kv_norm_g[layer].reshape(1, -1), wukv, seq=seq, tm=2 * tm)

        out_a = _dilated_mixer(qa, ka, va, seq=seq, tile=2048)
        out_b = _mla_attn(qn, qr, sh(kn, 384), sh(kr, 128), vb, tq=min(2 * tq, seq))
        g_sub = jnp.tile(diff_subln_g[layer], LANES // C_V).reshape(1, LANES)
        out_c = _diff_attn(diff_lambda[layer], g_sub, qc, sh(kc, 256), vc,
                           tq=min(2 * tq, seq), lam_init=lam_init)

        x2 = _post(x2, out_a, out_b.reshape(-1, B_WIDTH),
                   out_c.reshape(-1, C_WIDTH), w_o[layer].astype(BF16),
                   ln2_g[layer].reshape(1, -1), w_up[layer].astype(BF16),
                   w_down[layer].astype(BF16), gf, tm=tm, final=layer == depth - 1)

    return x2.reshape(bsz, seq, d_model)
```

```python
import functools
import math

import jax
import jax.numpy as jnp
from jax import lax
from jax.experimental import pallas as pl
from jax.experimental.pallas import tpu as pltpu

F32 = jnp.float32
BF16 = jnp.bfloat16

LANES = 128
ACC_ROWS = LANES + 16
VMEM_LIMIT = 56 * 1024 * 1024

ROPE_THETA = 10000.0
NORM_EPS = 1e-6
SUBLN_EPS = 1e-5
NEG = -1e30
LOG2E = math.log2(math.e)

A_HEADS, A_DIM = 6, 64
A_CONFIGS = ((128, 1), (512, 4), (2048, 16))
A_SPAN = 128
B_HEADS, B_NOPE, B_ROPE, B_V = 6, 64, 32, 64
B_Q_LORA, B_KV_LORA = 384, 128
C_HEADS, C_QK = 4, 32
C_V = 2 * C_QK
A_WIDTH = A_HEADS * A_DIM
B_WIDTH = B_HEADS * B_V
C_WIDTH = C_HEADS * C_V
D_FF_CHUNK = 512
INPROJ_SUB = 256


def _rms(x, g, eps):
    return x * lax.rsqrt(jnp.mean(x * x, axis=-1, keepdims=True) + eps) * g


def _rope(x, cos, sin_signed, half):
    lane = lax.broadcasted_iota(jnp.int32, (x.shape[0], LANES), 1)
    first = (lane & (2 * half - 1)) < half
    outs = []
    for c in range(x.shape[1] // LANES):
        xc = x[:, c * LANES:(c + 1) * LANES]
        partner = jnp.where(first, pltpu.roll(xc, LANES - half, 1), pltpu.roll(xc, half, 1))
        outs.append(xc * cos + partner * sin_signed)
    return outs[0] if len(outs) == 1 else jnp.concatenate(outs, axis=1)


def _nt_dot(a, b):
    return lax.dot_general(a, b, (((1,), (1,)), ((), ())), preferred_element_type=F32)


def _inproj_kernel(x_ref, g_ref, w1_ref, cosa_ref, sina_ref, cos32_ref, sin32_ref,
                   qg_ref, wuq_ref, kvg_ref, wukv_ref,
                   qa_ref, ka_ref, va_ref, qn_ref, qr_ref, kn_ref, vb_ref, kr_ref,
                   qc_ref, kc_ref, vc_ref):
    scale_a = A_DIM ** -0.5 * LOG2E
    scale_b = (B_NOPE + B_ROPE) ** -0.5 * LOG2E
    scale_c = C_QK ** -0.5 * LOG2E

    for r0 in range(0, x_ref.shape[0], INPROJ_SUB):
        rows = slice(r0, r0 + INPROJ_SUB)
        h = _rms(x_ref[rows, :], g_ref[...], NORM_EPS).astype(BF16)
        p_all = jnp.dot(h, w1_ref[...], preferred_element_type=F32)

        def proj(lo, width):
            return p_all[:, lo:lo + width]

        cosa, sina = cosa_ref[rows, :], sina_ref[rows, :]
        cos32, sin32 = cos32_ref[rows, :], sin32_ref[rows, :]

        qa_ref[rows, :] = _rope(proj(0, 384), cosa, sina, 32) * scale_a
        ka_ref[rows, :] = _rope(proj(384, 384), cosa, sina, 32)
        va_ref[rows, :] = proj(768, 384)

        cq = _rms(proj(1152, 384), qg_ref[...], NORM_EPS).astype(BF16)
        qf = jnp.dot(cq, wuq_ref[...], preferred_element_type=F32)
        qn_ref[:, rows] = (qf[:, :384] * scale_b).T.astype(BF16)
        qr_ref[:, rows] = (_rope(qf[:, 384:], cos32, sin32, 16) * scale_b).T.astype(BF16)

        ckv = _rms(proj(1536, 128), kvg_ref[...], NORM_EPS).astype(BF16)
        kvf = jnp.dot(ckv, wukv_ref[...], preferred_element_type=F32)
        kn_ref[rows, :] = kvf[:, :384].astype(BF16)
        vb_ref[:, rows] = kvf[:, 384:].T.astype(BF16)
        kr_ref[rows, :] = _rope(proj(1664, 128), cos32, sin32, 16).astype(BF16)

        qc_ref[:, rows] = (_rope(proj(1792, 256), cos32, sin32, 16) * scale_c).T.astype(BF16)
        kc_ref[rows, :] = _rope(proj(2048, 256), cos32, sin32, 16).astype(BF16)
        vc_ref[:, rows] = proj(2304, 256).T.astype(BF16)


def _inproj(x2, g, w1, tabs, qg, wuq, kvg, wukv, *, seq, tm):
    n = x2.shape[0]
    nt = n // tm
    st = seq // tm
    row = lambda i: (i, 0)
    const = lambda i: (0, 0)
    tab = lambda i: (i % st, 0)
    widths = (384, 384, 384, 384, 384, 384, 384, 128, 256, 256, 256)
    major = (False, False, False, True, True, False, True, False, True, False, True)
    bsz = n // seq
    col = lambda i: (i // st, 0, i % st)
    out_specs = [pl.BlockSpec((None, w, tm), col) if t else pl.BlockSpec((tm, w), row)
                 for w, t in zip(widths, major)]
    dtypes = (F32,) * 3 + (BF16,) * 8
    out_shape = [jax.ShapeDtypeStruct((bsz, w, seq) if t else (n, w), dt)
                 for w, t, dt in zip(widths, major, dtypes)]
    return pl.pallas_call(
        _inproj_kernel,
        grid=(nt,),
        in_specs=[
            pl.BlockSpec((tm, x2.shape[1]), row),
            pl.BlockSpec(g.shape, const),
            pl.BlockSpec(w1.shape, const),
            pl.BlockSpec((tm, LANES), tab), pl.BlockSpec((tm, LANES), tab),
            pl.BlockSpec((tm, LANES), tab), pl.BlockSpec((tm, LANES), tab),
            pl.BlockSpec(qg.shape, const), pl.BlockSpec(wuq.shape, const),
            pl.BlockSpec(kvg.shape, const), pl.BlockSpec(wukv.shape, const),
        ],
        out_specs=out_specs,
        out_shape=out_shape,
        compiler_params=pltpu.CompilerParams(
            dimension_semantics=("arbitrary",), vmem_limit_bytes=VMEM_LIMIT),
        name="inproj",
    )(x2, g, w1, *tabs, qg, wuq, kvg, wukv)


def _dilated_kernel(q_ref, k_ref, kp_ref, v_ref, vp_ref, o_ref,
                    q4_ref, k4_ref, kp4_ref, v4_ref, vp4_ref, o4_ref, l4_ref, ot_ref, lt_ref,
                    *, tile, nt):
    quarter = tile // 4
    t = pl.program_id(0) % nt
    lane = lax.broadcasted_iota(jnp.int32, (A_SPAN, LANES), 1)
    low = lane < A_DIM
    qi = lax.broadcasted_iota(jnp.int32, (2 * A_SPAN, 2 * A_SPAN), 0) & (A_SPAN - 1)
    ki = lax.broadcasted_iota(jnp.int32, (2 * A_SPAN, 2 * A_SPAN), 1)
    dist = qi + A_SPAN - ki
    band = (dist >= 0) & (dist <= A_SPAN)
    band0 = band & (ki >= jnp.where(t > 0, 0, A_SPAN))
    ones = jnp.ones((2 * A_SPAN, LANES), BF16)
    zero = jnp.zeros((), BF16)

    for src, dst in ((q_ref, q4_ref), (k_ref, k4_ref), (kp_ref, kp4_ref),
                     (v_ref, v4_ref), (vp_ref, vp4_ref)):
        for c in range(4):
            dst[c * quarter:(c + 1) * quarter, :] = src[pl.ds(c, quarter, stride=4), :]

    def attend(qg, k_prev, k_cur, v_prev, v_cur, valid):
        qg = qg.astype(BF16)
        qs = jnp.concatenate([jnp.where(low, qg, zero), jnp.where(low, zero, qg)], axis=0)
        kk = jnp.concatenate([k_prev, k_cur], axis=0).astype(BF16)
        vv = jnp.concatenate([v_prev, v_cur], axis=0).astype(BF16)
        s = jnp.where(valid, _nt_dot(qs, kk), NEG)
        m = jnp.max(s, axis=1, keepdims=True)
        p = jnp.exp2(s - m).astype(BF16)
        pv = jnp.dot(p, jnp.concatenate([vv, ones], axis=1), preferred_element_type=F32)
        num = jnp.where(low, pv[:A_SPAN, :LANES], pv[A_SPAN:, :LANES])
        den = jnp.where(low, pv[:A_SPAN, LANES:], pv[A_SPAN:, LANES:])
        mm = jnp.where(low, m[:A_SPAN], m[A_SPAN:])
        return num / den, mm + jnp.log2(den)

    def merge(o, lse, o_prev, l_prev):
        mx = jnp.maximum(lse, l_prev)
        wa = jnp.exp2(lse - mx)
        wb = jnp.exp2(l_prev - mx)
        tot = wa + wb
        return (wa * o + wb * o_prev) / tot, mx + jnp.log2(tot)

    for c in range(4):
        for g in range(quarter // A_SPAN):
            cur = pl.ds(c * quarter + g * A_SPAN, A_SPAN)
            if g == 0:
                prev = pl.ds((c + 1) * quarter - A_SPAN, A_SPAN)
                k_prev, v_prev, valid = kp4_ref[prev, :], vp4_ref[prev, :], band0
            else:
                prev = pl.ds(c * quarter + (g - 1) * A_SPAN, A_SPAN)
                k_prev, v_prev, valid = k4_ref[prev, :], v4_ref[prev, :], band
            o, lse = attend(q4_ref[cur, :], k_prev, k4_ref[cur, :], v_prev, v4_ref[cur, :], valid)
            o4_ref[cur, :] = o
            l4_ref[cur, :] = lse

    assert tile == A_SPAN * 16
    for c in range(4):
        for a in range(4):
            cur = pl.ds(c * quarter + a, A_SPAN, stride=4)
            o, lse = attend(q4_ref[cur, :], kp4_ref[cur, :], k4_ref[cur, :],
                            vp4_ref[cur, :], v4_ref[cur, :], band0)
            o, lse = merge(o, lse, o4_ref[cur, :], l4_ref[cur, :])
            o4_ref[cur, :] = o
            l4_ref[cur, :] = lse

    for c in range(4):
        ot_ref[pl.ds(c, quarter, stride=4), :] = o4_ref[c * quarter:(c + 1) * quarter, :]
        lt_ref[pl.ds(c, quarter, stride=4), :] = l4_ref[c * quarter:(c + 1) * quarter, :]

    for g in range(tile // A_SPAN):
        cur = pl.ds(g * A_SPAN, A_SPAN)
        if g == 0:
            prev = pl.ds(tile - A_SPAN, A_SPAN)
            k_prev, v_prev, valid = kp_ref[prev, :], vp_ref[prev, :], band0
        else:
            prev = pl.ds((g - 1) * A_SPAN, A_SPAN)
            k_prev, v_prev, valid = k_ref[prev, :], v_ref[prev, :], band
        o, lse = attend(q_ref[cur, :], k_prev, k_ref[cur, :], v_prev, v_ref[cur, :], valid)
        o, _ = merge(o, lse, ot_ref[cur, :], lt_ref[cur, :])
        o_ref[cur, :] = o.astype(o_ref.dtype)


def _dilated_mixer(q, k, v, *, seq, tile):
    n, width = q.shape
    nt = seq // tile
    for window, dilation in A_CONFIGS:
        assert window // dilation == A_SPAN and tile % window == 0
    cur = lambda i, p: (i, p)
    prev = lambda i, p: (jnp.where(i % nt == 0, i, i - 1), p)
    blk = pl.BlockSpec((tile, LANES), cur)
    pblk = pl.BlockSpec((tile, LANES), prev)
    return pl.pallas_call(
        functools.partial(_dilated_kernel, tile=tile, nt=nt),
        grid=(n // tile, width // LANES),
        in_specs=[blk, blk, pblk, blk, pblk],
        out_specs=blk,
        out_shape=jax.ShapeDtypeStruct((n, width), BF16),
        scratch_shapes=[pltpu.VMEM((tile, LANES), F32)] * 9,
        compiler_params=pltpu.CompilerParams(
            dimension_semantics=("arbitrary",) * 2, vmem_limit_bytes=VMEM_LIMIT),
        name="dilated",
    )(q, k, k, v, v)


def _flash_scratch(rows, tq):
    tk = tq // 2
    score_bufs = [pltpu.VMEM((tk, rows), F32), pltpu.VMEM((1, rows), F32),
                  pltpu.VMEM((tk, rows), BF16)]
    return [pltpu.VMEM((1, rows), F32), pltpu.VMEM((ACC_ROWS, rows), F32)] + 2 * score_bufs


def _flash_cols(qst, k_block, vt_block, scratch, *, qi, tq):
    m_ref, acc_ref, st_a, cm_a, pt_a, st_b, cm_b, pt_b = scratch
    tk = tq // 2
    segs = range(qst.shape[1] // tq)
    ones = jnp.ones((ACC_ROWS - LANES, tk), BF16)
    tri = (lax.broadcasted_iota(jnp.int32, (tk, tk), 0)
           <= lax.broadcasted_iota(jnp.int32, (tk, tk), 1))

    def scores(blk, st_ref, cm_ref):
        st = jnp.dot(k_block(blk), qst, preferred_element_type=F32)
        st_ref[...] = st
        cm_ref[...] = jnp.max(st, axis=0, keepdims=True)

    def softmax(st_ref, cm_ref, pt_ref):
        m_prev = m_ref[...]
        m_new = jnp.maximum(m_prev, cm_ref[...])
        m_ref[...] = m_new
        pt_ref[...] = jnp.exp2(st_ref[...] - m_new).astype(BF16)
        return jnp.exp2(m_prev - m_new)

    def values(blk, pt_ref):
        lhs = jnp.concatenate([vt_block(blk), ones], axis=0)
        return jnp.dot(lhs, pt_ref[...], preferred_element_type=F32)

    diag0 = 2 * qi
    q_late = jnp.concatenate([qst[:, s * tq + tk:(s + 1) * tq] for s in segs], axis=1)
    s_late = jnp.dot(k_block(diag0 + 1), q_late, preferred_element_type=F32)
    for s in segs:
        early = slice(s * tq, s * tq + tk)
        late = slice(s * tq + tk, (s + 1) * tq)
        st = jnp.where(tri, s_late[:, s * tk:(s + 1) * tk], NEG)
        cm = jnp.max(st, axis=0, keepdims=True)
        m_ref[:, early] = jnp.full((1, tk), NEG, F32)
        m_ref[:, late] = cm
        pt_a[:, early] = jnp.zeros((tk, tk), BF16)
        pt_a[:, late] = jnp.exp2(st - cm).astype(BF16)
    s_full = jnp.dot(k_block(diag0), qst, preferred_element_type=F32)
    st = jnp.concatenate(
        [x for s in segs for x in (jnp.where(tri, s_full[:, s * tq:s * tq + tk], NEG),
                                   s_full[:, s * tq + tk:(s + 1) * tq])], axis=1)
    st_b[...] = st
    cm_b[...] = jnp.max(st, axis=0, keepdims=True)
    acc_ref[...] = jnp.zeros(acc_ref.shape, F32)

    def body(j, carry):
        first = j == 0
        alpha = softmax(st_b, cm_b, pt_b)
        scores(2 * j, st_a, cm_a)
        acc_ref[...] = alpha * (acc_ref[...] + values(jnp.where(first, diag0 + 1, 2 * j - 2), pt_a))
        alpha = softmax(st_a, cm_a, pt_a)
        scores(2 * j + 1, st_b, cm_b)
        acc_ref[...] = alpha * (acc_ref[...] + values(jnp.where(first, diag0, 2 * j - 1), pt_b))
        return carry

    lax.fori_loop(0, qi, body, 0)

    first = qi == 0
    alpha = softmax(st_b, cm_b, pt_b)
    acc_ref[...] = alpha * (acc_ref[...] + values(jnp.where(first, 1, 2 * qi - 2), pt_a))
    acc_ref[...] += values(jnp.where(first, 0, 2 * qi - 1), pt_b)
    return acc_ref[:LANES, :] / acc_ref[LANES:LANES + 1, :]


def _mla_kernel(qn_ref, qr_ref, kn_ref, kr_ref, vt_ref, o_ref, *scratch, tq):
    tk = tq // 2
    row = lax.broadcasted_iota(jnp.int32, (LANES, tq), 0)
    zero = jnp.zeros((), BF16)

    def k_block(i):
        rows = pl.ds(pl.multiple_of(i * tk, tk), tk)
        return jnp.concatenate([kn_ref[rows, :], kr_ref[rows, :]], axis=1)

    def vt_block(i):
        return vt_ref[:, pl.ds(pl.multiple_of(i * tk, tk), tk)]

    def query_block(qi, carry):
        span = pl.ds(pl.multiple_of(qi * tq, tq), tq)
        qn, qr = qn_ref[:, span], qr_ref[:, span]
        q0 = jnp.concatenate([jnp.where(row < B_NOPE, qn, zero),
                              jnp.where(row < B_ROPE, qr, zero)], axis=0)
        q1 = jnp.concatenate([jnp.where(row >= B_NOPE, qn, zero),
                              jnp.where((row >= B_ROPE) & (row < 2 * B_ROPE), qr, zero)], axis=0)
        qst = jnp.concatenate([q0, q1], axis=1)
        ot = _flash_cols(qst, k_block, vt_block, scratch, qi=qi, tq=tq)
        pair_t = jnp.concatenate([ot[:B_V, :tq], ot[B_V:, tq:]], axis=0)
        o_ref[span, :] = pair_t.T.astype(o_ref.dtype)
        return carry

    lax.fori_loop(0, o_ref.shape[0] // tq, query_block, 0)


def _mla_attn(qnt, qrt, kn, kr, vt, *, tq):
    bsz, seq, width = kn.shape
    pairs = width // LANES
    tblk = pl.BlockSpec((None, LANES, seq), lambda b, p: (b, p, 0))
    kblk = pl.BlockSpec((None, seq, LANES), lambda b, p: (b, 0, p))
    krblk = pl.BlockSpec((None, seq, LANES), lambda b, p: (b, 0, 0))
    return pl.pallas_call(
        functools.partial(_mla_kernel, tq=tq),
        grid=(bsz, pairs),
        in_specs=[tblk, tblk, kblk, krblk, tblk],
        out_specs=kblk,
        out_shape=jax.ShapeDtypeStruct((bsz, seq, width), BF16),
        scratch_shapes=_flash_scratch(2 * tq, tq),
        compiler_params=pltpu.CompilerParams(
            dimension_semantics=("arbitrary",) * 2, vmem_limit_bytes=VMEM_LIMIT),
        name="mla_attn",
    )(qnt, qrt, kn, kr, vt)


def _diff_kernel(lam_ref, g_ref, q_ref, k_ref, vt_ref, o_ref, *scratch, tq, lam_init):
    tk = tq // 2
    lane = lax.broadcasted_iota(jnp.int32, (tq, LANES), 1)
    row = lax.broadcasted_iota(jnp.int32, (LANES, tq), 0)
    low = lane < C_V
    zero = jnp.zeros((), BF16)
    lp = lam_ref[...]
    lam = (jnp.exp(jnp.sum(lp[0:1] * lp[1:2], axis=1, keepdims=True))
           - jnp.exp(jnp.sum(lp[2:3] * lp[3:4], axis=1, keepdims=True)) + lam_init)

    def k_block(i):
        return k_ref[pl.ds(pl.multiple_of(i * tk, tk), tk), :]

    def vt_block(i):
        return vt_ref[:, pl.ds(pl.multiple_of(i * tk, tk), tk)]

    def query_block(qi, carry):
        span = pl.ds(pl.multiple_of(qi * tq, tq), tq)
        q = q_ref[:, span]
        qst = jnp.concatenate(
            [jnp.where((row >= j * C_QK) & (row < (j + 1) * C_QK), q, zero) for j in range(4)],
            axis=1)
        ot = _flash_cols(qst, k_block, vt_block, scratch, qi=qi, tq=tq)
        d0 = ot[:C_V, 0:tq] - lam * ot[:C_V, tq:2 * tq]
        d1 = ot[C_V:, 2 * tq:3 * tq] - lam * ot[C_V:, 3 * tq:]
        d = jnp.concatenate([d0, d1], axis=0).T
        sq = d * d
        ms_lo = jnp.sum(jnp.where(low, sq, 0.0), axis=1, keepdims=True) * (1.0 / C_V)
        ms_hi = jnp.sum(jnp.where(low, 0.0, sq), axis=1, keepdims=True) * (1.0 / C_V)
        rs = jnp.where(low, lax.rsqrt(ms_lo + SUBLN_EPS), lax.rsqrt(ms_hi + SUBLN_EPS))
        o_ref[span, :] = ((d * rs * g_ref[...]) * (1.0 - lam_init)).astype(o_ref.dtype)
        return carry

    lax.fori_loop(0, o_ref.shape[0] // tq, query_block, 0)


def _diff_attn(lam_params, g2, qt, k, vt, *, tq, lam_init):
    bsz, seq, width = k.shape
    pairs = width // LANES
    tblk = pl.BlockSpec((None, LANES, seq), lambda b, p: (b, p, 0))
    kblk = pl.BlockSpec((None, seq, LANES), lambda b, p: (b, 0, p))
    const = lambda b, p: (0, 0)
    return pl.pallas_call(
        functools.partial(_diff_kernel, tq=tq, lam_init=lam_init),
        grid=(bsz, pairs),
        in_specs=[pl.BlockSpec(lam_params.shape, const), pl.BlockSpec(g2.shape, const),
                  tblk, kblk, tblk],
        out_specs=kblk,
        out_shape=jax.ShapeDtypeStruct((bsz, seq, width), BF16),
        scratch_shapes=_flash_scratch(4 * tq, tq),
        compiler_params=pltpu.CompilerParams(
            dimension_semantics=("arbitrary",) * 2, vmem_limit_bytes=VMEM_LIMIT),
        name="diff_attn",
    )(lam_params, g2, qt, k, vt)


def _post_kernel(x_ref, oa_ref, ob_ref, oc_ref, wo_ref, g2_ref, wup_ref, wdn_ref, gf_ref,
                 y_ref, *, final):
    mixed = jnp.concatenate([oa_ref[...], ob_ref[...], oc_ref[...]], axis=1)
    x1 = x_ref[...] + jnp.dot(mixed, wo_ref[...], preferred_element_type=F32)
    hn = _rms(x1, g2_ref[...], NORM_EPS).astype(BF16)
    y_ref[...] = x1
    for c in range(wup_ref.shape[1] // D_FF_CHUNK):
        cs = slice(c * D_FF_CHUNK, (c + 1) * D_FF_CHUNK)
        u = jnp.dot(hn, wup_ref[:, cs], preferred_element_type=F32)
        a = jnp.square(jnp.maximum(u, 0.0)).astype(BF16)
        y_ref[...] += jnp.dot(a, wdn_ref[cs, :], preferred_element_type=F32)
    if final:
        y_ref[...] = _rms(y_ref[...], gf_ref[...], NORM_EPS)


def _post(x2, oa, ob, oc, wo, g2, wup, wdn, gf, *, tm, final):
    n, d = x2.shape
    row = lambda i: (i, 0)
    const = lambda i: (0, 0)
    return pl.pallas_call(
        functools.partial(_post_kernel, final=final),
        grid=(n // tm,),
        in_specs=[pl.BlockSpec((tm, d), row),
                  pl.BlockSpec((tm, oa.shape[1]), row),
                  pl.BlockSpec((tm, ob.shape[1]), row),
                  pl.BlockSpec((tm, oc.shape[1]), row),
                  pl.BlockSpec(wo.shape, const), pl.BlockSpec(g2.shape, const),
                  pl.BlockSpec(wup.shape, const), pl.BlockSpec(wdn.shape, const),
                  pl.BlockSpec(gf.shape, const)],
        out_specs=pl.BlockSpec((tm, d), row),
        out_shape=jax.ShapeDtypeStruct((n, d), F32),
        compiler_params=pltpu.CompilerParams(
            dimension_semantics=("arbitrary",), vmem_limit_bytes=VMEM_LIMIT),
        name="post_final" if final else "post",
    )(x2, oa, ob, oc, wo, g2, wup, wdn, gf)


def _rope_tables(seq):
    def tables(dim):
        half = dim // 2
        inv = 1.0 / (ROPE_THETA ** (jnp.arange(half, dtype=F32) / half))
        ang = jnp.arange(seq, dtype=F32)[:, None] * inv[None, :]
        cos, sin = jnp.cos(ang), jnp.sin(ang)
        reps = LANES // dim
        cos_t = jnp.tile(jnp.concatenate([cos, cos], axis=1), (1, reps))
        sin_t = jnp.tile(jnp.concatenate([-sin, sin], axis=1), (1, reps))
        return cos_t, sin_t
    cosa, sina = tables(A_DIM)
    cos32, sin32 = tables(C_QK)
    return cosa, sina, cos32, sin32


def _prep_layer_weights(w_in, w_uq, w_ukv):
    cols = jnp.split(w_in, [384, 768, 1152, 1536, 1664, 1696, 1952, 2208], axis=1)
    qa, ka, va, cq, ckv, kr, qc, kc, vc = cols
    w1 = jnp.concatenate([qa, ka, va, cq, ckv, jnp.tile(kr, (1, LANES // B_ROPE)), qc, kc, vc],
                         axis=1).astype(BF16)
    uq = w_uq.reshape(B_Q_LORA, B_HEADS, B_NOPE + B_ROPE)
    nope = uq[:, :, :B_NOPE].reshape(B_Q_LORA, B_HEADS * B_NOPE)
    rope = uq[:, :, B_NOPE:].reshape(B_Q_LORA, B_HEADS // 2, 2 * B_ROPE)
    rope = jnp.pad(rope, ((0, 0), (0, 0), (0, LANES - 2 * B_ROPE))).reshape(B_Q_LORA, -1)
    wuq = jnp.concatenate([nope, rope], axis=1).astype(BF16)
    ukv = w_ukv.reshape(B_KV_LORA, B_HEADS, B_NOPE + B_V)
    wukv = jnp.concatenate([ukv[:, :, :B_NOPE].reshape(B_KV_LORA, -1),
                            ukv[:, :, B_NOPE:].reshape(B_KV_LORA, -1)], axis=1).astype(BF16)
    return w1, wuq, wukv


def kernel(x, ln1_g, w_in, mla_q_norm_g, mla_w_uq, mla_kv_norm_g, mla_w_ukv, diff_lambda,
           diff_subln_g, w_o, ln2_g, w_up, w_down, final_g):
    bsz, seq, d_model = x.shape
    depth = w_in.shape[0]
    tm = 512
    tq = 512
    tabs = _rope_tables(seq)
    x2 = x.reshape(bsz * seq, d_model)
    gf = final_g.reshape(1, d_model)
    sh = lambda t, w: t.reshape(bsz, seq, w)

    for layer in range(depth):
        lam_init = 0.8 - 0.6 * math.exp(-0.3 * layer)
        w1, wuq, wukv = _prep_layer_weights(w_in[layer], mla_w_uq[layer], mla_w_ukv[layer])
        qa, ka, va, qn, qr, kn, vb, kr, qc, kc, vc = _inproj(
            x2, ln1_g[layer].reshape(1, -1), w1, tabs,
            mla_q_norm_g[layer].reshape(1, -1), wuq,
            mla_kv_norm_g[layer].reshape(1, -1), wukv, seq=seq, tm=2 * tm)

        out_a = _dilated_mixer(qa, ka, va, seq=seq, tile=2048)
        out_b = _mla_attn(qn, qr, sh(kn, 384), sh(kr, 128), vb, tq=min(2 * tq, seq))
        g_sub = jnp.tile(diff_subln_g[layer], LANES // C_V).reshape(1, LANES)
        out_c = _diff_attn(diff_lambda[layer], g_sub, qc, sh(kc, 256), vc,
                           tq=min(2 * tq, seq), lam_init=lam_init)

        x2 = _post(x2, out_a, out_b.reshape(-1, B_WIDTH),
                   out_c.reshape(-1, C_WIDTH), w_o[layer].astype(BF16),
                   ln2_g[layer].reshape(1, -1), w_up[layer].astype(BF16),
                   w_down[layer].astype(BF16), gf, tm=tm, final=layer == depth - 1)

    return x2.reshape(bsz, seq, d_model)
```

```python
import functools
import math

import jax
import jax.numpy as jnp
from jax import lax
from jax.experimental import pallas as pl
from jax.experimental.pallas import tpu as pltpu

F32 = jnp.float32
BF16 = jnp.bfloat16

LANES = 128
HEAD_V = 64
ACC_ROWS = HEAD_V + 16
VMEM_LIMIT = 56 * 1024 * 1024

ROPE_THETA = 10000.0
NORM_EPS = 1e-6
SUBLN_EPS = 1e-5
NEG = -1e30
LOG2E = math.log2(math.e)

A_HEADS, A_DIM = 6, 64
A_CONFIGS = ((128, 1), (512, 4), (2048, 16))
A_SPAN = 128
B_HEADS, B_NOPE, B_ROPE, B_V = 6, 64, 32, 64
B_Q_LORA, B_KV_LORA = 384, 128
C_HEADS, C_QK = 4, 32
C_V = 2 * C_QK
A_WIDTH = A_HEADS * A_DIM
B_WIDTH = B_HEADS * B_V
C_WIDTH = C_HEADS * C_V
D_FF_CHUNK = 512
INPROJ_SUB = 256


def _rms(x, g, eps):
    return x * lax.rsqrt(jnp.mean(x * x, axis=-1, keepdims=True) + eps) * g


def _rope(x, cos, sin_signed, half):
    lane = lax.broadcasted_iota(jnp.int32, (x.shape[0], LANES), 1)
    first = (lane & (2 * half - 1)) < half
    outs = []
    for c in range(x.shape[1] // LANES):
        xc = x[:, c * LANES:(c + 1) * LANES]
        partner = jnp.where(first, pltpu.roll(xc, LANES - half, 1), pltpu.roll(xc, half, 1))
        outs.append(xc * cos + partner * sin_signed)
    return outs[0] if len(outs) == 1 else jnp.concatenate(outs, axis=1)


def _nt_dot(a, b):
    return lax.dot_general(a, b, (((1,), (1,)), ((), ())), preferred_element_type=F32)


def _inproj_kernel(x_ref, g_ref, w1_ref, cosa_ref, sina_ref, cos32_ref, sin32_ref,
                   qg_ref, wuq_ref, kvg_ref, wukv_ref,
                   qa_ref, ka_ref, va_ref, qn_ref, qr_ref, kn_ref, vb_ref, kr_ref,
                   qc_ref, kc_ref, vc_ref):
    scale_a = A_DIM ** -0.5 * LOG2E
    scale_b = (B_NOPE + B_ROPE) ** -0.5 * LOG2E
    scale_c = C_QK ** -0.5 * LOG2E

    for r0 in range(0, x_ref.shape[0], INPROJ_SUB):
        rows = slice(r0, r0 + INPROJ_SUB)
        h = _rms(x_ref[rows, :], g_ref[...], NORM_EPS).astype(BF16)
        p_all = jnp.dot(h, w1_ref[...], preferred_element_type=F32)

        def proj(lo, width):
            return p_all[:, lo:lo + width]

        cosa, sina = cosa_ref[rows, :], sina_ref[rows, :]
        cos32, sin32 = cos32_ref[rows, :], sin32_ref[rows, :]

        qa_ref[rows, :] = _rope(proj(0, 384), cosa, sina, 32) * scale_a
        ka_ref[rows, :] = _rope(proj(384, 384), cosa, sina, 32)
        va_ref[rows, :] = proj(768, 384)

        cq = _rms(proj(1152, 384), qg_ref[...], NORM_EPS).astype(BF16)
        qf = jnp.dot(cq, wuq_ref[...], preferred_element_type=F32)
        qn_ref[:, rows] = (qf[:, :384] * scale_b).T.astype(BF16)
        qr_ref[:, rows] = (_rope(qf[:, 384:], cos32, sin32, 16) * scale_b).T.astype(BF16)

        ckv = _rms(proj(1536, 128), kvg_ref[...], NORM_EPS).astype(BF16)
        kvf = jnp.dot(ckv, wukv_ref[...], preferred_element_type=F32)
        kn_ref[rows, :] = kvf[:, :384].astype(BF16)
        vb_ref[:, rows] = kvf[:, 384:].T.astype(BF16)
        kr_ref[rows, :] = _rope(proj(1664, 128), cos32, sin32, 16).astype(BF16)

        qc_ref[:, rows] = (_rope(proj(1792, 256), cos32, sin32, 16) * scale_c).T.astype(BF16)
        kc_ref[rows, :] = _rope(proj(2048, 256), cos32, sin32, 16).astype(BF16)
        vc_ref[:, rows] = proj(2304, 256).T.astype(BF16)


def _inproj(x2, g, w1, tabs, qg, wuq, kvg, wukv, *, seq, tm):
    n = x2.shape[0]
    nt = n // tm
    st = seq // tm
    row = lambda i: (i, 0)
    const = lambda i: (0, 0)
    tab = lambda i: (i % st, 0)
    widths = (384, 384, 384, 384, 384, 384, 384, 128, 256, 256, 256)
    major = (False, False, False, True, True, False, True, False, True, False, True)
    bsz = n // seq
    col = lambda i: (i // st, 0, i % st)
    out_specs = [pl.BlockSpec((None, w, tm), col) if t else pl.BlockSpec((tm, w), row)
                 for w, t in zip(widths, major)]
    dtypes = (F32,) * 3 + (BF16,) * 8
    out_shape = [jax.ShapeDtypeStruct((bsz, w, seq) if t else (n, w), dt)
                 for w, t, dt in zip(widths, major, dtypes)]
    return pl.pallas_call(
        _inproj_kernel,
        grid=(nt,),
        in_specs=[
            pl.BlockSpec((tm, x2.shape[1]), row),
            pl.BlockSpec(g.shape, const),
            pl.BlockSpec(w1.shape, const),
            pl.BlockSpec((tm, LANES), tab), pl.BlockSpec((tm, LANES), tab),
            pl.BlockSpec((tm, LANES), tab), pl.BlockSpec((tm, LANES), tab),
            pl.BlockSpec(qg.shape, const), pl.BlockSpec(wuq.shape, const),
            pl.BlockSpec(kvg.shape, const), pl.BlockSpec(wukv.shape, const),
        ],
        out_specs=out_specs,
        out_shape=out_shape,
        compiler_params=pltpu.CompilerParams(
            dimension_semantics=("arbitrary",), vmem_limit_bytes=VMEM_LIMIT),
        name="inproj",
    )(x2, g, w1, *tabs, qg, wuq, kvg, wukv)


def _dilated_kernel(q_ref, k_ref, kp_ref, v_ref, vp_ref, o_ref,
                    q4_ref, k4_ref, kp4_ref, v4_ref, vp4_ref, o4_ref, l4_ref, ot_ref, lt_ref,
                    *, tile, nt):
    quarter = tile // 4
    t = pl.program_id(0) % nt
    lane = lax.broadcasted_iota(jnp.int32, (A_SPAN, LANES), 1)
    low = lane < A_DIM
    qi = lax.broadcasted_iota(jnp.int32, (2 * A_SPAN, 2 * A_SPAN), 0) & (A_SPAN - 1)
    ki = lax.broadcasted_iota(jnp.int32, (2 * A_SPAN, 2 * A_SPAN), 1)
    dist = qi + A_SPAN - ki
    band = (dist >= 0) & (dist <= A_SPAN)
    band0 = band & (ki >= jnp.where(t > 0, 0, A_SPAN))
    ones = jnp.ones((2 * A_SPAN, LANES), BF16)
    zero = jnp.zeros((), BF16)

    for src, dst in ((q_ref, q4_ref), (k_ref, k4_ref), (kp_ref, kp4_ref),
                     (v_ref, v4_ref), (vp_ref, vp4_ref)):
        for c in range(4):
            dst[c * quarter:(c + 1) * quarter, :] = src[pl.ds(c, quarter, stride=4), :]

    def attend(qg, k_prev, k_cur, v_prev, v_cur, valid):
        qg = qg.astype(BF16)
        qs = jnp.concatenate([jnp.where(low, qg, zero), jnp.where(low, zero, qg)], axis=0)
        kk = jnp.concatenate([k_prev, k_cur], axis=0).astype(BF16)
        vv = jnp.concatenate([v_prev, v_cur], axis=0).astype(BF16)
        s = jnp.where(valid, _nt_dot(qs, kk), NEG)
        m = jnp.max(s, axis=1, keepdims=True)
        p = jnp.exp2(s - m).astype(BF16)
        pv = jnp.dot(p, jnp.concatenate([vv, ones], axis=1), preferred_element_type=F32)
        num = jnp.where(low, pv[:A_SPAN, :LANES], pv[A_SPAN:, :LANES])
        den = jnp.where(low, pv[:A_SPAN, LANES:], pv[A_SPAN:, LANES:])
        mm = jnp.where(low, m[:A_SPAN], m[A_SPAN:])
        return num / den, mm + jnp.log2(den)

    def merge(o, lse, o_prev, l_prev):
        mx = jnp.maximum(lse, l_prev)
        wa = jnp.exp2(lse - mx)
        wb = jnp.exp2(l_prev - mx)
        tot = wa + wb
        return (wa * o + wb * o_prev) / tot, mx + jnp.log2(tot)

    for c in range(4):
        for g in range(quarter // A_SPAN):
            cur = pl.ds(c * quarter + g * A_SPAN, A_SPAN)
            if g == 0:
                prev = pl.ds((c + 1) * quarter - A_SPAN, A_SPAN)
                k_prev, v_prev, valid = kp4_ref[prev, :], vp4_ref[prev, :], band0
            else:
                prev = pl.ds(c * quarter + (g - 1) * A_SPAN, A_SPAN)
                k_prev, v_prev, valid = k4_ref[prev, :], v4_ref[prev, :], band
            o, lse = attend(q4_ref[cur, :], k_prev, k4_ref[cur, :], v_prev, v4_ref[cur, :], valid)
            o4_ref[cur, :] = o
            l4_ref[cur, :] = lse

    assert tile == A_SPAN * 16
    for c in range(4):
        for a in range(4):
            cur = pl.ds(c * quarter + a, A_SPAN, stride=4)
            o, lse = attend(q4_ref[cur, :], kp4_ref[cur, :], k4_ref[cur, :],
                            vp4_ref[cur, :], v4_ref[cur, :], band0)
            o, lse = merge(o, lse, o4_ref[cur, :], l4_ref[cur, :])
            o4_ref[cur, :] = o
            l4_ref[cur, :] = lse

    for c in range(4):
        ot_ref[pl.ds(c, quarter, stride=4), :] = o4_ref[c * quarter:(c + 1) * quarter, :]
        lt_ref[pl.ds(c, quarter, stride=4), :] = l4_ref[c * quarter:(c + 1) * quarter, :]

    for g in range(tile // A_SPAN):
        cur = pl.ds(g * A_SPAN, A_SPAN)
        if g == 0:
            prev = pl.ds(tile - A_SPAN, A_SPAN)
            k_prev, v_prev, valid = kp_ref[prev, :], vp_ref[prev, :], band0
        else:
            prev = pl.ds((g - 1) * A_SPAN, A_SPAN)
            k_prev, v_prev, valid = k_ref[prev, :], v_ref[prev, :], band
        o, lse = attend(q_ref[cur, :], k_prev, k_ref[cur, :], v_prev, v_ref[cur, :], valid)
        o, _ = merge(o, lse, ot_ref[cur, :], lt_ref[cur, :])
        o_ref[cur, :] = o.astype(o_ref.dtype)


def _dilated_mixer(q, k, v, *, seq, tile):
    n, width = q.shape
    nt = seq // tile
    for window, dilation in A_CONFIGS:
        assert window // dilation == A_SPAN and tile % window == 0
    cur = lambda i, p: (i, p)
    prev = lambda i, p: (jnp.where(i % nt == 0, i, i - 1), p)
    blk = pl.BlockSpec((tile, LANES), cur)
    pblk = pl.BlockSpec((tile, LANES), prev)
    return pl.pallas_call(
        functools.partial(_dilated_kernel, tile=tile, nt=nt),
        grid=(n // tile, width // LANES),
        in_specs=[blk, blk, pblk, blk, pblk],
        out_specs=blk,
        out_shape=jax.ShapeDtypeStruct((n, width), BF16),
        scratch_shapes=[pltpu.VMEM((tile, LANES), F32)] * 9,
        compiler_params=pltpu.CompilerParams(
            dimension_semantics=("arbitrary",) * 2, vmem_limit_bytes=VMEM_LIMIT),
        name="dilated",
    )(q, k, k, v, v)


def _flash_scratch(rows, tq):
    tk = tq // 2
    score_bufs = [pltpu.VMEM((tk, rows), F32), pltpu.VMEM((1, rows), F32),
                  pltpu.VMEM((tk, rows), BF16)]
    return [pltpu.VMEM((1, rows), F32), pltpu.VMEM((ACC_ROWS, rows), F32)] + 2 * score_bufs


def _flash_cols(qst, k_block, vt_block, scratch, *, qi, tq):
    m_ref, acc_ref, st_a, cm_a, pt_a, st_b, cm_b, pt_b = scratch
    tk = tq // 2
    segs = range(qst.shape[1] // tq)
    ones = jnp.ones((ACC_ROWS - HEAD_V, tk), BF16)
    tri = (lax.broadcasted_iota(jnp.int32, (tk, tk), 0)
           <= lax.broadcasted_iota(jnp.int32, (tk, tk), 1))

    def scores(blk, st_ref, cm_ref):
        st = jnp.dot(k_block(blk), qst, preferred_element_type=F32)
        st_ref[...] = st
        cm_ref[...] = jnp.max(st, axis=0, keepdims=True)

    def softmax(st_ref, cm_ref, pt_ref):
        m_prev = m_ref[...]
        m_new = jnp.maximum(m_prev, cm_ref[...])
        m_ref[...] = m_new
        pt_ref[...] = jnp.exp2(st_ref[...] - m_new).astype(BF16)
        return jnp.exp2(m_prev - m_new)

    def values(blk, pt_ref):
        vt = vt_block(blk)
        half = qst.shape[1] // 2
        outs = []
        for h in range(2):
            lhs = jnp.concatenate([vt[h * HEAD_V:(h + 1) * HEAD_V], ones], axis=0)
            outs.append(jnp.dot(lhs, pt_ref[:, h * half:(h + 1) * half],
                                preferred_element_type=F32))
        return jnp.concatenate(outs, axis=1)

    diag0 = 2 * qi
    q_late = jnp.concatenate([qst[:, s * tq + tk:(s + 1) * tq] for s in segs], axis=1)
    s_late = jnp.dot(k_block(diag0 + 1), q_late, preferred_element_type=F32)
    for s in segs:
        early = slice(s * tq, s * tq + tk)
        late = slice(s * tq + tk, (s + 1) * tq)
        st = jnp.where(tri, s_late[:, s * tk:(s + 1) * tk], NEG)
        cm = jnp.max(st, axis=0, keepdims=True)
        m_ref[:, early] = jnp.full((1, tk), NEG, F32)
        m_ref[:, late] = cm
        pt_a[:, early] = jnp.zeros((tk, tk), BF16)
        pt_a[:, late] = jnp.exp2(st - cm).astype(BF16)
    s_full = jnp.dot(k_block(diag0), qst, preferred_element_type=F32)
    st = jnp.concatenate(
        [x for s in segs for x in (jnp.where(tri, s_full[:, s * tq:s * tq + tk], NEG),
                                   s_full[:, s * tq + tk:(s + 1) * tq])], axis=1)
    st_b[...] = st
    cm_b[...] = jnp.max(st, axis=0, keepdims=True)
    acc_ref[...] = jnp.zeros(acc_ref.shape, F32)

    def body(j, carry):
        first = j == 0
        alpha = softmax(st_b, cm_b, pt_b)
        scores(2 * j, st_a, cm_a)
        acc_ref[...] = alpha * (acc_ref[...] + values(jnp.where(first, diag0 + 1, 2 * j - 2), pt_a))
        alpha = softmax(st_a, cm_a, pt_a)
        scores(2 * j + 1, st_b, cm_b)
        acc_ref[...] = alpha * (acc_ref[...] + values(jnp.where(first, diag0, 2 * j - 1), pt_b))
        return carry

    lax.fori_loop(0, qi, body, 0)

    first = qi == 0
    alpha = softmax(st_b, cm_b, pt_b)
    acc_ref[...] = alpha * (acc_ref[...] + values(jnp.where(first, 1, 2 * qi - 2), pt_a))
    acc_ref[...] += values(jnp.where(first, 0, 2 * qi - 1), pt_b)
    return acc_ref[:HEAD_V, :] / acc_ref[HEAD_V:HEAD_V + 1, :]


def _mla_kernel(qn_ref, qr_ref, kn_ref, kr_ref, vt_ref, o_ref, *scratch, tq):
    tk = tq // 2
    row = lax.broadcasted_iota(jnp.int32, (LANES, tq), 0)
    zero = jnp.zeros((), BF16)

    def k_block(i):
        rows = pl.ds(pl.multiple_of(i * tk, tk), tk)
        return jnp.concatenate([kn_ref[rows, :], kr_ref[rows, :]], axis=1)

    def vt_block(i):
        return vt_ref[:, pl.ds(pl.multiple_of(i * tk, tk), tk)]

    def query_block(qi, carry):
        span = pl.ds(pl.multiple_of(qi * tq, tq), tq)
        qn, qr = qn_ref[:, span], qr_ref[:, span]
        q0 = jnp.concatenate([jnp.where(row < B_NOPE, qn, zero),
                              jnp.where(row < B_ROPE, qr, zero)], axis=0)
        q1 = jnp.concatenate([jnp.where(row >= B_NOPE, qn, zero),
                              jnp.where((row >= B_ROPE) & (row < 2 * B_ROPE), qr, zero)], axis=0)
        qst = jnp.concatenate([q0, q1], axis=1)
        ot = _flash_cols(qst, k_block, vt_block, scratch, qi=qi, tq=tq)
        pair_t = jnp.concatenate([ot[:, :tq], ot[:, tq:]], axis=0)
        o_ref[span, :] = pair_t.T.astype(o_ref.dtype)
        return carry

    lax.fori_loop(0, o_ref.shape[0] // tq, query_block, 0)


def _mla_attn(qnt, qrt, kn, kr, vt, *, tq):
    bsz, seq, width = kn.shape
    pairs = width // LANES
    tblk = pl.BlockSpec((None, LANES, seq), lambda b, p: (b, p, 0))
    kblk = pl.BlockSpec((None, seq, LANES), lambda b, p: (b, 0, p))
    krblk = pl.BlockSpec((None, seq, LANES), lambda b, p: (b, 0, 0))
    return pl.pallas_call(
        functools.partial(_mla_kernel, tq=tq),
        grid=(bsz, pairs),
        in_specs=[tblk, tblk, kblk, krblk, tblk],
        out_specs=kblk,
        out_shape=jax.ShapeDtypeStruct((bsz, seq, width), BF16),
        scratch_shapes=_flash_scratch(2 * tq, tq),
        compiler_params=pltpu.CompilerParams(
            dimension_semantics=("arbitrary",) * 2, vmem_limit_bytes=VMEM_LIMIT),
        name="mla_attn",
    )(qnt, qrt, kn, kr, vt)


def _diff_kernel(lam_ref, g_ref, q_ref, k_ref, vt_ref, o_ref, *scratch, tq, lam_init):
    tk = tq // 2
    lane = lax.broadcasted_iota(jnp.int32, (tq, LANES), 1)
    row = lax.broadcasted_iota(jnp.int32, (LANES, tq), 0)
    low = lane < C_V
    zero = jnp.zeros((), BF16)
    lp = lam_ref[...]
    lam = (jnp.exp(jnp.sum(lp[0:1] * lp[1:2], axis=1, keepdims=True))
           - jnp.exp(jnp.sum(lp[2:3] * lp[3:4], axis=1, keepdims=True)) + lam_init)

    def k_block(i):
        return k_ref[pl.ds(pl.multiple_of(i * tk, tk), tk), :]

    def vt_block(i):
        return vt_ref[:, pl.ds(pl.multiple_of(i * tk, tk), tk)]

    def query_block(qi, carry):
        span = pl.ds(pl.multiple_of(qi * tq, tq), tq)
        q = q_ref[:, span]
        qst = jnp.concatenate(
            [jnp.where((row >= j * C_QK) & (row < (j + 1) * C_QK), q, zero) for j in range(4)],
            axis=1)
        ot = _flash_cols(qst, k_block, vt_block, scratch, qi=qi, tq=tq)
        d0 = ot[:, 0:tq] - lam * ot[:, tq:2 * tq]
        d1 = ot[:, 2 * tq:3 * tq] - lam * ot[:, 3 * tq:]
        d = jnp.concatenate([d0, d1], axis=0).T
        sq = d * d
        ms_lo = jnp.sum(jnp.where(low, sq, 0.0), axis=1, keepdims=True) * (1.0 / C_V)
        ms_hi = jnp.sum(jnp.where(low, 0.0, sq), axis=1, keepdims=True) * (1.0 / C_V)
        rs = jnp.where(low, lax.rsqrt(ms_lo + SUBLN_EPS), lax.rsqrt(ms_hi + SUBLN_EPS))
        o_ref[span, :] = ((d * rs * g_ref[...]) * (1.0 - lam_init)).astype(o_ref.dtype)
        return carry

    lax.fori_loop(0, o_ref.shape[0] // tq, query_block, 0)


def _diff_attn(lam_params, g2, qt, k, vt, *, tq, lam_init):
    bsz, seq, width = k.shape
    pairs = width // LANES
    tblk = pl.BlockSpec((None, LANES, seq), lambda b, p: (b, p, 0))
    kblk = pl.BlockSpec((None, seq, LANES), lambda b, p: (b, 0, p))
    const = lambda b, p: (0, 0)
    return pl.pallas_call(
        functools.partial(_diff_kernel, tq=tq, lam_init=lam_init),
        grid=(bsz, pairs),
        in_specs=[pl.BlockSpec(lam_params.shape, const), pl.BlockSpec(g2.shape, const),
                  tblk, kblk, tblk],
        out_specs=kblk,
        out_shape=jax.ShapeDtypeStruct((bsz, seq, width), BF16),
        scratch_shapes=_flash_scratch(4 * tq, tq),
        compiler_params=pltpu.CompilerParams(
            dimension_semantics=("arbitrary",) * 2, vmem_limit_bytes=VMEM_LIMIT),
        name="diff_attn",
    )(lam_params, g2, qt, k, vt)


def _post_kernel(x_ref, oa_ref, ob_ref, oc_ref, wo_ref, g2_ref, wup_ref, wdn_ref, gf_ref,
                 y_ref, *, final):
    mixed = jnp.concatenate([oa_ref[...], ob_ref[...], oc_ref[...]], axis=1)
    x1 = x_ref[...] + jnp.dot(mixed, wo_ref[...], preferred_element_type=F32)
    hn = _rms(x1, g2_ref[...], NORM_EPS).astype(BF16)
    y_ref[...] = x1
    for c in range(wup_ref.shape[1] // D_FF_CHUNK):
        cs = slice(c * D_FF_CHUNK, (c + 1) * D_FF_CHUNK)
        u = jnp.dot(hn, wup_ref[:, cs], preferred_element_type=F32)
        a = jnp.square(jnp.maximum(u, 0.0)).astype(BF16)
        y_ref[...] += jnp.dot(a, wdn_ref[cs, :], preferred_element_type=F32)
    if final:
        y_ref[...] = _rms(y_ref[...], gf_ref[...], NORM_EPS)


def _post(x2, oa, ob, oc, wo, g2, wup, wdn, gf, *, tm, final):
    n, d = x2.shape
    row = lambda i: (i, 0)
    const = lambda i: (0, 0)
    return pl.pallas_call(
        functools.partial(_post_kernel, final=final),
        grid=(n // tm,),
        in_specs=[pl.BlockSpec((tm, d), row),
                  pl.BlockSpec((tm, oa.shape[1]), row),
                  pl.BlockSpec((tm, ob.shape[1]), row),
                  pl.BlockSpec((tm, oc.shape[1]), row),
                  pl.BlockSpec(wo.shape, const), pl.BlockSpec(g2.shape, const),
                  pl.BlockSpec(wup.shape, const), pl.BlockSpec(wdn.shape, const),
                  pl.BlockSpec(gf.shape, const)],
        out_specs=pl.BlockSpec((tm, d), row),
        out_shape=jax.ShapeDtypeStruct((n, d), F32),
        compiler_params=pltpu.CompilerParams(
            dimension_semantics=("arbitrary",), vmem_limit_bytes=VMEM_LIMIT),
        name="post_final" if final else "post",
    )(x2, oa, ob, oc, wo, g2, wup, wdn, gf)


def _rope_tables(seq):
    def tables(dim):
        half = dim // 2
        inv = 1.0 / (ROPE_THETA ** (jnp.arange(half, dtype=F32) / half))
        ang = jnp.arange(seq, dtype=F32)[:, None] * inv[None, :]
        cos, sin = jnp.cos(ang), jnp.sin(ang)
        reps = LANES // dim
        cos_t = jnp.tile(jnp.concatenate([cos, cos], axis=1), (1, reps))
        sin_t = jnp.tile(jnp.concatenate([-sin, sin], axis=1), (1, reps))
        return cos_t, sin_t
    cosa, sina = tables(A_DIM)
    cos32, sin32 = tables(C_QK)
    return cosa, sina, cos32, sin32


def _prep_layer_weights(w_in, w_uq, w_ukv):
    cols = jnp.split(w_in, [384, 768, 1152, 1536, 1664, 1696, 1952, 2208], axis=1)
    qa, ka, va, cq, ckv, kr, qc, kc, vc = cols
    w1 = jnp.concatenate([qa, ka, va, cq, ckv, jnp.tile(kr, (1, LANES // B_ROPE)), qc, kc, vc],
                         axis=1).astype(BF16)
    uq = w_uq.reshape(B_Q_LORA, B_HEADS, B_NOPE + B_ROPE)
    nope = uq[:, :, :B_NOPE].reshape(B_Q_LORA, B_HEADS * B_NOPE)
    rope = uq[:, :, B_NOPE:].reshape(B_Q_LORA, B_HEADS // 2, 2 * B_ROPE)
    rope = jnp.pad(rope, ((0, 0), (0, 0), (0, LANES - 2 * B_ROPE))).reshape(B_Q_LORA, -1)
    wuq = jnp.concatenate([nope, rope], axis=1).astype(BF16)
    ukv = w_ukv.reshape(B_KV_LORA, B_HEADS, B_NOPE + B_V)
    wukv = jnp.concatenate([ukv[:, :, :B_NOPE].reshape(B_KV_LORA, -1),
                            ukv[:, :, B_NOPE:].reshape(B_KV_LORA, -1)], axis=1).astype(BF16)
    return w1, wuq, wukv


def kernel(x, ln1_g, w_in, mla_q_norm_g, mla_w_uq, mla_kv_norm_g, mla_w_ukv, diff_lambda,
           diff_subln_g, w_o, ln2_g, w_up, w_down, final_g):
    bsz, seq, d_model = x.shape
    depth = w_in.shape[0]
    tm = 512
    tq = 512
    tabs = _rope_tables(seq)
    x2 = x.reshape(bsz * seq, d_model)
    gf = final_g.reshape(1, d_model)
    sh = lambda t, w: t.reshape(bsz, seq, w)

    for layer in range(depth):
        lam_init = 0.8 - 0.6 * math.exp(-0.3 * layer)
        w1, wuq, wukv = _prep_layer_weights(w_in[layer], mla_w_uq[layer], mla_w_ukv[layer])
        qa, ka, va, qn, qr, kn, vb, kr, qc, kc, vc = _inproj(
            x2, ln1_g[layer].reshape(1, -1), w1, tabs,
            mla_q_norm_g[layer].reshape(1, -1), wuq,
            mla_kv_norm_g[layer].reshape(1, -1), wukv, seq=seq, tm=2 * tm)

        out_a = _dilated_mixer(qa, ka, va, seq=seq, tile=2048)
        out_b = _mla_attn(qn, qr, sh(kn, 384), sh(kr, 128), vb, tq=min(2 * tq, seq))
        g_sub = jnp.tile(diff_subln_g[layer], LANES // C_V).reshape(1, LANES)
        out_c = _diff_attn(diff_lambda[layer], g_sub, qc, sh(kc, 256), vc,
                           tq=min(2 * tq, seq), lam_init=lam_init)

        x2 = _post(x2, out_a, out_b.reshape(-1, B_WIDTH),
                   out_c.reshape(-1, C_WIDTH), w_o[layer].astype(BF16),
                   ln2_g[layer].reshape(1, -1), w_up[layer].astype(BF16),
                   w_down[layer].astype(BF16), gf, tm=tm, final=layer == depth - 1)

    return x2.reshape(bsz, seq, d_model)
```

```python
import functools
import math

import jax
import jax.numpy as jnp
from jax import lax
from jax.experimental import pallas as pl
from jax.experimental.pallas import tpu as pltpu

F32 = jnp.float32
BF16 = jnp.bfloat16

LANES = 128
HEAD_V = 64
ACC_ROWS = HEAD_V + 16
VMEM_LIMIT = 56 * 1024 * 1024

ROPE_THETA = 10000.0
NORM_EPS = 1e-6
SUBLN_EPS = 1e-5
NEG = -1e30
LOG2E = math.log2(math.e)

A_HEADS, A_DIM = 6, 64
A_CONFIGS = ((128, 1), (512, 4), (2048, 16))
A_SPAN = 128
B_HEADS, B_NOPE, B_ROPE, B_V = 6, 64, 32, 64
B_Q_LORA, B_KV_LORA = 384, 128
C_HEADS, C_QK = 4, 32
C_V = 2 * C_QK
A_WIDTH = A_HEADS * A_DIM
B_WIDTH = B_HEADS * B_V
C_WIDTH = C_HEADS * C_V
D_FF_CHUNK = 512
INPROJ_SUB = 256


def _rms(x, g, eps):
    return x * lax.rsqrt(jnp.mean(x * x, axis=-1, keepdims=True) + eps) * g


def _rope(x, cos, sin_signed, half):
    lane = lax.broadcasted_iota(jnp.int32, (x.shape[0], LANES), 1)
    first = (lane & (2 * half - 1)) < half
    outs = []
    for c in range(x.shape[1] // LANES):
        xc = x[:, c * LANES:(c + 1) * LANES]
        partner = jnp.where(first, pltpu.roll(xc, LANES - half, 1), pltpu.roll(xc, half, 1))
        outs.append(xc * cos + partner * sin_signed)
    return outs[0] if len(outs) == 1 else jnp.concatenate(outs, axis=1)


def _nt_dot(a, b):
    return lax.dot_general(a, b, (((1,), (1,)), ((), ())), preferred_element_type=F32)


def _inproj_kernel(x_ref, g_ref, w1_ref, cosa_ref, sina_ref, cos32_ref, sin32_ref,
                   qg_ref, wuq_ref, kvg_ref, wukv_ref,
                   qa_ref, ka_ref, va_ref, qn_ref, qr_ref, kn_ref, vb_ref, kr_ref,
                   qc_ref, kc_ref, vc_ref):
    scale_a = A_DIM ** -0.5 * LOG2E
    scale_b = (B_NOPE + B_ROPE) ** -0.5 * LOG2E
    scale_c = C_QK ** -0.5 * LOG2E

    for r0 in range(0, x_ref.shape[0], INPROJ_SUB):
        rows = slice(r0, r0 + INPROJ_SUB)
        h = _rms(x_ref[rows, :], g_ref[...], NORM_EPS).astype(BF16)
        p_all = jnp.dot(h, w1_ref[...], preferred_element_type=F32)

        def proj(lo, width):
            return p_all[:, lo:lo + width]

        cosa, sina = cosa_ref[rows, :], sina_ref[rows, :]
        cos32, sin32 = cos32_ref[rows, :], sin32_ref[rows, :]

        qa_ref[rows, :] = _rope(proj(0, 384), cosa, sina, 32) * scale_a
        ka_ref[rows, :] = _rope(proj(384, 384), cosa, sina, 32)
        va_ref[rows, :] = proj(768, 384)

        cq = _rms(proj(1152, 384), qg_ref[...], NORM_EPS).astype(BF16)
        qf = jnp.dot(cq, wuq_ref[...], preferred_element_type=F32)
        qn_ref[:, rows] = (qf[:, :384] * scale_b).T.astype(BF16)
        qr_ref[:, rows] = (_rope(qf[:, 384:], cos32, sin32, 16) * scale_b).T.astype(BF16)

        ckv = _rms(proj(1536, 128), kvg_ref[...], NORM_EPS).astype(BF16)
        kvf = jnp.dot(ckv, wukv_ref[...], preferred_element_type=F32)
        kn_ref[rows, :] = kvf[:, :384].astype(BF16)
        vb_ref[:, rows] = kvf[:, 384:].T.astype(BF16)
        kr_ref[rows, :] = _rope(proj(1664, 128), cos32, sin32, 16).astype(BF16)

        qc_ref[:, rows] = (_rope(proj(1792, 256), cos32, sin32, 16) * scale_c).T.astype(BF16)
        kc_ref[rows, :] = _rope(proj(2048, 256), cos32, sin32, 16).astype(BF16)
        vc_ref[:, rows] = proj(2304, 256).T.astype(BF16)


def _inproj(x2, g, w1, tabs, qg, wuq, kvg, wukv, *, seq, tm):
    n = x2.shape[0]
    nt = n // tm
    st = seq // tm
    row = lambda i: (i, 0)
    const = lambda i: (0, 0)
    tab = lambda i: (i % st, 0)
    widths = (384, 384, 384, 384, 384, 384, 384, 128, 256, 256, 256)
    major = (False, False, False, True, True, False, True, False, True, False, True)
    bsz = n // seq
    col = lambda i: (i // st, 0, i % st)
    out_specs = [pl.BlockSpec((None, w, tm), col) if t else pl.BlockSpec((tm, w), row)
                 for w, t in zip(widths, major)]
    dtypes = (F32,) * 3 + (BF16,) * 8
    out_shape = [jax.ShapeDtypeStruct((bsz, w, seq) if t else (n, w), dt)
                 for w, t, dt in zip(widths, major, dtypes)]
    return pl.pallas_call(
        _inproj_kernel,
        grid=(nt,),
        in_specs=[
            pl.BlockSpec((tm, x2.shape[1]), row),
            pl.BlockSpec(g.shape, const),
            pl.BlockSpec(w1.shape, const),
            pl.BlockSpec((tm, LANES), tab), pl.BlockSpec((tm, LANES), tab),
            pl.BlockSpec((tm, LANES), tab), pl.BlockSpec((tm, LANES), tab),
            pl.BlockSpec(qg.shape, const), pl.BlockSpec(wuq.shape, const),
            pl.BlockSpec(kvg.shape, const), pl.BlockSpec(wukv.shape, const),
        ],
        out_specs=out_specs,
        out_shape=out_shape,
        compiler_params=pltpu.CompilerParams(
            dimension_semantics=("arbitrary",), vmem_limit_bytes=VMEM_LIMIT),
        name="inproj",
    )(x2, g, w1, *tabs, qg, wuq, kvg, wukv)


def _dilated_kernel(q_ref, k_ref, kp_ref, v_ref, vp_ref, o_ref,
                    q4_ref, k4_ref, kp4_ref, v4_ref, vp4_ref, o4_ref, l4_ref, ot_ref, lt_ref,
                    *, tile, nt):
    quarter = tile // 4
    t = pl.program_id(0) % nt
    lane = lax.broadcasted_iota(jnp.int32, (A_SPAN, LANES), 1)
    low = lane < A_DIM
    qi = lax.broadcasted_iota(jnp.int32, (2 * A_SPAN, 2 * A_SPAN), 0) & (A_SPAN - 1)
    ki = lax.broadcasted_iota(jnp.int32, (2 * A_SPAN, 2 * A_SPAN), 1)
    dist = qi + A_SPAN - ki
    band = (dist >= 0) & (dist <= A_SPAN)
    band0 = band & (ki >= jnp.where(t > 0, 0, A_SPAN))
    ones = jnp.ones((2 * A_SPAN, LANES), BF16)
    zero = jnp.zeros((), BF16)

    for src, dst in ((q_ref, q4_ref), (k_ref, k4_ref), (kp_ref, kp4_ref),
                     (v_ref, v4_ref), (vp_ref, vp4_ref)):
        for c in range(4):
            dst[c * quarter:(c + 1) * quarter, :] = src[pl.ds(c, quarter, stride=4), :]

    def attend(qg, k_prev, k_cur, v_prev, v_cur, valid):
        qg = qg.astype(BF16)
        qs = jnp.concatenate([jnp.where(low, qg, zero), jnp.where(low, zero, qg)], axis=0)
        kk = jnp.concatenate([k_prev, k_cur], axis=0).astype(BF16)
        vv = jnp.concatenate([v_prev, v_cur], axis=0).astype(BF16)
        s = jnp.where(valid, _nt_dot(qs, kk), NEG)
        m = jnp.max(s, axis=1, keepdims=True)
        p = jnp.exp2(s - m).astype(BF16)
        pv = jnp.dot(p, jnp.concatenate([vv, ones], axis=1), preferred_element_type=F32)
        num = jnp.where(low, pv[:A_SPAN, :LANES], pv[A_SPAN:, :LANES])
        den = jnp.where(low, pv[:A_SPAN, LANES:], pv[A_SPAN:, LANES:])
        mm = jnp.where(low, m[:A_SPAN], m[A_SPAN:])
        return num / den, mm + jnp.log2(den)

    def merge(o, lse, o_prev, l_prev):
        mx = jnp.maximum(lse, l_prev)
        wa = jnp.exp2(lse - mx)
        wb = jnp.exp2(l_prev - mx)
        tot = wa + wb
        return (wa * o + wb * o_prev) / tot, mx + jnp.log2(tot)

    for c in range(4):
        for g in range(quarter // A_SPAN):
            cur = pl.ds(c * quarter + g * A_SPAN, A_SPAN)
            if g == 0:
                prev = pl.ds((c + 1) * quarter - A_SPAN, A_SPAN)
                k_prev, v_prev, valid = kp4_ref[prev, :], vp4_ref[prev, :], band0
            else:
                prev = pl.ds(c * quarter + (g - 1) * A_SPAN, A_SPAN)
                k_prev, v_prev, valid = k4_ref[prev, :], v4_ref[prev, :], band
            o, lse = attend(q4_ref[cur, :], k_prev, k4_ref[cur, :], v_prev, v4_ref[cur, :], valid)
            o4_ref[cur, :] = o
            l4_ref[cur, :] = lse

    assert tile == A_SPAN * 16
    for c in range(4):
        for a in range(4):
            cur = pl.ds(c * quarter + a, A_SPAN, stride=4)
            o, lse = attend(q4_ref[cur, :], kp4_ref[cur, :], k4_ref[cur, :],
                            vp4_ref[cur, :], v4_ref[cur, :], band0)
            o, lse = merge(o, lse, o4_ref[cur, :], l4_ref[cur, :])
            o4_ref[cur, :] = o
            l4_ref[cur, :] = lse

    for c in range(4):
        ot_ref[pl.ds(c, quarter, stride=4), :] = o4_ref[c * quarter:(c + 1) * quarter, :]
        lt_ref[pl.ds(c, quarter, stride=4), :] = l4_ref[c * quarter:(c + 1) * quarter, :]

    for g in range(tile // A_SPAN):
        cur = pl.ds(g * A_SPAN, A_SPAN)
        if g == 0:
            prev = pl.ds(tile - A_SPAN, A_SPAN)
            k_prev, v_prev, valid = kp_ref[prev, :], vp_ref[prev, :], band0
        else:
            prev = pl.ds((g - 1) * A_SPAN, A_SPAN)
            k_prev, v_prev, valid = k_ref[prev, :], v_ref[prev, :], band
        o, lse = attend(q_ref[cur, :], k_prev, k_ref[cur, :], v_prev, v_ref[cur, :], valid)
        o, _ = merge(o, lse, ot_ref[cur, :], lt_ref[cur, :])
        o_ref[cur, :] = o.astype(o_ref.dtype)


def _dilated_mixer(q, k, v, *, seq, tile):
    n, width = q.shape
    nt = seq // tile
    for window, dilation in A_CONFIGS:
        assert window // dilation == A_SPAN and tile % window == 0
    cur = lambda i, p: (i, p)
    prev = lambda i, p: (jnp.where(i % nt == 0, i, i - 1), p)
    blk = pl.BlockSpec((tile, LANES), cur)
    pblk = pl.BlockSpec((tile, LANES), prev)
    return pl.pallas_call(
        functools.partial(_dilated_kernel, tile=tile, nt=nt),
        grid=(n // tile, width // LANES),
        in_specs=[blk, blk, pblk, blk, pblk],
        out_specs=blk,
        out_shape=jax.ShapeDtypeStruct((n, width), BF16),
        scratch_shapes=[pltpu.VMEM((tile, LANES), F32)] * 9,
        compiler_params=pltpu.CompilerParams(
            dimension_semantics=("arbitrary",) * 2, vmem_limit_bytes=VMEM_LIMIT),
        name="dilated",
    )(q, k, k, v, v)


def _flash_scratch(rows, tq):
    tk = tq // 2
    score_bufs = [pltpu.VMEM((tk, rows), F32), pltpu.VMEM((1, rows), F32),
                  pltpu.VMEM((tk, rows), BF16)]
    return [pltpu.VMEM((1, rows), F32), pltpu.VMEM((ACC_ROWS, rows), F32)] + 2 * score_bufs


def _flash_cols(qst, k_block, vt_block, scratch, *, qi, tq):
    m_ref, acc_ref, st_a, cm_a, pt_a, st_b, cm_b, pt_b = scratch
    tk = tq // 2
    segs = range(qst.shape[1] // tq)
    ones = jnp.ones((ACC_ROWS - HEAD_V, tk), BF16)
    tri = (lax.broadcasted_iota(jnp.int32, (tk, tk), 0)
           <= lax.broadcasted_iota(jnp.int32, (tk, tk), 1))

    def scores(blk, st_ref, cm_ref):
        st = jnp.dot(k_block(blk), qst, preferred_element_type=F32)
        st_ref[...] = st
        cm_ref[...] = jnp.max(st, axis=0, keepdims=True)

    def softmax(st_ref, cm_ref, pt_ref):
        m_prev = m_ref[...]
        m_new = jnp.maximum(m_prev, cm_ref[...])
        m_ref[...] = m_new
        pt_ref[...] = jnp.exp2((st_ref[...] - m_new).astype(BF16))
        return jnp.exp2(m_prev - m_new)

    def values(blk, pt_ref):
        vt = vt_block(blk)
        half = qst.shape[1] // 2
        outs = []
        for h in range(2):
            lhs = jnp.concatenate([vt[h * HEAD_V:(h + 1) * HEAD_V], ones], axis=0)
            outs.append(jnp.dot(lhs, pt_ref[:, h * half:(h + 1) * half],
                                preferred_element_type=F32))
        return jnp.concatenate(outs, axis=1)

    diag0 = 2 * qi
    q_late = jnp.concatenate([qst[:, s * tq + tk:(s + 1) * tq] for s in segs], axis=1)
    s_late = jnp.dot(k_block(diag0 + 1), q_late, preferred_element_type=F32)
    for s in segs:
        early = slice(s * tq, s * tq + tk)
        late = slice(s * tq + tk, (s + 1) * tq)
        st = jnp.where(tri, s_late[:, s * tk:(s + 1) * tk], NEG)
        cm = jnp.max(st, axis=0, keepdims=True)
        m_ref[:, early] = jnp.full((1, tk), NEG, F32)
        m_ref[:, late] = cm
        pt_a[:, early] = jnp.zeros((tk, tk), BF16)
        pt_a[:, late] = jnp.exp2((st - cm).astype(BF16))
    s_full = jnp.dot(k_block(diag0), qst, preferred_element_type=F32)
    st = jnp.concatenate(
        [x for s in segs for x in (jnp.where(tri, s_full[:, s * tq:s * tq + tk], NEG),
                                   s_full[:, s * tq + tk:(s + 1) * tq])], axis=1)
    st_b[...] = st
    cm_b[...] = jnp.max(st, axis=0, keepdims=True)
    acc_ref[...] = jnp.zeros(acc_ref.shape, F32)

    def body(j, carry):
        first = j == 0
        alpha = softmax(st_b, cm_b, pt_b)
        scores(2 * j, st_a, cm_a)
        acc_ref[...] = alpha * (acc_ref[...] + values(jnp.where(first, diag0 + 1, 2 * j - 2), pt_a))
        alpha = softmax(st_a, cm_a, pt_a)
        scores(2 * j + 1, st_b, cm_b)
        acc_ref[...] = alpha * (acc_ref[...] + values(jnp.where(first, diag0, 2 * j - 1), pt_b))
        return carry

    lax.fori_loop(0, qi, body, 0)

    first = qi == 0
    alpha = softmax(st_b, cm_b, pt_b)
    acc_ref[...] = alpha * (acc_ref[...] + values(jnp.where(first, 1, 2 * qi - 2), pt_a))
    acc_ref[...] += values(jnp.where(first, 0, 2 * qi - 1), pt_b)
    return acc_ref[:HEAD_V, :] / acc_ref[HEAD_V:HEAD_V + 1, :]


def _mla_kernel(qn_ref, qr_ref, kn_ref, kr_ref, vt_ref, o_ref, *scratch, tq):
    tk = tq // 2
    row = lax.broadcasted_iota(jnp.int32, (LANES, tq), 0)
    zero = jnp.zeros((), BF16)

    def k_block(i):
        rows = pl.ds(pl.multiple_of(i * tk, tk), tk)
        return jnp.concatenate([kn_ref[rows, :], kr_ref[rows, :]], axis=1)

    def vt_block(i):
        return vt_ref[:, pl.ds(pl.multiple_of(i * tk, tk), tk)]

    def query_block(qi, carry):
        span = pl.ds(pl.multiple_of(qi * tq, tq), tq)
        qn, qr = qn_ref[:, span], qr_ref[:, span]
        q0 = jnp.concatenate([jnp.where(row < B_NOPE, qn, zero),
                              jnp.where(row < B_ROPE, qr, zero)], axis=0)
        q1 = jnp.concatenate([jnp.where(row >= B_NOPE, qn, zero),
                              jnp.where((row >= B_ROPE) & (row < 2 * B_ROPE), qr, zero)], axis=0)
        qst = jnp.concatenate([q0, q1], axis=1)
        ot = _flash_cols(qst, k_block, vt_block, scratch, qi=qi, tq=tq)
        pair_t = jnp.concatenate([ot[:, :tq], ot[:, tq:]], axis=0)
        o_ref[span, :] = pair_t.T.astype(o_ref.dtype)
        return carry

    lax.fori_loop(0, o_ref.shape[0] // tq, query_block, 0)


def _mla_attn(qnt, qrt, kn, kr, vt, *, tq):
    bsz, seq, width = kn.shape
    pairs = width // LANES
    tblk = pl.BlockSpec((None, LANES, seq), lambda b, p: (b, p, 0))
    kblk = pl.BlockSpec((None, seq, LANES), lambda b, p: (b, 0, p))
    krblk = pl.BlockSpec((None, seq, LANES), lambda b, p: (b, 0, 0))
    return pl.pallas_call(
        functools.partial(_mla_kernel, tq=tq),
        grid=(bsz, pairs),
        in_specs=[tblk, tblk, kblk, krblk, tblk],
        out_specs=kblk,
        out_shape=jax.ShapeDtypeStruct((bsz, seq, width), BF16),
        scratch_shapes=_flash_scratch(2 * tq, tq),
        compiler_params=pltpu.CompilerParams(
            dimension_semantics=("arbitrary",) * 2, vmem_limit_bytes=VMEM_LIMIT),
        name="mla_attn",
    )(qnt, qrt, kn, kr, vt)


def _diff_kernel(lam_ref, g_ref, q_ref, k_ref, vt_ref, o_ref, *scratch, tq, lam_init):
    tk = tq // 2
    lane = lax.broadcasted_iota(jnp.int32, (tq, LANES), 1)
    row = lax.broadcasted_iota(jnp.int32, (LANES, tq), 0)
    low = lane < C_V
    zero = jnp.zeros((), BF16)
    lp = lam_ref[...]
    lam = (jnp.exp(jnp.sum(lp[0:1] * lp[1:2], axis=1, keepdims=True))
           - jnp.exp(jnp.sum(lp[2:3] * lp[3:4], axis=1, keepdims=True)) + lam_init)

    def k_block(i):
        return k_ref[pl.ds(pl.multiple_of(i * tk, tk), tk), :]

    def vt_block(i):
        return vt_ref[:, pl.ds(pl.multiple_of(i * tk, tk), tk)]

    def query_block(qi, carry):
        span = pl.ds(pl.multiple_of(qi * tq, tq), tq)
        q = q_ref[:, span]
        qst = jnp.concatenate(
            [jnp.where((row >= j * C_QK) & (row < (j + 1) * C_QK), q, zero) for j in range(4)],
            axis=1)
        ot = _flash_cols(qst, k_block, vt_block, scratch, qi=qi, tq=tq)
        d0 = ot[:, 0:tq] - lam * ot[:, tq:2 * tq]
        d1 = ot[:, 2 * tq:3 * tq] - lam * ot[:, 3 * tq:]
        d = jnp.concatenate([d0, d1], axis=0).T
        sq = d * d
        ms_lo = jnp.sum(jnp.where(low, sq, 0.0), axis=1, keepdims=True) * (1.0 / C_V)
        ms_hi = jnp.sum(jnp.where(low, 0.0, sq), axis=1, keepdims=True) * (1.0 / C_V)
        rs = jnp.where(low, lax.rsqrt(ms_lo + SUBLN_EPS), lax.rsqrt(ms_hi + SUBLN_EPS))
        o_ref[span, :] = ((d * rs * g_ref[...]) * (1.0 - lam_init)).astype(o_ref.dtype)
        return carry

    lax.fori_loop(0, o_ref.shape[0] // tq, query_block, 0)


def _diff_attn(lam_params, g2, qt, k, vt, *, tq, lam_init):
    bsz, seq, width = k.shape
    pairs = width // LANES
    tblk = pl.BlockSpec((None, LANES, seq), lambda b, p: (b, p, 0))
    kblk = pl.BlockSpec((None, seq, LANES), lambda b, p: (b, 0, p))
    const = lambda b, p: (0, 0)
    return pl.pallas_call(
        functools.partial(_diff_kernel, tq=tq, lam_init=lam_init),
        grid=(bsz, pairs),
        in_specs=[pl.BlockSpec(lam_params.shape, const), pl.BlockSpec(g2.shape, const),
                  tblk, kblk, tblk],
        out_specs=kblk,
        out_shape=jax.ShapeDtypeStruct((bsz, seq, width), BF16),
        scratch_shapes=_flash_scratch(4 * tq, tq),
        compiler_params=pltpu.CompilerParams(
            dimension_semantics=("arbitrary",) * 2, vmem_limit_bytes=VMEM_LIMIT),
        name="diff_attn",
    )(lam_params, g2, qt, k, vt)


def _post_kernel(x_ref, oa_ref, ob_ref, oc_ref, wo_ref, g2_ref, wup_ref, wdn_ref, gf_ref,
                 y_ref, *, final):
    mixed = jnp.concatenate([oa_ref[...], ob_ref[...], oc_ref[...]], axis=1)
    x1 = x_ref[...] + jnp.dot(mixed, wo_ref[...], preferred_element_type=F32)
    hn = _rms(x1, g2_ref[...], NORM_EPS).astype(BF16)
    y_ref[...] = x1
    for c in range(wup_ref.shape[1] // D_FF_CHUNK):
        cs = slice(c * D_FF_CHUNK, (c + 1) * D_FF_CHUNK)
        u = jnp.dot(hn, wup_ref[:, cs], preferred_element_type=F32)
        a = jnp.square(jnp.maximum(u, 0.0)).astype(BF16)
        y_ref[...] += jnp.dot(a, wdn_ref[cs, :], preferred_element_type=F32)
    if final:
        y_ref[...] = _rms(y_ref[...], gf_ref[...], NORM_EPS)


def _post(x2, oa, ob, oc, wo, g2, wup, wdn, gf, *, tm, final):
    n, d = x2.shape
    row = lambda i: (i, 0)
    const = lambda i: (0, 0)
    return pl.pallas_call(
        functools.partial(_post_kernel, final=final),
        grid=(n // tm,),
        in_specs=[pl.BlockSpec((tm, d), row),
                  pl.BlockSpec((tm, oa.shape[1]), row),
                  pl.BlockSpec((tm, ob.shape[1]), row),
                  pl.BlockSpec((tm, oc.shape[1]), row),
                  pl.BlockSpec(wo.shape, const), pl.BlockSpec(g2.shape, const),
                  pl.BlockSpec(wup.shape, const), pl.BlockSpec(wdn.shape, const),
                  pl.BlockSpec(gf.shape, const)],
        out_specs=pl.BlockSpec((tm, d), row),
        out_shape=jax.ShapeDtypeStruct((n, d), F32),
        compiler_params=pltpu.CompilerParams(
            dimension_semantics=("arbitrary",), vmem_limit_bytes=VMEM_LIMIT),
        name="post_final" if final else "post",
    )(x2, oa, ob, oc, wo, g2, wup, wdn, gf)


def _rope_tables(seq):
    def tables(dim):
        half = dim // 2
        inv = 1.0 / (ROPE_THETA ** (jnp.arange(half, dtype=F32) / half))
        ang = jnp.arange(seq, dtype=F32)[:, None] * inv[None, :]
        cos, sin = jnp.cos(ang), jnp.sin(ang)
        reps = LANES // dim
        cos_t = jnp.tile(jnp.concatenate([cos, cos], axis=1), (1, reps))
        sin_t = jnp.tile(jnp.concatenate([-sin, sin], axis=1), (1, reps))
        return cos_t, sin_t
    cosa, sina = tables(A_DIM)
    cos32, sin32 = tables(C_QK)
    return cosa, sina, cos32, sin32


def _prep_layer_weights(w_in, w_uq, w_ukv):
    cols = jnp.split(w_in, [384, 768, 1152, 1536, 1664, 1696, 1952, 2208], axis=1)
    qa, ka, va, cq, ckv, kr, qc, kc, vc = cols
    w1 = jnp.concatenate([qa, ka, va, cq, ckv, jnp.tile(kr, (1, LANES // B_ROPE)), qc, kc, vc],
                         axis=1).astype(BF16)
    uq = w_uq.reshape(B_Q_LORA, B_HEADS, B_NOPE + B_ROPE)
    nope = uq[:, :, :B_NOPE].reshape(B_Q_LORA, B_HEADS * B_NOPE)
    rope = uq[:, :, B_NOPE:].reshape(B_Q_LORA, B_HEADS // 2, 2 * B_ROPE)
    rope = jnp.pad(rope, ((0, 0), (0, 0), (0, LANES - 2 * B_ROPE))).reshape(B_Q_LORA, -1)
    wuq = jnp.concatenate([nope, rope], axis=1).astype(BF16)
    ukv = w_ukv.reshape(B_KV_LORA, B_HEADS, B_NOPE + B_V)
    wukv = jnp.concatenate([ukv[:, :, :B_NOPE].reshape(B_KV_LORA, -1),
                            ukv[:, :, B_NOPE:].reshape(B_KV_LORA, -1)], axis=1).astype(BF16)
    return w1, wuq, wukv


def kernel(x, ln1_g, w_in, mla_q_norm_g, mla_w_uq, mla_kv_norm_g, mla_w_ukv, diff_lambda,
           diff_subln_g, w_o, ln2_g, w_up, w_down, final_g):
    bsz, seq, d_model = x.shape
    depth = w_in.shape[0]
    tm = 512
    tq = 512
    tabs = _rope_tables(seq)
    x2 = x.reshape(bsz * seq, d_model)
    gf = final_g.reshape(1, d_model)
    sh = lambda t, w: t.reshape(bsz, seq, w)

    for layer in range(depth):
        lam_init = 0.8 - 0.6 * math.exp(-0.3 * layer)
        w1, wuq, wukv = _prep_layer_weights(w_in[layer], mla_w_uq[layer], mla_w_ukv[layer])
        qa, ka, va, qn, qr, kn, vb, kr, qc, kc, vc = _inproj(
            x2, ln1_g[layer].reshape(1, -1), w1, tabs,
            mla_q_norm_g[layer].reshape(1, -1), wuq,
            mla_kv_norm_g[layer].reshape(1, -1), wukv, seq=seq, tm=2 * tm)

        out_a = _dilated_mixer(qa, ka, va, seq=seq, tile=2048)
        out_b = _mla_attn(qn, qr, sh(kn, 384), sh(kr, 128), vb, tq=min(2 * tq, seq))
        g_sub = jnp.tile(diff_subln_g[layer], LANES // C_V).reshape(1, LANES)
        out_c = _diff_attn(diff_lambda[layer], g_sub, qc, sh(kc, 256), vc,
                           tq=min(2 * tq, seq), lam_init=lam_init)

        x2 = _post(x2, out_a, out_b.reshape(-1, B_WIDTH),
                   out_c.reshape(-1, C_WIDTH), w_o[layer].astype(BF16),
                   ln2_g[layer].reshape(1, -1), w_up[layer].astype(BF16),
                   w_down[layer].astype(BF16), gf, tm=tm, final=layer == depth - 1)

    return x2.reshape(bsz, seq, d_model)
```

```python
import functools
import math

import jax
import jax.numpy as jnp
from jax import lax
from jax.experimental import pallas as pl
from jax.experimental.pallas import tpu as pltpu

F32 = jnp.float32
BF16 = jnp.bfloat16

LANES = 128
HEAD_V = 64
ACC_ROWS = HEAD_V + 16
VMEM_LIMIT = 56 * 1024 * 1024

ROPE_THETA = 10000.0
NORM_EPS = 1e-6
SUBLN_EPS = 1e-5
NEG = -1e30
LOG2E = math.log2(math.e)

A_HEADS, A_DIM = 6, 64
A_CONFIGS = ((128, 1), (512, 4), (2048, 16))
A_SPAN = 128
B_HEADS, B_NOPE, B_ROPE, B_V = 6, 64, 32, 64
B_Q_LORA, B_KV_LORA = 384, 128
C_HEADS, C_QK = 4, 32
C_V = 2 * C_QK
A_WIDTH = A_HEADS * A_DIM
B_WIDTH = B_HEADS * B_V
C_WIDTH = C_HEADS * C_V
D_FF_CHUNK = 512
INPROJ_SUB = 256


def _rms(x, g, eps):
    return x * lax.rsqrt(jnp.mean(x * x, axis=-1, keepdims=True) + eps) * g


def _rope(x, cos, sin_signed, half):
    lane = lax.broadcasted_iota(jnp.int32, (x.shape[0], LANES), 1)
    first = (lane & (2 * half - 1)) < half
    outs = []
    for c in range(x.shape[1] // LANES):
        xc = x[:, c * LANES:(c + 1) * LANES]
        partner = jnp.where(first, pltpu.roll(xc, LANES - half, 1), pltpu.roll(xc, half, 1))
        outs.append(xc * cos + partner * sin_signed)
    return outs[0] if len(outs) == 1 else jnp.concatenate(outs, axis=1)


def _nt_dot(a, b):
    return lax.dot_general(a, b, (((1,), (1,)), ((), ())), preferred_element_type=F32)


def _inproj_kernel(x_ref, g_ref, w1_ref, cosa_ref, sina_ref, cos32_ref, sin32_ref,
                   qg_ref, wuq_ref, kvg_ref, wukv_ref,
                   qa_ref, ka_ref, va_ref, qn_ref, qr_ref, kn_ref, vb_ref, kr_ref,
                   qc_ref, kc_ref, vc_ref):
    scale_a = A_DIM ** -0.5 * LOG2E
    scale_b = (B_NOPE + B_ROPE) ** -0.5 * LOG2E
    scale_c = C_QK ** -0.5 * LOG2E

    for r0 in range(0, x_ref.shape[0], INPROJ_SUB):
        rows = slice(r0, r0 + INPROJ_SUB)
        h = _rms(x_ref[rows, :], g_ref[...], NORM_EPS).astype(BF16)
        p_all = jnp.dot(h, w1_ref[...], preferred_element_type=F32)

        def proj(lo, width):
            return p_all[:, lo:lo + width]

        cosa, sina = cosa_ref[rows, :], sina_ref[rows, :]
        cos32, sin32 = cos32_ref[rows, :], sin32_ref[rows, :]

        qa_ref[rows, :] = _rope(proj(0, 384), cosa, sina, 32) * scale_a
        ka_ref[rows, :] = _rope(proj(384, 384), cosa, sina, 32)
        va_ref[rows, :] = proj(768, 384)

        cq = _rms(proj(1152, 384), qg_ref[...], NORM_EPS).astype(BF16)
        qf = jnp.dot(cq, wuq_ref[...], preferred_element_type=F32)
        qn_ref[:, rows] = (qf[:, :384] * scale_b).T.astype(BF16)
        qr_ref[:, rows] = (_rope(qf[:, 384:], cos32, sin32, 16) * scale_b).T.astype(BF16)

        ckv = _rms(proj(1536, 128), kvg_ref[...], NORM_EPS).astype(BF16)
        kvf = jnp.dot(ckv, wukv_ref[...], preferred_element_type=F32)
        kn_ref[rows, :] = kvf[:, :384].astype(BF16)
        vb_ref[:, rows] = kvf[:, 384:].T.astype(BF16)
        kr_ref[rows, :] = _rope(proj(1664, 128), cos32, sin32, 16).astype(BF16)

        qc_ref[:, rows] = (_rope(proj(1792, 256), cos32, sin32, 16) * scale_c).T.astype(BF16)
        kc_ref[rows, :] = _rope(proj(2048, 256), cos32, sin32, 16).astype(BF16)
        vc_ref[:, rows] = proj(2304, 256).T.astype(BF16)


def _inproj(x2, g, w1, tabs, qg, wuq, kvg, wukv, *, layer, seq, tm):
    n = x2.shape[0]
    nt = n // tm
    st = seq // tm
    row = lambda i: (i, 0)
    const = lambda i: (0, 0)
    slab = lambda i: (layer, 0, 0)
    tab = lambda i: (i % st, 0)
    widths = (384, 384, 384, 384, 384, 384, 384, 128, 256, 256, 256)
    major = (False, False, False, True, True, False, True, False, True, False, True)
    bsz = n // seq
    col = lambda i: (i // st, 0, i % st)
    out_specs = [pl.BlockSpec((None, w, tm), col) if t else pl.BlockSpec((tm, w), row)
                 for w, t in zip(widths, major)]
    dtypes = (F32,) * 3 + (BF16,) * 8
    out_shape = [jax.ShapeDtypeStruct((bsz, w, seq) if t else (n, w), dt)
                 for w, t, dt in zip(widths, major, dtypes)]
    return pl.pallas_call(
        _inproj_kernel,
        grid=(nt,),
        in_specs=[
            pl.BlockSpec((tm, x2.shape[1]), row),
            pl.BlockSpec(g.shape, const),
            pl.BlockSpec((None,) + w1.shape[1:], slab),
            pl.BlockSpec((tm, LANES), tab), pl.BlockSpec((tm, LANES), tab),
            pl.BlockSpec((tm, LANES), tab), pl.BlockSpec((tm, LANES), tab),
            pl.BlockSpec(qg.shape, const), pl.BlockSpec((None,) + wuq.shape[1:], slab),
            pl.BlockSpec(kvg.shape, const), pl.BlockSpec((None,) + wukv.shape[1:], slab),
        ],
        out_specs=out_specs,
        out_shape=out_shape,
        compiler_params=pltpu.CompilerParams(
            dimension_semantics=("arbitrary",), vmem_limit_bytes=VMEM_LIMIT),
        name="inproj",
    )(x2, g, w1, *tabs, qg, wuq, kvg, wukv)


def _dilated_kernel(q_ref, k_ref, kp_ref, v_ref, vp_ref, o_ref,
                    q4_ref, k4_ref, kp4_ref, v4_ref, vp4_ref,
                    n4_ref, d4_ref, m4_ref, nt_ref, dt_ref, mt_ref, *, tile, nt):
    quarter = tile // 4
    t = pl.program_id(0) % nt
    lane = lax.broadcasted_iota(jnp.int32, (A_SPAN, LANES), 1)
    low = lane < A_DIM
    qi = lax.broadcasted_iota(jnp.int32, (2 * A_SPAN, 2 * A_SPAN), 0) & (A_SPAN - 1)
    ki = lax.broadcasted_iota(jnp.int32, (2 * A_SPAN, 2 * A_SPAN), 1)
    dist = qi + A_SPAN - ki
    band = (dist >= 0) & (dist <= A_SPAN)
    band0 = band & (ki >= jnp.where(t > 0, 0, A_SPAN))
    ones = jnp.ones((2 * A_SPAN, LANES), BF16)
    zero = jnp.zeros((), BF16)

    for src, dst in ((q_ref, q4_ref), (k_ref, k4_ref), (kp_ref, kp4_ref),
                     (v_ref, v4_ref), (vp_ref, vp4_ref)):
        for c in range(4):
            dst[c * quarter:(c + 1) * quarter, :] = src[pl.ds(c, quarter, stride=4), :]

    def attend(qg, k_prev, k_cur, v_prev, v_cur, valid):
        qg = qg.astype(BF16)
        qs = jnp.concatenate([jnp.where(low, qg, zero), jnp.where(low, zero, qg)], axis=0)
        kk = jnp.concatenate([k_prev, k_cur], axis=0).astype(BF16)
        vv = jnp.concatenate([v_prev, v_cur], axis=0).astype(BF16)
        s = jnp.where(valid, _nt_dot(qs, kk), NEG)
        m = jnp.max(s, axis=1, keepdims=True)
        p = jnp.exp2(s - m).astype(BF16)
        pv = jnp.dot(p, jnp.concatenate([vv, ones], axis=1), preferred_element_type=F32)
        num = jnp.where(low, pv[:A_SPAN, :LANES], pv[A_SPAN:, :LANES])
        den = jnp.where(low, pv[:A_SPAN, LANES:], pv[A_SPAN:, LANES:])
        return num, den, jnp.where(low, m[:A_SPAN], m[A_SPAN:])

    def merge(a, b):
        (num_a, den_a, m_a), (num_b, den_b, m_b) = a, b
        mx = jnp.maximum(m_a, m_b)
        wa = jnp.exp2(m_a - mx)
        wb = jnp.exp2(m_b - mx)
        return wa * num_a + wb * num_b, wa * den_a + wb * den_b, mx

    def load(refs, rows):
        return tuple(r[rows, :] for r in refs)

    def store(refs, rows, state):
        for r, x in zip(refs, state):
            r[rows, :] = x

    state4 = (n4_ref, d4_ref, m4_ref)
    state_t = (nt_ref, dt_ref, mt_ref)

    for c in range(4):
        for g in range(quarter // A_SPAN):
            cur = pl.ds(c * quarter + g * A_SPAN, A_SPAN)
            if g == 0:
                prev = pl.ds((c + 1) * quarter - A_SPAN, A_SPAN)
                k_prev, v_prev, valid = kp4_ref[prev, :], vp4_ref[prev, :], band0
            else:
                prev = pl.ds(c * quarter + (g - 1) * A_SPAN, A_SPAN)
                k_prev, v_prev, valid = k4_ref[prev, :], v4_ref[prev, :], band
            store(state4, cur,
                  attend(q4_ref[cur, :], k_prev, k4_ref[cur, :], v_prev, v4_ref[cur, :], valid))

    assert tile == A_SPAN * 16
    for c in range(4):
        for a in range(4):
            cur = pl.ds(c * quarter + a, A_SPAN, stride=4)
            new = attend(q4_ref[cur, :], kp4_ref[cur, :], k4_ref[cur, :],
                         vp4_ref[cur, :], v4_ref[cur, :], band0)
            store(state4, cur, merge(new, load(state4, cur)))

    for c in range(4):
        for r4, rt in zip(state4, state_t):
            rt[pl.ds(c, quarter, stride=4), :] = r4[c * quarter:(c + 1) * quarter, :]

    for g in range(tile // A_SPAN):
        cur = pl.ds(g * A_SPAN, A_SPAN)
        if g == 0:
            prev = pl.ds(tile - A_SPAN, A_SPAN)
            k_prev, v_prev, valid = kp_ref[prev, :], vp_ref[prev, :], band0
        else:
            prev = pl.ds((g - 1) * A_SPAN, A_SPAN)
            k_prev, v_prev, valid = k_ref[prev, :], v_ref[prev, :], band
        new = attend(q_ref[cur, :], k_prev, k_ref[cur, :], v_prev, v_ref[cur, :], valid)
        num, den, _ = merge(new, load(state_t, cur))
        o_ref[cur, :] = (num / den).astype(o_ref.dtype)


def _dilated_mixer(q, k, v, *, seq, tile):
    n, width = q.shape
    nt = seq // tile
    for window, dilation in A_CONFIGS:
        assert window // dilation == A_SPAN and tile % window == 0
    cur = lambda i, p: (i, p)
    prev = lambda i, p: (jnp.where(i % nt == 0, i, i - 1), p)
    blk = pl.BlockSpec((tile, LANES), cur)
    pblk = pl.BlockSpec((tile, LANES), prev)
    return pl.pallas_call(
        functools.partial(_dilated_kernel, tile=tile, nt=nt),
        grid=(n // tile, width // LANES),
        in_specs=[blk, blk, pblk, blk, pblk],
        out_specs=blk,
        out_shape=jax.ShapeDtypeStruct((n, width), BF16),
        scratch_shapes=[pltpu.VMEM((tile, LANES), F32)] * 11,
        compiler_params=pltpu.CompilerParams(
            dimension_semantics=("arbitrary",) * 2, vmem_limit_bytes=VMEM_LIMIT),
        name="dilated",
    )(q, k, k, v, v)


def _flash_scratch(rows, tq):
    tk = tq // 2
    score_bufs = [pltpu.VMEM((tk, rows), F32), pltpu.VMEM((1, rows), F32),
                  pltpu.VMEM((tk, rows), BF16)]
    return [pltpu.VMEM((1, rows), F32), pltpu.VMEM((ACC_ROWS, rows), F32)] + 2 * score_bufs


def _flash_cols(qst, k_block, vt_block, scratch, *, qi, tq):
    m_ref, acc_ref, st_a, cm_a, pt_a, st_b, cm_b, pt_b = scratch
    tk = tq // 2
    segs = range(qst.shape[1] // tq)
    ones = jnp.ones((ACC_ROWS - HEAD_V, tk), BF16)
    tri = (lax.broadcasted_iota(jnp.int32, (tk, tk), 0)
           <= lax.broadcasted_iota(jnp.int32, (tk, tk), 1))

    def scores(blk, st_ref, cm_ref):
        st = jnp.dot(k_block(blk), qst, preferred_element_type=F32)
        st_ref[...] = st
        cm_ref[...] = jnp.max(st, axis=0, keepdims=True)

    def softmax(st_ref, cm_ref, pt_ref):
        m_prev = m_ref[...]
        m_new = jnp.maximum(m_prev, cm_ref[...])
        m_ref[...] = m_new
        pt_ref[...] = jnp.exp2(st_ref[...] - m_new).astype(BF16)
        return jnp.exp2(m_prev - m_new)

    def values(blk, pt_ref):
        vt = vt_block(blk)
        half = qst.shape[1] // 2
        outs = []
        for h in range(2):
            lhs = jnp.concatenate([vt[h * HEAD_V:(h + 1) * HEAD_V], ones], axis=0)
            outs.append(jnp.dot(lhs, pt_ref[:, h * half:(h + 1) * half],
                                preferred_element_type=F32))
        return jnp.concatenate(outs, axis=1)

    diag0 = 2 * qi
    q_late = jnp.concatenate([qst[:, s * tq + tk:(s + 1) * tq] for s in segs], axis=1)
    s_late = jnp.dot(k_block(diag0 + 1), q_late, preferred_element_type=F32)
    for s in segs:
        early = slice(s * tq, s * tq + tk)
        late = slice(s * tq + tk, (s + 1) * tq)
        st = jnp.where(tri, s_late[:, s * tk:(s + 1) * tk], NEG)
        cm = jnp.max(st, axis=0, keepdims=True)
        m_ref[:, early] = jnp.full((1, tk), NEG, F32)
        m_ref[:, late] = cm
        pt_a[:, early] = jnp.zeros((tk, tk), BF16)
        pt_a[:, late] = jnp.exp2(st - cm).astype(BF16)
    s_full = jnp.dot(k_block(diag0), qst, preferred_element_type=F32)
    st = jnp.concatenate(
        [x for s in segs for x in (jnp.where(tri, s_full[:, s * tq:s * tq + tk], NEG),
                                   s_full[:, s * tq + tk:(s + 1) * tq])], axis=1)
    st_b[...] = st
    cm_b[...] = jnp.max(st, axis=0, keepdims=True)
    acc_ref[...] = jnp.zeros(acc_ref.shape, F32)

    def body(j, carry):
        first = j == 0
        alpha = softmax(st_b, cm_b, pt_b)
        scores(2 * j, st_a, cm_a)
        acc_ref[...] = alpha * (acc_ref[...] + values(jnp.where(first, diag0 + 1, 2 * j - 2), pt_a))
        alpha = softmax(st_a, cm_a, pt_a)
        scores(2 * j + 1, st_b, cm_b)
        acc_ref[...] = alpha * (acc_ref[...] + values(jnp.where(first, diag0, 2 * j - 1), pt_b))
        return carry

    lax.fori_loop(0, qi, body, 0)

    first = qi == 0
    alpha = softmax(st_b, cm_b, pt_b)
    acc_ref[...] = alpha * (acc_ref[...] + values(jnp.where(first, 1, 2 * qi - 2), pt_a))
    acc_ref[...] += values(jnp.where(first, 0, 2 * qi - 1), pt_b)
    return acc_ref[:HEAD_V, :] / acc_ref[HEAD_V:HEAD_V + 1, :]


def _mla_kernel(qn_ref, qr_ref, kn_ref, kr_ref, vt_ref, o_ref, *scratch, tq):
    tk = tq // 2
    row = lax.broadcasted_iota(jnp.int32, (LANES, tq), 0)
    zero = jnp.zeros((), BF16)

    def k_block(i):
        rows = pl.ds(pl.multiple_of(i * tk, tk), tk)
        return jnp.concatenate([kn_ref[rows, :], kr_ref[rows, :]], axis=1)

    def vt_block(i):
        return vt_ref[:, pl.ds(pl.multiple_of(i * tk, tk), tk)]

    def query_block(qi, carry):
        span = pl.ds(pl.multiple_of(qi * tq, tq), tq)
        qn, qr = qn_ref[:, span], qr_ref[:, span]
        q0 = jnp.concatenate([jnp.where(row < B_NOPE, qn, zero),
                              jnp.where(row < B_ROPE, qr, zero)], axis=0)
        q1 = jnp.concatenate([jnp.where(row >= B_NOPE, qn, zero),
                              jnp.where((row >= B_ROPE) & (row < 2 * B_ROPE), qr, zero)], axis=0)
        qst = jnp.concatenate([q0, q1], axis=1)
        ot = _flash_cols(qst, k_block, vt_block, scratch, qi=qi, tq=tq)
        pair_t = jnp.concatenate([ot[:, :tq], ot[:, tq:]], axis=0)
        o_ref[span, :] = pair_t.T.astype(o_ref.dtype)
        return carry

    lax.fori_loop(0, o_ref.shape[0] // tq, query_block, 0)


def _mla_attn(qnt, qrt, kn, kr, vt, *, tq):
    bsz, seq, width = kn.shape
    pairs = width // LANES
    tblk = pl.BlockSpec((None, LANES, seq), lambda b, p: (b, p, 0))
    kblk = pl.BlockSpec((None, seq, LANES), lambda b, p: (b, 0, p))
    krblk = pl.BlockSpec((None, seq, LANES), lambda b, p: (b, 0, 0))
    return pl.pallas_call(
        functools.partial(_mla_kernel, tq=tq),
        grid=(bsz, pairs),
        in_specs=[tblk, tblk, kblk, krblk, tblk],
        out_specs=kblk,
        out_shape=jax.ShapeDtypeStruct((bsz, seq, width), BF16),
        scratch_shapes=_flash_scratch(2 * tq, tq),
        compiler_params=pltpu.CompilerParams(
            dimension_semantics=("arbitrary",) * 2, vmem_limit_bytes=VMEM_LIMIT),
        name="mla_attn",
    )(qnt, qrt, kn, kr, vt)


def _diff_kernel(lam_ref, g_ref, q_ref, k_ref, vt_ref, o_ref, *scratch, tq, lam_init):
    tk = tq // 2
    lane = lax.broadcasted_iota(jnp.int32, (tq, LANES), 1)
    row = lax.broadcasted_iota(jnp.int32, (LANES, tq), 0)
    low = lane < C_V
    zero = jnp.zeros((), BF16)
    lp = lam_ref[...]
    lam = (jnp.exp(jnp.sum(lp[0:1] * lp[1:2], axis=1, keepdims=True))
           - jnp.exp(jnp.sum(lp[2:3] * lp[3:4], axis=1, keepdims=True)) + lam_init)

    def k_block(i):
        return k_ref[pl.ds(pl.multiple_of(i * tk, tk), tk), :]

    def vt_block(i):
        return vt_ref[:, pl.ds(pl.multiple_of(i * tk, tk), tk)]

    def query_block(qi, carry):
        span = pl.ds(pl.multiple_of(qi * tq, tq), tq)
        q = q_ref[:, span]
        qst = jnp.concatenate(
            [jnp.where((row >= j * C_QK) & (row < (j + 1) * C_QK), q, zero) for j in range(4)],
            axis=1)
        ot = _flash_cols(qst, k_block, vt_block, scratch, qi=qi, tq=tq)
        d0 = ot[:, 0:tq] - lam * ot[:, tq:2 * tq]
        d1 = ot[:, 2 * tq:3 * tq] - lam * ot[:, 3 * tq:]
        d = jnp.concatenate([d0, d1], axis=0).T
        sq = d * d
        ms_lo = jnp.sum(jnp.where(low, sq, 0.0), axis=1, keepdims=True) * (1.0 / C_V)
        ms_hi = jnp.sum(jnp.where(low, 0.0, sq), axis=1, keepdims=True) * (1.0 / C_V)
        rs = jnp.where(low, lax.rsqrt(ms_lo + SUBLN_EPS), lax.rsqrt(ms_hi + SUBLN_EPS))
        o_ref[span, :] = ((d * rs * g_ref[...]) * (1.0 - lam_init)).astype(o_ref.dtype)
        return carry

    lax.fori_loop(0, o_ref.shape[0] // tq, query_block, 0)


def _diff_attn(lam_params, g2, qt, k, vt, *, tq, lam_init):
    bsz, seq, width = k.shape
    pairs = width // LANES
    tblk = pl.BlockSpec((None, LANES, seq), lambda b, p: (b, p, 0))
    kblk = pl.BlockSpec((None, seq, LANES), lambda b, p: (b, 0, p))
    const = lambda b, p: (0, 0)
    return pl.pallas_call(
        functools.partial(_diff_kernel, tq=tq, lam_init=lam_init),
        grid=(bsz, pairs),
        in_specs=[pl.BlockSpec(lam_params.shape, const), pl.BlockSpec(g2.shape, const),
                  tblk, kblk, tblk],
        out_specs=kblk,
        out_shape=jax.ShapeDtypeStruct((bsz, seq, width), BF16),
        scratch_shapes=_flash_scratch(4 * tq, tq),
        compiler_params=pltpu.CompilerParams(
            dimension_semantics=("arbitrary",) * 2, vmem_limit_bytes=VMEM_LIMIT),
        name="diff_attn",
    )(lam_params, g2, qt, k, vt)


def _post_kernel(x_ref, oa_ref, ob_ref, oc_ref, wo_ref, g2_ref, wup_ref, wdn_ref, gf_ref,
                 y_ref, *, final):
    mixed = jnp.concatenate([oa_ref[...], ob_ref[...], oc_ref[...]], axis=1)
    x1 = x_ref[...] + jnp.dot(mixed, wo_ref[...], preferred_element_type=F32)
    hn = _rms(x1, g2_ref[...], NORM_EPS).astype(BF16)
    y_ref[...] = x1
    for c in range(wup_ref.shape[1] // D_FF_CHUNK):
        cs = slice(c * D_FF_CHUNK, (c + 1) * D_FF_CHUNK)
        u = jnp.dot(hn, wup_ref[:, cs], preferred_element_type=F32)
        a = jnp.square(jnp.maximum(u, 0.0)).astype(BF16)
        y_ref[...] += jnp.dot(a, wdn_ref[cs, :], preferred_element_type=F32)
    if final:
        y_ref[...] = _rms(y_ref[...], gf_ref[...], NORM_EPS)


def _post(x2, oa, ob, oc, wo, g2, wup, wdn, gf, *, layer, tm, final):
    n, d = x2.shape
    row = lambda i: (i, 0)
    const = lambda i: (0, 0)
    slab = lambda i: (layer, 0, 0)
    return pl.pallas_call(
        functools.partial(_post_kernel, final=final),
        grid=(n // tm,),
        in_specs=[pl.BlockSpec((tm, d), row),
                  pl.BlockSpec((tm, oa.shape[1]), row),
                  pl.BlockSpec((tm, ob.shape[1]), row),
                  pl.BlockSpec((tm, oc.shape[1]), row),
                  pl.BlockSpec((None,) + wo.shape[1:], slab), pl.BlockSpec(g2.shape, const),
                  pl.BlockSpec((None,) + wup.shape[1:], slab),
                  pl.BlockSpec((None,) + wdn.shape[1:], slab),
                  pl.BlockSpec(gf.shape, const)],
        out_specs=pl.BlockSpec((tm, d), row),
        out_shape=jax.ShapeDtypeStruct((n, d), F32),
        compiler_params=pltpu.CompilerParams(
            dimension_semantics=("arbitrary",), vmem_limit_bytes=VMEM_LIMIT),
        name="post_final" if final else "post",
    )(x2, oa, ob, oc, wo, g2, wup, wdn, gf)


def _rope_tables(seq):
    def tables(dim):
        half = dim // 2
        inv = 1.0 / (ROPE_THETA ** (jnp.arange(half, dtype=F32) / half))
        ang = jnp.arange(seq, dtype=F32)[:, None] * inv[None, :]
        cos, sin = jnp.cos(ang), jnp.sin(ang)
        reps = LANES // dim
        cos_t = jnp.tile(jnp.concatenate([cos, cos], axis=1), (1, reps))
        sin_t = jnp.tile(jnp.concatenate([-sin, sin], axis=1), (1, reps))
        return cos_t, sin_t
    cosa, sina = tables(A_DIM)
    cos32, sin32 = tables(C_QK)
    return cosa, sina, cos32, sin32


def _prep_weights(w_in, w_uq, w_ukv):
    depth = w_in.shape[0]
    cols = jnp.split(w_in, [384, 768, 1152, 1536, 1664, 1696, 1952, 2208], axis=2)
    qa, ka, va, cq, ckv, kr, qc, kc, vc = cols
    w1 = jnp.concatenate([qa, ka, va, cq, ckv, jnp.tile(kr, (1, 1, LANES // B_ROPE)), qc, kc, vc],
                         axis=2).astype(BF16)
    uq = w_uq.reshape(depth, B_Q_LORA, B_HEADS, B_NOPE + B_ROPE)
    nope = uq[..., :B_NOPE].reshape(depth, B_Q_LORA, B_HEADS * B_NOPE)
    rope = uq[..., B_NOPE:].reshape(depth, B_Q_LORA, B_HEADS // 2, 2 * B_ROPE)
    rope = jnp.pad(rope, ((0, 0), (0, 0), (0, 0), (0, LANES - 2 * B_ROPE)))
    wuq = jnp.concatenate([nope, rope.reshape(depth, B_Q_LORA, -1)], axis=2).astype(BF16)
    ukv = w_ukv.reshape(depth, B_KV_LORA, B_HEADS, B_NOPE + B_V)
    wukv = jnp.concatenate([ukv[..., :B_NOPE].reshape(depth, B_KV_LORA, -1),
                            ukv[..., B_NOPE:].reshape(depth, B_KV_LORA, -1)],
                           axis=2).astype(BF16)
    return w1, wuq, wukv


def kernel(x, ln1_g, w_in, mla_q_norm_g, mla_w_uq, mla_kv_norm_g, mla_w_ukv, diff_lambda,
           diff_subln_g, w_o, ln2_g, w_up, w_down, final_g):
    bsz, seq, d_model = x.shape
    depth = w_in.shape[0]
    tm = 512
    tq = 512
    tabs = _rope_tables(seq)
    x2 = x.reshape(bsz * seq, d_model)
    gf = final_g.reshape(1, d_model)
    sh = lambda t, w: t.reshape(bsz, seq, w)
    w1, wuq, wukv = _prep_weights(w_in, mla_w_uq, mla_w_ukv)
    wo, wup, wdn = w_o.astype(BF16), w_up.astype(BF16), w_down.astype(BF16)

    for layer in range(depth):
        lam_init = 0.8 - 0.6 * math.exp(-0.3 * layer)
        qa, ka, va, qn, qr, kn, vb, kr, qc, kc, vc = _inproj(
            x2, ln1_g[layer].reshape(1, -1), w1, tabs,
            mla_q_norm_g[layer].reshape(1, -1), wuq,
            mla_kv_norm_g[layer].reshape(1, -1), wukv, layer=layer, seq=seq, tm=2 * tm)

        out_a = _dilated_mixer(qa, ka, va, seq=seq, tile=2048)
        out_b = _mla_attn(qn, qr, sh(kn, 384), sh(kr, 128), vb, tq=min(2 * tq, seq))
        g_sub = jnp.tile(diff_subln_g[layer], LANES // C_V).reshape(1, LANES)
        out_c = _diff_attn(diff_lambda[layer], g_sub, qc, sh(kc, 256), vc,
                           tq=min(2 * tq, seq), lam_init=lam_init)

        x2 = _post(x2, out_a, out_b.reshape(-1, B_WIDTH), out_c.reshape(-1, C_WIDTH), wo,
                   ln2_g[layer].reshape(1, -1), wup, wdn, gf,
                   layer=layer, tm=tm, final=layer == depth - 1)

    return x2.reshape(bsz, seq, d_model)
```

```python
import functools
import math

import jax
import jax.numpy as jnp
from jax import lax
from jax.experimental import pallas as pl
from jax.experimental.pallas import tpu as pltpu

F32 = jnp.float32
BF16 = jnp.bfloat16

LANES = 128
HEAD_V = 64
ACC_ROWS = HEAD_V + 16
VMEM_LIMIT = 56 * 1024 * 1024

ROPE_THETA = 10000.0
NORM_EPS = 1e-6
SUBLN_EPS = 1e-5
NEG = -1e30
LOG2E = math.log2(math.e)

A_HEADS, A_DIM = 6, 64
A_CONFIGS = ((128, 1), (512, 4), (2048, 16))
A_SPAN = 128
B_HEADS, B_NOPE, B_ROPE, B_V = 6, 64, 32, 64
B_Q_LORA, B_KV_LORA = 384, 128
C_HEADS, C_QK = 4, 32
C_V = 2 * C_QK
A_WIDTH = A_HEADS * A_DIM
B_WIDTH = B_HEADS * B_V
C_WIDTH = C_HEADS * C_V
D_FF_CHUNK = 1024
INPROJ_SUB = 512


def _rms(x, g, eps):
    return x * lax.rsqrt(jnp.mean(x * x, axis=-1, keepdims=True) + eps) * g


def _rope(x, cos, sin_signed, half):
    lane = lax.broadcasted_iota(jnp.int32, (x.shape[0], LANES), 1)
    first = (lane & (2 * half - 1)) < half
    outs = []
    for c in range(x.shape[1] // LANES):
        xc = x[:, c * LANES:(c + 1) * LANES]
        partner = jnp.where(first, pltpu.roll(xc, LANES - half, 1), pltpu.roll(xc, half, 1))
        outs.append(xc * cos + partner * sin_signed)
    return outs[0] if len(outs) == 1 else jnp.concatenate(outs, axis=1)


def _nt_dot(a, b):
    return lax.dot_general(a, b, (((1,), (1,)), ((), ())), preferred_element_type=F32)


def _inproj_kernel(x_ref, g_ref, w1_ref, cosa_ref, sina_ref, cos32_ref, sin32_ref,
                   qg_ref, wuq_ref, kvg_ref, wukv_ref,
                   qa_ref, ka_ref, va_ref, qn_ref, qr_ref, kn_ref, vb_ref, kr_ref,
                   qc_ref, kc_ref, vc_ref):
    scale_a = A_DIM ** -0.5 * LOG2E
    scale_b = (B_NOPE + B_ROPE) ** -0.5 * LOG2E
    scale_c = C_QK ** -0.5 * LOG2E

    for r0 in range(0, x_ref.shape[0], INPROJ_SUB):
        rows = slice(r0, r0 + INPROJ_SUB)
        h = _rms(x_ref[rows, :], g_ref[...], NORM_EPS).astype(BF16)
        p_all = jnp.dot(h, w1_ref[...], preferred_element_type=F32)

        def proj(lo, width):
            return p_all[:, lo:lo + width]

        cosa, sina = cosa_ref[rows, :], sina_ref[rows, :]
        cos32, sin32 = cos32_ref[rows, :], sin32_ref[rows, :]

        qa_ref[rows, :] = _rope(proj(0, 384), cosa, sina, 32) * scale_a
        ka_ref[rows, :] = _rope(proj(384, 384), cosa, sina, 32)
        va_ref[rows, :] = proj(768, 384)

        cq = _rms(proj(1152, 384), qg_ref[...], NORM_EPS).astype(BF16)
        qf = jnp.dot(cq, wuq_ref[...], preferred_element_type=F32)
        qn_ref[:, rows] = (qf[:, :384] * scale_b).T.astype(BF16)
        qr_ref[:, rows] = (_rope(qf[:, 384:], cos32, sin32, 16) * scale_b).T.astype(BF16)

        ckv = _rms(proj(1536, 128), kvg_ref[...], NORM_EPS).astype(BF16)
        kvf = jnp.dot(ckv, wukv_ref[...], preferred_element_type=F32)
        kn_ref[rows, :] = kvf[:, :384].astype(BF16)
        vb_ref[:, rows] = kvf[:, 384:].T.astype(BF16)
        kr_ref[rows, :] = _rope(proj(1664, 128), cos32, sin32, 16).astype(BF16)

        qc_ref[:, rows] = (_rope(proj(1792, 256), cos32, sin32, 16) * scale_c).T.astype(BF16)
        kc_ref[rows, :] = _rope(proj(2048, 256), cos32, sin32, 16).astype(BF16)
        vc_ref[:, rows] = proj(2304, 256).T.astype(BF16)


def _inproj(x2, g, w1, tabs, qg, wuq, kvg, wukv, *, layer, seq, tm):
    n = x2.shape[0]
    nt = n // tm
    st = seq // tm
    row = lambda i: (i, 0)
    const = lambda i: (0, 0)
    slab = lambda i: (layer, 0, 0)
    tab = lambda i: (i % st, 0)
    widths = (384, 384, 384, 384, 384, 384, 384, 128, 256, 256, 256)
    major = (False, False, False, True, True, False, True, False, True, False, True)
    bsz = n // seq
    col = lambda i: (i // st, 0, i % st)
    out_specs = [pl.BlockSpec((None, w, tm), col) if t else pl.BlockSpec((tm, w), row)
                 for w, t in zip(widths, major)]
    dtypes = (F32,) * 3 + (BF16,) * 8
    out_shape = [jax.ShapeDtypeStruct((bsz, w, seq) if t else (n, w), dt)
                 for w, t, dt in zip(widths, major, dtypes)]
    return pl.pallas_call(
        _inproj_kernel,
        grid=(nt,),
        in_specs=[
            pl.BlockSpec((tm, x2.shape[1]), row),
            pl.BlockSpec(g.shape, const),
            pl.BlockSpec((None,) + w1.shape[1:], slab),
            pl.BlockSpec((tm, LANES), tab), pl.BlockSpec((tm, LANES), tab),
            pl.BlockSpec((tm, LANES), tab), pl.BlockSpec((tm, LANES), tab),
            pl.BlockSpec(qg.shape, const), pl.BlockSpec((None,) + wuq.shape[1:], slab),
            pl.BlockSpec(kvg.shape, const), pl.BlockSpec((None,) + wukv.shape[1:], slab),
        ],
        out_specs=out_specs,
        out_shape=out_shape,
        compiler_params=pltpu.CompilerParams(
            dimension_semantics=("arbitrary",), vmem_limit_bytes=VMEM_LIMIT),
        name="inproj",
    )(x2, g, w1, *tabs, qg, wuq, kvg, wukv)


def _dilated_kernel(q_ref, k_ref, kp_ref, v_ref, vp_ref, o_ref,
                    q4_ref, k4_ref, kp4_ref, v4_ref, vp4_ref,
                    n4_ref, d4_ref, m4_ref, nt_ref, dt_ref, mt_ref, *, tile, nt):
    quarter = tile // 4
    t = pl.program_id(0) % nt
    lane = lax.broadcasted_iota(jnp.int32, (A_SPAN, LANES), 1)
    low = lane < A_DIM
    qi = lax.broadcasted_iota(jnp.int32, (2 * A_SPAN, 2 * A_SPAN), 0) & (A_SPAN - 1)
    ki = lax.broadcasted_iota(jnp.int32, (2 * A_SPAN, 2 * A_SPAN), 1)
    dist = qi + A_SPAN - ki
    band = (dist >= 0) & (dist <= A_SPAN)
    band0 = band & (ki >= jnp.where(t > 0, 0, A_SPAN))
    ones = jnp.ones((2 * A_SPAN, LANES), BF16)
    zero = jnp.zeros((), BF16)

    for src, dst in ((q_ref, q4_ref), (k_ref, k4_ref), (kp_ref, kp4_ref),
                     (v_ref, v4_ref), (vp_ref, vp4_ref)):
        for c in range(4):
            dst[c * quarter:(c + 1) * quarter, :] = src[pl.ds(c, quarter, stride=4), :]

    def attend(qg, k_prev, k_cur, v_prev, v_cur, valid):
        qg = qg.astype(BF16)
        qs = jnp.concatenate([jnp.where(low, qg, zero), jnp.where(low, zero, qg)], axis=0)
        kk = jnp.concatenate([k_prev, k_cur], axis=0).astype(BF16)
        vv = jnp.concatenate([v_prev, v_cur], axis=0).astype(BF16)
        s = jnp.where(valid, _nt_dot(qs, kk), NEG)
        m = jnp.max(s, axis=1, keepdims=True)
        p = jnp.exp2(s - m).astype(BF16)
        pv = jnp.dot(p, jnp.concatenate([vv, ones], axis=1), preferred_element_type=F32)
        num = jnp.where(low, pv[:A_SPAN, :LANES], pv[A_SPAN:, :LANES])
        den = jnp.where(low, pv[:A_SPAN, LANES:], pv[A_SPAN:, LANES:])
        return num, den, jnp.where(low, m[:A_SPAN], m[A_SPAN:])

    def merge(a, b):
        (num_a, den_a, m_a), (num_b, den_b, m_b) = a, b
        mx = jnp.maximum(m_a, m_b)
        wa = jnp.exp2(m_a - mx)
        wb = jnp.exp2(m_b - mx)
        return wa * num_a + wb * num_b, wa * den_a + wb * den_b, mx

    def load(refs, rows):
        return tuple(r[rows, :] for r in refs)

    def store(refs, rows, state):
        for r, x in zip(refs, state):
            r[rows, :] = x

    state4 = (n4_ref, d4_ref, m4_ref)
    state_t = (nt_ref, dt_ref, mt_ref)

    for c in range(4):
        for g in range(quarter // A_SPAN):
            cur = pl.ds(c * quarter + g * A_SPAN, A_SPAN)
            if g == 0:
                prev = pl.ds((c + 1) * quarter - A_SPAN, A_SPAN)
                k_prev, v_prev, valid = kp4_ref[prev, :], vp4_ref[prev, :], band0
            else:
                prev = pl.ds(c * quarter + (g - 1) * A_SPAN, A_SPAN)
                k_prev, v_prev, valid = k4_ref[prev, :], v4_ref[prev, :], band
            store(state4, cur,
                  attend(q4_ref[cur, :], k_prev, k4_ref[cur, :], v_prev, v4_ref[cur, :], valid))

    assert tile == A_SPAN * 16
    for c in range(4):
        for a in range(4):
            cur = pl.ds(c * quarter + a, A_SPAN, stride=4)
            new = attend(q4_ref[cur, :], kp4_ref[cur, :], k4_ref[cur, :],
                         vp4_ref[cur, :], v4_ref[cur, :], band0)
            store(state4, cur, merge(new, load(state4, cur)))

    for c in range(4):
        for r4, rt in zip(state4, state_t):
            rt[pl.ds(c, quarter, stride=4), :] = r4[c * quarter:(c + 1) * quarter, :]

    for g in range(tile // A_SPAN):
        cur = pl.ds(g * A_SPAN, A_SPAN)
        if g == 0:
            prev = pl.ds(tile - A_SPAN, A_SPAN)
            k_prev, v_prev, valid = kp_ref[prev, :], vp_ref[prev, :], band0
        else:
            prev = pl.ds((g - 1) * A_SPAN, A_SPAN)
            k_prev, v_prev, valid = k_ref[prev, :], v_ref[prev, :], band
        new = attend(q_ref[cur, :], k_prev, k_ref[cur, :], v_prev, v_ref[cur, :], valid)
        num, den, _ = merge(new, load(state_t, cur))
        o_ref[cur, :] = (num / den).astype(o_ref.dtype)


def _dilated_mixer(q, k, v, *, seq, tile):
    n, width = q.shape
    nt = seq // tile
    for window, dilation in A_CONFIGS:
        assert window // dilation == A_SPAN and tile % window == 0
    cur = lambda i, p: (i, p)
    prev = lambda i, p: (jnp.where(i % nt == 0, i, i - 1), p)
    blk = pl.BlockSpec((tile, LANES), cur)
    pblk = pl.BlockSpec((tile, LANES), prev)
    return pl.pallas_call(
        functools.partial(_dilated_kernel, tile=tile, nt=nt),
        grid=(n // tile, width // LANES),
        in_specs=[blk, blk, pblk, blk, pblk],
        out_specs=blk,
        out_shape=jax.ShapeDtypeStruct((n, width), BF16),
        scratch_shapes=[pltpu.VMEM((tile, LANES), F32)] * 11,
        compiler_params=pltpu.CompilerParams(
            dimension_semantics=("arbitrary",) * 2, vmem_limit_bytes=VMEM_LIMIT),
        name="dilated",
    )(q, k, k, v, v)


def _flash_scratch(rows, tq):
    tk = tq // 2
    score_bufs = [pltpu.VMEM((tk, rows), F32), pltpu.VMEM((1, rows), F32),
                  pltpu.VMEM((tk, rows), BF16)]
    return [pltpu.VMEM((1, rows), F32), pltpu.VMEM((ACC_ROWS, rows), F32)] + 2 * score_bufs


def _flash_cols(qst, k_block, vt_block, scratch, *, qi, tq):
    m_ref, acc_ref, st_a, cm_a, pt_a, st_b, cm_b, pt_b = scratch
    tk = tq // 2
    segs = range(qst.shape[1] // tq)
    ones = jnp.ones((ACC_ROWS - HEAD_V, tk), BF16)
    tri = (lax.broadcasted_iota(jnp.int32, (tk, tk), 0)
           <= lax.broadcasted_iota(jnp.int32, (tk, tk), 1))

    def scores(blk, st_ref, cm_ref):
        st = jnp.dot(k_block(blk), qst, preferred_element_type=F32)
        st_ref[...] = st
        cm_ref[...] = jnp.max(st, axis=0, keepdims=True)

    def softmax(st_ref, cm_ref, pt_ref):
        m_prev = m_ref[...]
        m_new = jnp.maximum(m_prev, cm_ref[...])
        m_ref[...] = m_new
        pt_ref[...] = jnp.exp2(st_ref[...] - m_new).astype(BF16)
        return jnp.exp2(m_prev - m_new)

    def values(blk, pt_ref):
        vt = vt_block(blk)
        half = qst.shape[1] // 2
        outs = []
        for h in range(2):
            lhs = jnp.concatenate([vt[h * HEAD_V:(h + 1) * HEAD_V], ones], axis=0)
            outs.append(jnp.dot(lhs, pt_ref[:, h * half:(h + 1) * half],
                                preferred_element_type=F32))
        return jnp.concatenate(outs, axis=1)

    diag0 = 2 * qi
    q_late = jnp.concatenate([qst[:, s * tq + tk:(s + 1) * tq] for s in segs], axis=1)
    s_late = jnp.dot(k_block(diag0 + 1), q_late, preferred_element_type=F32)
    for s in segs:
        early = slice(s * tq, s * tq + tk)
        late = slice(s * tq + tk, (s + 1) * tq)
        st = jnp.where(tri, s_late[:, s * tk:(s + 1) * tk], NEG)
        cm = jnp.max(st, axis=0, keepdims=True)
        m_ref[:, early] = jnp.full((1, tk), NEG, F32)
        m_ref[:, late] = cm
        pt_a[:, early] = jnp.zeros((tk, tk), BF16)
        pt_a[:, late] = jnp.exp2(st - cm).astype(BF16)
    s_full = jnp.dot(k_block(diag0), qst, preferred_element_type=F32)
    st = jnp.concatenate(
        [x for s in segs for x in (jnp.where(tri, s_full[:, s * tq:s * tq + tk], NEG),
                                   s_full[:, s * tq + tk:(s + 1) * tq])], axis=1)
    st_b[...] = st
    cm_b[...] = jnp.max(st, axis=0, keepdims=True)
    acc_ref[...] = jnp.zeros(acc_ref.shape, F32)

    def body(j, carry):
        first = j == 0
        alpha = softmax(st_b, cm_b, pt_b)
        scores(2 * j, st_a, cm_a)
        acc_ref[...] = alpha * (acc_ref[...] + values(jnp.where(first, diag0 + 1, 2 * j - 2), pt_a))
        alpha = softmax(st_a, cm_a, pt_a)
        scores(2 * j + 1, st_b, cm_b)
        acc_ref[...] = alpha * (acc_ref[...] + values(jnp.where(first, diag0, 2 * j - 1), pt_b))
        return carry

    lax.fori_loop(0, qi, body, 0)

    first = qi == 0
    alpha = softmax(st_b, cm_b, pt_b)
    acc_ref[...] = alpha * (acc_ref[...] + values(jnp.where(first, 1, 2 * qi - 2), pt_a))
    acc_ref[...] += values(jnp.where(first, 0, 2 * qi - 1), pt_b)
    return acc_ref[:HEAD_V, :] / acc_ref[HEAD_V:HEAD_V + 1, :]


def _mla_kernel(qn_ref, qr_ref, kn_ref, kr_ref, vt_ref, o_ref, *scratch, tq):
    tk = tq // 2
    row = lax.broadcasted_iota(jnp.int32, (LANES, tq), 0)
    zero = jnp.zeros((), BF16)

    def k_block(i):
        rows = pl.ds(pl.multiple_of(i * tk, tk), tk)
        return jnp.concatenate([kn_ref[rows, :], kr_ref[rows, :]], axis=1)

    def vt_block(i):
        return vt_ref[:, pl.ds(pl.multiple_of(i * tk, tk), tk)]

    def query_block(qi, carry):
        span = pl.ds(pl.multiple_of(qi * tq, tq), tq)
        qn, qr = qn_ref[:, span], qr_ref[:, span]
        q0 = jnp.concatenate([jnp.where(row < B_NOPE, qn, zero),
                              jnp.where(row < B_ROPE, qr, zero)], axis=0)
        q1 = jnp.concatenate([jnp.where(row >= B_NOPE, qn, zero),
                              jnp.where((row >= B_ROPE) & (row < 2 * B_ROPE), qr, zero)], axis=0)
        qst = jnp.concatenate([q0, q1], axis=1)
        ot = _flash_cols(qst, k_block, vt_block, scratch, qi=qi, tq=tq)
        pair_t = jnp.concatenate([ot[:, :tq], ot[:, tq:]], axis=0)
        o_ref[span, :] = pair_t.T.astype(o_ref.dtype)
        return carry

    lax.fori_loop(0, o_ref.shape[0] // tq, query_block, 0)


def _mla_attn(qnt, qrt, kn, kr, vt, *, tq):
    bsz, seq, width = kn.shape
    pairs = width // LANES
    tblk = pl.BlockSpec((None, LANES, seq), lambda b, p: (b, p, 0))
    kblk = pl.BlockSpec((None, seq, LANES), lambda b, p: (b, 0, p))
    krblk = pl.BlockSpec((None, seq, LANES), lambda b, p: (b, 0, 0))
    return pl.pallas_call(
        functools.partial(_mla_kernel, tq=tq),
        grid=(bsz, pairs),
        in_specs=[tblk, tblk, kblk, krblk, tblk],
        out_specs=kblk,
        out_shape=jax.ShapeDtypeStruct((bsz, seq, width), BF16),
        scratch_shapes=_flash_scratch(2 * tq, tq),
        compiler_params=pltpu.CompilerParams(
            dimension_semantics=("arbitrary",) * 2, vmem_limit_bytes=VMEM_LIMIT),
        name="mla_attn",
    )(qnt, qrt, kn, kr, vt)


def _diff_kernel(lam_ref, g_ref, q_ref, k_ref, vt_ref, o_ref, *scratch, tq, lam_init):
    tk = tq // 2
    lane = lax.broadcasted_iota(jnp.int32, (tq, LANES), 1)
    row = lax.broadcasted_iota(jnp.int32, (LANES, tq), 0)
    low = lane < C_V
    zero = jnp.zeros((), BF16)
    lp = lam_ref[...]
    lam = (jnp.exp(jnp.sum(lp[0:1] * lp[1:2], axis=1, keepdims=True))
           - jnp.exp(jnp.sum(lp[2:3] * lp[3:4], axis=1, keepdims=True)) + lam_init)

    def k_block(i):
        return k_ref[pl.ds(pl.multiple_of(i * tk, tk), tk), :]

    def vt_block(i):
        return vt_ref[:, pl.ds(pl.multiple_of(i * tk, tk), tk)]

    def query_block(qi, carry):
        span = pl.ds(pl.multiple_of(qi * tq, tq), tq)
        q = q_ref[:, span]
        qst = jnp.concatenate(
            [jnp.where((row >= j * C_QK) & (row < (j + 1) * C_QK), q, zero) for j in range(4)],
            axis=1)
        ot = _flash_cols(qst, k_block, vt_block, scratch, qi=qi, tq=tq)
        d0 = ot[:, 0:tq] - lam * ot[:, tq:2 * tq]
        d1 = ot[:, 2 * tq:3 * tq] - lam * ot[:, 3 * tq:]
        d = jnp.concatenate([d0, d1], axis=0).T
        sq = d * d
        ms_lo = jnp.sum(jnp.where(low, sq, 0.0), axis=1, keepdims=True) * (1.0 / C_V)
        ms_hi = jnp.sum(jnp.where(low, 0.0, sq), axis=1, keepdims=True) * (1.0 / C_V)
        rs = jnp.where(low, lax.rsqrt(ms_lo + SUBLN_EPS), lax.rsqrt(ms_hi + SUBLN_EPS))
        o_ref[span, :] = ((d * rs * g_ref[...]) * (1.0 - lam_init)).astype(o_ref.dtype)
        return carry

    lax.fori_loop(0, o_ref.shape[0] // tq, query_block, 0)


def _diff_attn(lam_params, g2, qt, k, vt, *, tq, lam_init):
    bsz, seq, width = k.shape
    pairs = width // LANES
    tblk = pl.BlockSpec((None, LANES, seq), lambda b, p: (b, p, 0))
    kblk = pl.BlockSpec((None, seq, LANES), lambda b, p: (b, 0, p))
    const = lambda b, p: (0, 0)
    return pl.pallas_call(
        functools.partial(_diff_kernel, tq=tq, lam_init=lam_init),
        grid=(bsz, pairs),
        in_specs=[pl.BlockSpec(lam_params.shape, const), pl.BlockSpec(g2.shape, const),
                  tblk, kblk, tblk],
        out_specs=kblk,
        out_shape=jax.ShapeDtypeStruct((bsz, seq, width), BF16),
        scratch_shapes=_flash_scratch(4 * tq, tq),
        compiler_params=pltpu.CompilerParams(
            dimension_semantics=("arbitrary",) * 2, vmem_limit_bytes=VMEM_LIMIT),
        name="diff_attn",
    )(lam_params, g2, qt, k, vt)


def _post_kernel(x_ref, oa_ref, ob_ref, oc_ref, wo_ref, g2_ref, wup_ref, wdn_ref, gf_ref,
                 y_ref, *, final):
    mixed = jnp.concatenate([oa_ref[...], ob_ref[...], oc_ref[...]], axis=1)
    x1 = x_ref[...] + jnp.dot(mixed, wo_ref[...], preferred_element_type=F32)
    hn = _rms(x1, g2_ref[...], NORM_EPS).astype(BF16)
    y_ref[...] = x1
    for c in range(wup_ref.shape[1] // D_FF_CHUNK):
        cs = slice(c * D_FF_CHUNK, (c + 1) * D_FF_CHUNK)
        u = jnp.dot(hn, wup_ref[:, cs], preferred_element_type=F32)
        a = jnp.square(jnp.maximum(u, 0.0)).astype(BF16)
        y_ref[...] += jnp.dot(a, wdn_ref[cs, :], preferred_element_type=F32)
    if final:
        y_ref[...] = _rms(y_ref[...], gf_ref[...], NORM_EPS)


def _post(x2, oa, ob, oc, wo, g2, wup, wdn, gf, *, layer, tm, final):
    n, d = x2.shape
    row = lambda i: (i, 0)
    const = lambda i: (0, 0)
    slab = lambda i: (layer, 0, 0)
    return pl.pallas_call(
        functools.partial(_post_kernel, final=final),
        grid=(n // tm,),
        in_specs=[pl.BlockSpec((tm, d), row),
                  pl.BlockSpec((tm, oa.shape[1]), row),
                  pl.BlockSpec((tm, ob.shape[1]), row),
                  pl.BlockSpec((tm, oc.shape[1]), row),
                  pl.BlockSpec((None,) + wo.shape[1:], slab), pl.BlockSpec(g2.shape, const),
                  pl.BlockSpec((None,) + wup.shape[1:], slab),
                  pl.BlockSpec((None,) + wdn.shape[1:], slab),
                  pl.BlockSpec(gf.shape, const)],
        out_specs=pl.BlockSpec((tm, d), row),
        out_shape=jax.ShapeDtypeStruct((n, d), F32),
        compiler_params=pltpu.CompilerParams(
            dimension_semantics=("arbitrary",), vmem_limit_bytes=VMEM_LIMIT),
        name="post_final" if final else "post",
    )(x2, oa, ob, oc, wo, g2, wup, wdn, gf)


def _rope_tables(seq):
    def tables(dim):
        half = dim // 2
        inv = 1.0 / (ROPE_THETA ** (jnp.arange(half, dtype=F32) / half))
        ang = jnp.arange(seq, dtype=F32)[:, None] * inv[None, :]
        cos, sin = jnp.cos(ang), jnp.sin(ang)
        reps = LANES // dim
        cos_t = jnp.tile(jnp.concatenate([cos, cos], axis=1), (1, reps))
        sin_t = jnp.tile(jnp.concatenate([-sin, sin], axis=1), (1, reps))
        return cos_t, sin_t
    cosa, sina = tables(A_DIM)
    cos32, sin32 = tables(C_QK)
    return cosa, sina, cos32, sin32


def _prep_weights(w_in, w_uq, w_ukv):
    depth = w_in.shape[0]
    cols = jnp.split(w_in, [384, 768, 1152, 1536, 1664, 1696, 1952, 2208], axis=2)
    qa, ka, va, cq, ckv, kr, qc, kc, vc = cols
    w1 = jnp.concatenate([qa, ka, va, cq, ckv, jnp.tile(kr, (1, 1, LANES // B_ROPE)), qc, kc, vc],
                         axis=2).astype(BF16)
    uq = w_uq.reshape(depth, B_Q_LORA, B_HEADS, B_NOPE + B_ROPE)
    nope = uq[..., :B_NOPE].reshape(depth, B_Q_LORA, B_HEADS * B_NOPE)
    rope = uq[..., B_NOPE:].reshape(depth, B_Q_LORA, B_HEADS // 2, 2 * B_ROPE)
    rope = jnp.pad(rope, ((0, 0), (0, 0), (0, 0), (0, LANES - 2 * B_ROPE)))
    wuq = jnp.concatenate([nope, rope.reshape(depth, B_Q_LORA, -1)], axis=2).astype(BF16)
    ukv = w_ukv.reshape(depth, B_KV_LORA, B_HEADS, B_NOPE + B_V)
    wukv = jnp.concatenate([ukv[..., :B_NOPE].reshape(depth, B_KV_LORA, -1),
                            ukv[..., B_NOPE:].reshape(depth, B_KV_LORA, -1)],
                           axis=2).astype(BF16)
    return w1, wuq, wukv


def _tiles(seq):
    dilated = max(window for window, _ in A_CONFIGS)
    causal = min(1024, seq)
    assert seq % dilated == 0 and seq % causal == 0
    return 1024, 512, causal, dilated


def kernel(x, ln1_g, w_in, mla_q_norm_g, mla_w_uq, mla_kv_norm_g, mla_w_ukv, diff_lambda,
           diff_subln_g, w_o, ln2_g, w_up, w_down, final_g):
    bsz, seq, d_model = x.shape
    depth = w_in.shape[0]
    tm_in, tm_post, tq, tile_a = _tiles(seq)
    tabs = _rope_tables(seq)
    x2 = x.reshape(bsz * seq, d_model)
    gf = final_g.reshape(1, d_model)
    sh = lambda t, w: t.reshape(bsz, seq, w)
    w1, wuq, wukv = _prep_weights(w_in, mla_w_uq, mla_w_ukv)
    wo, wup, wdn = w_o.astype(BF16), w_up.astype(BF16), w_down.astype(BF16)

    for layer in range(depth):
        lam_init = 0.8 - 0.6 * math.exp(-0.3 * layer)
        qa, ka, va, qn, qr, kn, vb, kr, qc, kc, vc = _inproj(
            x2, ln1_g[layer].reshape(1, -1), w1, tabs,
            mla_q_norm_g[layer].reshape(1, -1), wuq,
            mla_kv_norm_g[layer].reshape(1, -1), wukv, layer=layer, seq=seq, tm=tm_in)

        out_a = _dilated_mixer(qa, ka, va, seq=seq, tile=tile_a)
        out_b = _mla_attn(qn, qr, sh(kn, B_WIDTH), sh(kr, LANES), vb, tq=tq)
        g_sub = jnp.tile(diff_subln_g[layer], LANES // C_V).reshape(1, LANES)
        out_c = _diff_attn(diff_lambda[layer], g_sub, qc, sh(kc, C_WIDTH), vc,
                           tq=tq, lam_init=lam_init)

        x2 = _post(x2, out_a, out_b.reshape(-1, B_WIDTH), out_c.reshape(-1, C_WIDTH), wo,
                   ln2_g[layer].reshape(1, -1), wup, wdn, gf,
                   layer=layer, tm=tm_post, final=layer == depth - 1)

    return x2.reshape(bsz, seq, d_model)
```

```python
import functools
import math

import jax
import jax.numpy as jnp
from jax import lax
from jax.experimental import pallas as pl
from jax.experimental.pallas import tpu as pltpu

F32 = jnp.float32
BF16 = jnp.bfloat16

LANES = 128
HEAD_V = 64
ACC_ROWS = HEAD_V + 16
VMEM_LIMIT = 56 * 1024 * 1024

ROPE_THETA = 10000.0
NORM_EPS = 1e-6
SUBLN_EPS = 1e-5
NEG = -1e30
LOG2E = math.log2(math.e)

A_HEADS, A_DIM = 6, 64
A_CONFIGS = ((128, 1), (512, 4), (2048, 16))
A_SPAN = 128
B_HEADS, B_NOPE, B_ROPE, B_V = 6, 64, 32, 64
B_Q_LORA, B_KV_LORA = 384, 128
C_HEADS, C_QK = 4, 32
C_V = 2 * C_QK
A_WIDTH = A_HEADS * A_DIM
B_WIDTH = B_HEADS * B_V
C_WIDTH = C_HEADS * C_V
D_FF_CHUNK = 1024
INPROJ_SUB = 512


def _rms(x, g, eps):
    return x * lax.rsqrt(jnp.mean(x * x, axis=-1, keepdims=True) + eps) * g


def _rope(x, cos, sin_signed, half):
    lane = lax.broadcasted_iota(jnp.int32, (x.shape[0], LANES), 1)
    first = (lane & (2 * half - 1)) < half
    outs = []
    for c in range(x.shape[1] // LANES):
        xc = x[:, c * LANES:(c + 1) * LANES]
        partner = jnp.where(first, pltpu.roll(xc, LANES - half, 1), pltpu.roll(xc, half, 1))
        outs.append(xc * cos + partner * sin_signed)
    return outs[0] if len(outs) == 1 else jnp.concatenate(outs, axis=1)


def _nt_dot(a, b):
    return lax.dot_general(a, b, (((1,), (1,)), ((), ())), preferred_element_type=F32)


def _inproj_kernel(x_ref, g_ref, w1_ref, cosa_ref, sina_ref, cos32_ref, sin32_ref,
                   qg_ref, wuq_ref, kvg_ref, wukv_ref,
                   qa_ref, ka_ref, va_ref, qn_ref, qr_ref, kn_ref, vb_ref, kr_ref,
                   qc_ref, kc_ref, vc_ref):
    scale_a = A_DIM ** -0.5 * LOG2E
    scale_b = (B_NOPE + B_ROPE) ** -0.5 * LOG2E
    scale_c = C_QK ** -0.5 * LOG2E

    for r0 in range(0, x_ref.shape[0], INPROJ_SUB):
        rows = slice(r0, r0 + INPROJ_SUB)
        h = _rms(x_ref[rows, :], g_ref[...], NORM_EPS).astype(BF16)
        p_all = jnp.dot(h, w1_ref[...], preferred_element_type=F32)

        def proj(lo, width):
            return p_all[:, lo:lo + width]

        cosa, sina = cosa_ref[rows, :], sina_ref[rows, :]
        cos32, sin32 = cos32_ref[rows, :], sin32_ref[rows, :]

        qa_ref[rows, :] = _rope(proj(0, 384), cosa, sina, 32) * scale_a
        ka_ref[rows, :] = _rope(proj(384, 384), cosa, sina, 32)
        va_ref[rows, :] = proj(768, 384)

        cq = _rms(proj(1152, 384), qg_ref[...], NORM_EPS).astype(BF16)
        qf = jnp.dot(cq, wuq_ref[...], preferred_element_type=F32)
        qn_ref[:, rows] = (qf[:, :384] * scale_b).T.astype(BF16)
        qr_ref[:, rows] = (_rope(qf[:, 384:], cos32, sin32, 16) * scale_b).T.astype(BF16)

        ckv = _rms(proj(1536, 128), kvg_ref[...], NORM_EPS).astype(BF16)
        kvf = jnp.dot(ckv, wukv_ref[...], preferred_element_type=F32)
        kn_ref[rows, :] = kvf[:, :384].astype(BF16)
        vb_ref[:, rows] = kvf[:, 384:].T.astype(BF16)
        kr_ref[rows, :] = _rope(proj(1664, 128), cos32, sin32, 16).astype(BF16)

        qc_ref[:, rows] = (_rope(proj(1792, 256), cos32, sin32, 16) * scale_c).T.astype(BF16)
        kc_ref[rows, :] = _rope(proj(2048, 256), cos32, sin32, 16).astype(BF16)
        vc_ref[:, rows] = proj(2304, 256).T.astype(BF16)


def _inproj(x2, g, w1, tabs, qg, wuq, kvg, wukv, *, layer, seq, tm):
    n = x2.shape[0]
    nt = n // tm
    st = seq // tm
    row = lambda i: (i, 0)
    const = lambda i: (0, 0)
    slab = lambda i: (layer, 0, 0)
    tab = lambda i: (i % st, 0)
    widths = (384, 384, 384, 384, 384, 384, 384, 128, 256, 256, 256)
    major = (False, False, False, True, True, False, True, False, True, False, True)
    bsz = n // seq
    col = lambda i: (i // st, 0, i % st)
    out_specs = [pl.BlockSpec((None, w, tm), col) if t else pl.BlockSpec((tm, w), row)
                 for w, t in zip(widths, major)]
    dtypes = (F32,) * 3 + (BF16,) * 8
    out_shape = [jax.ShapeDtypeStruct((bsz, w, seq) if t else (n, w), dt)
                 for w, t, dt in zip(widths, major, dtypes)]
    return pl.pallas_call(
        _inproj_kernel,
        grid=(nt,),
        in_specs=[
            pl.BlockSpec((tm, x2.shape[1]), row),
            pl.BlockSpec(g.shape, const),
            pl.BlockSpec((None,) + w1.shape[1:], slab),
            pl.BlockSpec((tm, LANES), tab), pl.BlockSpec((tm, LANES), tab),
            pl.BlockSpec((tm, LANES), tab), pl.BlockSpec((tm, LANES), tab),
            pl.BlockSpec(qg.shape, const), pl.BlockSpec((None,) + wuq.shape[1:], slab),
            pl.BlockSpec(kvg.shape, const), pl.BlockSpec((None,) + wukv.shape[1:], slab),
        ],
        out_specs=out_specs,
        out_shape=out_shape,
        compiler_params=pltpu.CompilerParams(
            dimension_semantics=("arbitrary",), vmem_limit_bytes=VMEM_LIMIT),
        name="inproj",
    )(x2, g, w1, *tabs, qg, wuq, kvg, wukv)


def _dilated_kernel(q_ref, k_ref, kp_ref, v_ref, vp_ref, o_ref,
                    q4_ref, k4_ref, kp4_ref, v4_ref, vp4_ref,
                    n4_ref, d4_ref, m4_ref, nt_ref, dt_ref, mt_ref, *, tile, nt):
    quarter = tile // 4
    t = pl.program_id(0) % nt
    lane = lax.broadcasted_iota(jnp.int32, (A_SPAN, LANES), 1)
    low = lane < A_DIM
    qi = lax.broadcasted_iota(jnp.int32, (2 * A_SPAN, 2 * A_SPAN), 0) & (A_SPAN - 1)
    ki = lax.broadcasted_iota(jnp.int32, (2 * A_SPAN, 2 * A_SPAN), 1)
    dist = qi + A_SPAN - ki
    band = (dist >= 0) & (dist <= A_SPAN)
    band0 = band & (ki >= jnp.where(t > 0, 0, A_SPAN))
    ones = jnp.ones((2 * A_SPAN, LANES), BF16)
    zero = jnp.zeros((), BF16)

    for src, dst in ((q_ref, q4_ref), (k_ref, k4_ref), (kp_ref, kp4_ref),
                     (v_ref, v4_ref), (vp_ref, vp4_ref)):
        for c in range(4):
            dst[c * quarter:(c + 1) * quarter, :] = src[pl.ds(c, quarter, stride=4), :]

    def attend(qg, k_prev, k_cur, v_prev, v_cur, valid):
        qg = qg.astype(BF16)
        qs = jnp.concatenate([jnp.where(low, qg, zero), jnp.where(low, zero, qg)], axis=0)
        kk = jnp.concatenate([k_prev, k_cur], axis=0).astype(BF16)
        vv = jnp.concatenate([v_prev, v_cur], axis=0).astype(BF16)
        s = jnp.where(valid, _nt_dot(qs, kk), NEG)
        m = jnp.max(s, axis=1, keepdims=True)
        p = jnp.exp2(s - m).astype(BF16)
        pv = jnp.dot(p, jnp.concatenate([vv, ones], axis=1), preferred_element_type=F32)
        num = jnp.where(low, pv[:A_SPAN, :LANES], pv[A_SPAN:, :LANES])
        den = jnp.where(low, pv[:A_SPAN, LANES:], pv[A_SPAN:, LANES:])
        return num, den, jnp.where(low, m[:A_SPAN], m[A_SPAN:])

    def merge(a, b):
        (num_a, den_a, m_a), (num_b, den_b, m_b) = a, b
        mx = jnp.maximum(m_a, m_b)
        wa = jnp.exp2(m_a - mx)
        wb = jnp.exp2(m_b - mx)
        return wa * num_a + wb * num_b, wa * den_a + wb * den_b, mx

    def load(refs, rows):
        return tuple(r[rows, :] for r in refs)

    def store(refs, rows, state):
        for r, x in zip(refs, state):
            r[rows, :] = x

    state4 = (n4_ref, d4_ref, m4_ref)
    state_t = (nt_ref, dt_ref, mt_ref)

    for c in range(4):
        for g in range(quarter // A_SPAN):
            cur = pl.ds(c * quarter + g * A_SPAN, A_SPAN)
            if g == 0:
                prev = pl.ds((c + 1) * quarter - A_SPAN, A_SPAN)
                k_prev, v_prev, valid = kp4_ref[prev, :], vp4_ref[prev, :], band0
            else:
                prev = pl.ds(c * quarter + (g - 1) * A_SPAN, A_SPAN)
                k_prev, v_prev, valid = k4_ref[prev, :], v4_ref[prev, :], band
            store(state4, cur,
                  attend(q4_ref[cur, :], k_prev, k4_ref[cur, :], v_prev, v4_ref[cur, :], valid))

    assert tile == A_SPAN * 16
    for c in range(4):
        for a in range(4):
            cur = pl.ds(c * quarter + a, A_SPAN, stride=4)
            new = attend(q4_ref[cur, :], kp4_ref[cur, :], k4_ref[cur, :],
                         vp4_ref[cur, :], v4_ref[cur, :], band0)
            store(state4, cur, merge(new, load(state4, cur)))

    for c in range(4):
        for r4, rt in zip(state4, state_t):
            rt[pl.ds(c, quarter, stride=4), :] = r4[c * quarter:(c + 1) * quarter, :]

    for g in range(tile // A_SPAN):
        cur = pl.ds(g * A_SPAN, A_SPAN)
        if g == 0:
            prev = pl.ds(tile - A_SPAN, A_SPAN)
            k_prev, v_prev, valid = kp_ref[prev, :], vp_ref[prev, :], band0
        else:
            prev = pl.ds((g - 1) * A_SPAN, A_SPAN)
            k_prev, v_prev, valid = k_ref[prev, :], v_ref[prev, :], band
        new = attend(q_ref[cur, :], k_prev, k_ref[cur, :], v_prev, v_ref[cur, :], valid)
        num, den, _ = merge(new, load(state_t, cur))
        o_ref[cur, :] = (num / den).astype(o_ref.dtype)


def _dilated_mixer(q, k, v, *, seq, tile):
    n, width = q.shape
    nt = seq // tile
    for window, dilation in A_CONFIGS:
        assert window // dilation == A_SPAN and tile % window == 0
    cur = lambda i, p: (i, p)
    prev = lambda i, p: (jnp.where(i % nt == 0, i, i - 1), p)
    blk = pl.BlockSpec((tile, LANES), cur)
    pblk = pl.BlockSpec((tile, LANES), prev)
    return pl.pallas_call(
        functools.partial(_dilated_kernel, tile=tile, nt=nt),
        grid=(n // tile, width // LANES),
        in_specs=[blk, blk, pblk, blk, pblk],
        out_specs=blk,
        out_shape=jax.ShapeDtypeStruct((n, width), BF16),
        scratch_shapes=[pltpu.VMEM((tile, LANES), F32)] * 11,
        compiler_params=pltpu.CompilerParams(
            dimension_semantics=("arbitrary",) * 2, vmem_limit_bytes=VMEM_LIMIT),
        name="dilated",
    )(q, k, k, v, v)


def _flash_scratch(rows, tq):
    tk = tq // 2
    score_bufs = [pltpu.VMEM((tk, rows), F32), pltpu.VMEM((1, rows), F32),
                  pltpu.VMEM((tk, rows), BF16)]
    return [pltpu.VMEM((1, rows), F32), pltpu.VMEM((ACC_ROWS, rows), F32)] + 2 * score_bufs


def _flash_cols(qst, k_block, vt_block, scratch, *, qi, tq):
    m_ref, acc_ref, st_a, cm_a, pt_a, st_b, cm_b, pt_b = scratch
    tk = tq // 2
    segs = range(qst.shape[1] // tq)
    ones = jnp.ones((ACC_ROWS - HEAD_V, tk), BF16)
    tri = (lax.broadcasted_iota(jnp.int32, (tk, tk), 0)
           <= lax.broadcasted_iota(jnp.int32, (tk, tk), 1))

    def scores(blk, st_ref, cm_ref):
        st = jnp.dot(k_block(blk), qst, preferred_element_type=F32)
        st_ref[...] = st
        cm_ref[...] = jnp.max(st, axis=0, keepdims=True)

    def softmax(st_ref, cm_ref, pt_ref):
        m_prev = m_ref[...]
        m_new = jnp.maximum(m_prev, cm_ref[...])
        m_ref[...] = m_new
        pt_ref[...] = jnp.exp2(st_ref[...] - m_new).astype(BF16)
        return jnp.exp2(m_prev - m_new)

    def values(blk, pt_ref):
        vt = vt_block(blk)
        half = qst.shape[1] // 2
        outs = []
        for h in range(2):
            lhs = jnp.concatenate([vt[h * HEAD_V:(h + 1) * HEAD_V], ones], axis=0)
            outs.append(jnp.dot(lhs, pt_ref[:, h * half:(h + 1) * half],
                                preferred_element_type=F32))
        return jnp.concatenate(outs, axis=1)

    diag0 = 2 * qi
    q_late = jnp.concatenate([qst[:, s * tq + tk:(s + 1) * tq] for s in segs], axis=1)
    s_late = jnp.dot(k_block(diag0 + 1), q_late, preferred_element_type=F32)
    for s in segs:
        early = slice(s * tq, s * tq + tk)
        late = slice(s * tq + tk, (s + 1) * tq)
        st = jnp.where(tri, s_late[:, s * tk:(s + 1) * tk], NEG)
        cm = jnp.max(st, axis=0, keepdims=True)
        m_ref[:, early] = jnp.full((1, tk), NEG, F32)
        m_ref[:, late] = cm
        pt_a[:, early] = jnp.zeros((tk, tk), BF16)
        pt_a[:, late] = jnp.exp2(st - cm).astype(BF16)
    s_full = jnp.dot(k_block(diag0), qst, preferred_element_type=F32)
    st = jnp.concatenate(
        [x for s in segs for x in (jnp.where(tri, s_full[:, s * tq:s * tq + tk], NEG),
                                   s_full[:, s * tq + tk:(s + 1) * tq])], axis=1)
    st_b[...] = st
    cm_b[...] = jnp.max(st, axis=0, keepdims=True)
    acc_ref[...] = jnp.zeros(acc_ref.shape, F32)

    def body(j, carry):
        first = j == 0
        alpha = softmax(st_b, cm_b, pt_b)
        scores(2 * j, st_a, cm_a)
        acc_ref[...] = alpha * (acc_ref[...] + values(jnp.where(first, diag0 + 1, 2 * j - 2), pt_a))
        alpha = softmax(st_a, cm_a, pt_a)
        scores(2 * j + 1, st_b, cm_b)
        acc_ref[...] = alpha * (acc_ref[...] + values(jnp.where(first, diag0, 2 * j - 1), pt_b))
        return carry

    lax.fori_loop(0, qi, body, 0)

    first = qi == 0
    alpha = softmax(st_b, cm_b, pt_b)
    acc_ref[...] = alpha * (acc_ref[...] + values(jnp.where(first, 1, 2 * qi - 2), pt_a))
    acc_ref[...] += values(jnp.where(first, 0, 2 * qi - 1), pt_b)
    return acc_ref[:HEAD_V, :] / acc_ref[HEAD_V:HEAD_V + 1, :]


def _mla_kernel(qn_ref, qr_ref, kn_ref, kr_ref, vt_ref, o_ref, *scratch, tq):
    tk = tq // 2
    row = lax.broadcasted_iota(jnp.int32, (LANES, tq), 0)
    zero = jnp.zeros((), BF16)

    def k_block(i):
        rows = pl.ds(pl.multiple_of(i * tk, tk), tk)
        return jnp.concatenate([kn_ref[rows, :], kr_ref[rows, :]], axis=1)

    def vt_block(i):
        return vt_ref[:, pl.ds(pl.multiple_of(i * tk, tk), tk)]

    def query_block(qi, carry):
        span = pl.ds(pl.multiple_of(qi * tq, tq), tq)
        qn, qr = qn_ref[:, span], qr_ref[:, span]
        q0 = jnp.concatenate([jnp.where(row < B_NOPE, qn, zero),
                              jnp.where(row < B_ROPE, qr, zero)], axis=0)
        q1 = jnp.concatenate([jnp.where(row >= B_NOPE, qn, zero),
                              jnp.where((row >= B_ROPE) & (row < 2 * B_ROPE), qr, zero)], axis=0)
        qst = jnp.concatenate([q0, q1], axis=1)
        ot = _flash_cols(qst, k_block, vt_block, scratch, qi=qi, tq=tq)
        pair_t = jnp.concatenate([ot[:, :tq], ot[:, tq:]], axis=0)
        o_ref[span, :] = pair_t.T.astype(o_ref.dtype)
        return carry

    lax.fori_loop(0, o_ref.shape[0] // tq, query_block, 0)


def _mla_attn(qnt, qrt, kn, kr, vt, *, tq):
    bsz, seq, width = kn.shape
    pairs = width // LANES
    tblk = pl.BlockSpec((None, LANES, seq), lambda b, p: (b, p, 0))
    kblk = pl.BlockSpec((None, seq, LANES), lambda b, p: (b, 0, p))
    krblk = pl.BlockSpec((None, seq, LANES), lambda b, p: (b, 0, 0))
    return pl.pallas_call(
        functools.partial(_mla_kernel, tq=tq),
        grid=(bsz, pairs),
        in_specs=[tblk, tblk, kblk, krblk, tblk],
        out_specs=kblk,
        out_shape=jax.ShapeDtypeStruct((bsz, seq, width), BF16),
        scratch_shapes=_flash_scratch(2 * tq, tq),
        compiler_params=pltpu.CompilerParams(
            dimension_semantics=("arbitrary",) * 2, vmem_limit_bytes=VMEM_LIMIT),
        name="mla_attn",
    )(qnt, qrt, kn, kr, vt)


def _diff_kernel(lam_ref, g_ref, q_ref, k_ref, vt_ref, o_ref, *scratch, tq, lam_init):
    tk = tq // 2
    lane = lax.broadcasted_iota(jnp.int32, (tq, LANES), 1)
    row = lax.broadcasted_iota(jnp.int32, (LANES, tq), 0)
    low = lane < C_V
    zero = jnp.zeros((), BF16)
    lp = lam_ref[...]
    lam = (jnp.exp(jnp.sum(lp[0:1] * lp[1:2], axis=1, keepdims=True))
           - jnp.exp(jnp.sum(lp[2:3] * lp[3:4], axis=1, keepdims=True)) + lam_init)

    def k_block(i):
        return k_ref[pl.ds(pl.multiple_of(i * tk, tk), tk), :]

    def vt_block(i):
        return vt_ref[:, pl.ds(pl.multiple_of(i * tk, tk), tk)]

    def query_block(qi, carry):
        span = pl.ds(pl.multiple_of(qi * tq, tq), tq)
        q = q_ref[:, span]
        qst = jnp.concatenate(
            [jnp.where((row >= j * C_QK) & (row < (j + 1) * C_QK), q, zero) for j in range(4)],
            axis=1)
        ot = _flash_cols(qst, k_block, vt_block, scratch, qi=qi, tq=tq)
        d0 = ot[:, 0:tq] - lam * ot[:, tq:2 * tq]
        d1 = ot[:, 2 * tq:3 * tq] - lam * ot[:, 3 * tq:]
        d = jnp.concatenate([d0, d1], axis=0).T
        sq = d * d
        ms_lo = jnp.sum(jnp.where(low, sq, 0.0), axis=1, keepdims=True) * (1.0 / C_V)
        ms_hi = jnp.sum(jnp.where(low, 0.0, sq), axis=1, keepdims=True) * (1.0 / C_V)
        rs = jnp.where(low, lax.rsqrt(ms_lo + SUBLN_EPS), lax.rsqrt(ms_hi + SUBLN_EPS))
        o_ref[span, :] = ((d * rs * g_ref[...]) * (1.0 - lam_init)).astype(o_ref.dtype)
        return carry

    lax.fori_loop(0, o_ref.shape[0] // tq, query_block, 0)


def _diff_attn(lam_params, g2, qt, k, vt, *, tq, lam_init):
    bsz, seq, width = k.shape
    pairs = width // LANES
    tblk = pl.BlockSpec((None, LANES, seq), lambda b, p: (b, p, 0))
    kblk = pl.BlockSpec((None, seq, LANES), lambda b, p: (b, 0, p))
    const = lambda b, p: (0, 0)
    return pl.pallas_call(
        functools.partial(_diff_kernel, tq=tq, lam_init=lam_init),
        grid=(bsz, pairs),
        in_specs=[pl.BlockSpec(lam_params.shape, const), pl.BlockSpec(g2.shape, const),
                  tblk, kblk, tblk],
        out_specs=kblk,
        out_shape=jax.ShapeDtypeStruct((bsz, seq, width), BF16),
        scratch_shapes=_flash_scratch(4 * tq, tq),
        compiler_params=pltpu.CompilerParams(
            dimension_semantics=("arbitrary",) * 2, vmem_limit_bytes=VMEM_LIMIT),
        name="diff_attn",
    )(lam_params, g2, qt, k, vt)


def _post_kernel(x_ref, oa_ref, ob_ref, oc_ref, wo_ref, g2_ref, wup_ref, wdn_ref, gf_ref,
                 y_ref, *, final):
    mixed = jnp.concatenate([oa_ref[...], ob_ref[...], oc_ref[...]], axis=1)
    x1 = x_ref[...] + jnp.dot(mixed, wo_ref[...].astype(BF16), preferred_element_type=F32)
    hn = _rms(x1, g2_ref[...], NORM_EPS).astype(BF16)
    y_ref[...] = x1
    for c in range(wup_ref.shape[1] // D_FF_CHUNK):
        cs = slice(c * D_FF_CHUNK, (c + 1) * D_FF_CHUNK)
        u = jnp.dot(hn, wup_ref[:, cs].astype(BF16), preferred_element_type=F32)
        a = jnp.square(jnp.maximum(u, 0.0)).astype(BF16)
        y_ref[...] += jnp.dot(a, wdn_ref[cs, :].astype(BF16), preferred_element_type=F32)
    if final:
        y_ref[...] = _rms(y_ref[...], gf_ref[...], NORM_EPS)


def _post(x2, oa, ob, oc, wo, g2, wup, wdn, gf, *, layer, tm, final):
    n, d = x2.shape
    row = lambda i: (i, 0)
    const = lambda i: (0, 0)
    slab = lambda i: (layer, 0, 0)
    return pl.pallas_call(
        functools.partial(_post_kernel, final=final),
        grid=(n // tm,),
        in_specs=[pl.BlockSpec((tm, d), row),
                  pl.BlockSpec((tm, oa.shape[1]), row),
                  pl.BlockSpec((tm, ob.shape[1]), row),
                  pl.BlockSpec((tm, oc.shape[1]), row),
                  pl.BlockSpec((None,) + wo.shape[1:], slab), pl.BlockSpec(g2.shape, const),
                  pl.BlockSpec((None,) + wup.shape[1:], slab),
                  pl.BlockSpec((None,) + wdn.shape[1:], slab),
                  pl.BlockSpec(gf.shape, const)],
        out_specs=pl.BlockSpec((tm, d), row),
        out_shape=jax.ShapeDtypeStruct((n, d), F32),
        compiler_params=pltpu.CompilerParams(
            dimension_semantics=("arbitrary",), vmem_limit_bytes=VMEM_LIMIT),
        name="post_final" if final else "post",
    )(x2, oa, ob, oc, wo, g2, wup, wdn, gf)


def _rope_tables(seq):
    def tables(dim):
        half = dim // 2
        inv = 1.0 / (ROPE_THETA ** (jnp.arange(half, dtype=F32) / half))
        ang = jnp.arange(seq, dtype=F32)[:, None] * inv[None, :]
        cos, sin = jnp.cos(ang), jnp.sin(ang)
        reps = LANES // dim
        cos_t = jnp.tile(jnp.concatenate([cos, cos], axis=1), (1, reps))
        sin_t = jnp.tile(jnp.concatenate([-sin, sin], axis=1), (1, reps))
        return cos_t, sin_t
    cosa, sina = tables(A_DIM)
    cos32, sin32 = tables(C_QK)
    return cosa, sina, cos32, sin32


def _prep_weights(w_in, w_uq, w_ukv):
    depth = w_in.shape[0]
    cols = jnp.split(w_in, [384, 768, 1152, 1536, 1664, 1696, 1952, 2208], axis=2)
    qa, ka, va, cq, ckv, kr, qc, kc, vc = cols
    w1 = jnp.concatenate([qa, ka, va, cq, ckv, jnp.tile(kr, (1, 1, LANES // B_ROPE)), qc, kc, vc],
                         axis=2).astype(BF16)
    uq = w_uq.reshape(depth, B_Q_LORA, B_HEADS, B_NOPE + B_ROPE)
    nope = uq[..., :B_NOPE].reshape(depth, B_Q_LORA, B_HEADS * B_NOPE)
    rope = uq[..., B_NOPE:].reshape(depth, B_Q_LORA, B_HEADS // 2, 2 * B_ROPE)
    rope = jnp.pad(rope, ((0, 0), (0, 0), (0, 0), (0, LANES - 2 * B_ROPE)))
    wuq = jnp.concatenate([nope, rope.reshape(depth, B_Q_LORA, -1)], axis=2).astype(BF16)
    ukv = w_ukv.reshape(depth, B_KV_LORA, B_HEADS, B_NOPE + B_V)
    wukv = jnp.concatenate([ukv[..., :B_NOPE].reshape(depth, B_KV_LORA, -1),
                            ukv[..., B_NOPE:].reshape(depth, B_KV_LORA, -1)],
                           axis=2).astype(BF16)
    return w1, wuq, wukv


def _tiles(seq):
    dilated = max(window for window, _ in A_CONFIGS)
    causal = min(1024, seq)
    assert seq % dilated == 0 and seq % causal == 0
    return 1024, 512, causal, dilated


def kernel(x, ln1_g, w_in, mla_q_norm_g, mla_w_uq, mla_kv_norm_g, mla_w_ukv, diff_lambda,
           diff_subln_g, w_o, ln2_g, w_up, w_down, final_g):
    bsz, seq, d_model = x.shape
    depth = w_in.shape[0]
    tm_in, tm_post, tq, tile_a = _tiles(seq)
    tabs = _rope_tables(seq)
    x2 = x.reshape(bsz * seq, d_model)
    gf = final_g.reshape(1, d_model)
    sh = lambda t, w: t.reshape(bsz, seq, w)
    w1, wuq, wukv = _prep_weights(w_in, mla_w_uq, mla_w_ukv)
    wo, wup, wdn = w_o, w_up, w_down

    for layer in range(depth):
        lam_init = 0.8 - 0.6 * math.exp(-0.3 * layer)
        qa, ka, va, qn, qr, kn, vb, kr, qc, kc, vc = _inproj(
            x2, ln1_g[layer].reshape(1, -1), w1, tabs,
            mla_q_norm_g[layer].reshape(1, -1), wuq,
            mla_kv_norm_g[layer].reshape(1, -1), wukv, layer=layer, seq=seq, tm=tm_in)

        out_a = _dilated_mixer(qa, ka, va, seq=seq, tile=tile_a)
        out_b = _mla_attn(qn, qr, sh(kn, B_WIDTH), sh(kr, LANES), vb, tq=tq)
        g_sub = jnp.tile(diff_subln_g[layer], LANES // C_V).reshape(1, LANES)
        out_c = _diff_attn(diff_lambda[layer], g_sub, qc, sh(kc, C_WIDTH), vc,
                           tq=tq, lam_init=lam_init)

        x2 = _post(x2, out_a, out_b.reshape(-1, B_WIDTH), out_c.reshape(-1, C_WIDTH), wo,
                   ln2_g[layer].reshape(1, -1), wup, wdn, gf,
                   layer=layer, tm=tm_post, final=layer == depth - 1)

    return x2.reshape(bsz, seq, d_model)
```

```python
import functools
import math

import jax
import jax.numpy as jnp
from jax import lax
from jax.experimental import pallas as pl
from jax.experimental.pallas import tpu as pltpu

F32 = jnp.float32
BF16 = jnp.bfloat16

LANES = 128
HEAD_V = 64
ACC_ROWS = HEAD_V + 16
VMEM_LIMIT = 56 * 1024 * 1024

ROPE_THETA = 10000.0
NORM_EPS = 1e-6
SUBLN_EPS = 1e-5
NEG = -1e30
LOG2E = math.log2(math.e)

A_HEADS, A_DIM = 6, 64
A_CONFIGS = ((128, 1), (512, 4), (2048, 16))
A_SPAN = 128
B_HEADS, B_NOPE, B_ROPE, B_V = 6, 64, 32, 64
B_Q_LORA, B_KV_LORA = 384, 128
C_HEADS, C_QK = 4, 32
C_V = 2 * C_QK
A_WIDTH = A_HEADS * A_DIM
B_WIDTH = B_HEADS * B_V
C_WIDTH = C_HEADS * C_V
D_FF_CHUNK = 1024
INPROJ_SUB = 512


def _rms(x, g, eps):
    return x * lax.rsqrt(jnp.mean(x * x, axis=-1, keepdims=True) + eps) * g


def _rope(x, cos, sin_signed, half):
    lane = lax.broadcasted_iota(jnp.int32, (x.shape[0], LANES), 1)
    first = (lane & (2 * half - 1)) < half
    outs = []
    for c in range(x.shape[1] // LANES):
        xc = x[:, c * LANES:(c + 1) * LANES]
        partner = jnp.where(first, pltpu.roll(xc, LANES - half, 1), pltpu.roll(xc, half, 1))
        outs.append(xc * cos + partner * sin_signed)
    return outs[0] if len(outs) == 1 else jnp.concatenate(outs, axis=1)


def _nt_dot(a, b):
    return lax.dot_general(a, b, (((1,), (1,)), ((), ())), preferred_element_type=F32)


def _inproj_kernel(x_ref, g_ref, w1_ref, cosa_ref, sina_ref, cos32_ref, sin32_ref,
                   qg_ref, wuq_ref, kvg_ref, wukv_ref,
                   qa_ref, ka_ref, va_ref, qn_ref, qr_ref, kn_ref, vb_ref, kr_ref,
                   qc_ref, kc_ref, vc_ref):
    scale_a = A_DIM ** -0.5 * LOG2E
    scale_b = (B_NOPE + B_ROPE) ** -0.5 * LOG2E
    scale_c = C_QK ** -0.5 * LOG2E

    for r0 in range(0, x_ref.shape[0], INPROJ_SUB):
        rows = slice(r0, r0 + INPROJ_SUB)
        h = _rms(x_ref[rows, :], g_ref[...], NORM_EPS).astype(BF16)
        p_all = jnp.dot(h, w1_ref[...], preferred_element_type=F32)

        def proj(lo, width):
            return p_all[:, lo:lo + width]

        cosa, sina = cosa_ref[rows, :], sina_ref[rows, :]
        cos32, sin32 = cos32_ref[rows, :], sin32_ref[rows, :]

        qa_ref[rows, :] = _rope(proj(0, 384), cosa, sina, 32) * scale_a
        ka_ref[rows, :] = _rope(proj(384, 384), cosa, sina, 32)
        va_ref[rows, :] = proj(768, 384)

        cq = _rms(proj(1152, 384), qg_ref[...], NORM_EPS).astype(BF16)
        qf = jnp.dot(cq, wuq_ref[...], preferred_element_type=F32)
        qn_ref[:, rows] = (qf[:, :384] * scale_b).T.astype(BF16)
        qr_ref[:, rows] = (_rope(qf[:, 384:], cos32, sin32, 16) * scale_b).T.astype(BF16)

        ckv = _rms(proj(1536, 128), kvg_ref[...], NORM_EPS).astype(BF16)
        kvf = jnp.dot(ckv, wukv_ref[...], preferred_element_type=F32)
        kn_ref[rows, :] = kvf[:, :384].astype(BF16)
        vb_ref[:, rows] = kvf[:, 384:].T.astype(BF16)
        kr_ref[rows, :] = _rope(proj(1664, 128), cos32, sin32, 16).astype(BF16)

        qc_ref[:, rows] = (_rope(proj(1792, 256), cos32, sin32, 16) * scale_c).T.astype(BF16)
        kc_ref[rows, :] = _rope(proj(2048, 256), cos32, sin32, 16).astype(BF16)
        vc_ref[:, rows] = proj(2304, 256).T.astype(BF16)


def _inproj(x2, g, w1, tabs, qg, wuq, kvg, wukv, *, layer, seq, tm):
    n = x2.shape[0]
    nt = n // tm
    st = seq // tm
    row = lambda i: (i, 0)
    const = lambda i: (0, 0)
    slab = lambda i: (layer, 0, 0)
    tab = lambda i: (i % st, 0)
    widths = (384, 384, 384, 384, 384, 384, 384, 128, 256, 256, 256)
    major = (False, False, False, True, True, False, True, False, True, False, True)
    bsz = n // seq
    col = lambda i: (i // st, 0, i % st)
    out_specs = [pl.BlockSpec((None, w, tm), col) if t else pl.BlockSpec((tm, w), row)
                 for w, t in zip(widths, major)]
    dtypes = (F32,) * 3 + (BF16,) * 8
    out_shape = [jax.ShapeDtypeStruct((bsz, w, seq) if t else (n, w), dt)
                 for w, t, dt in zip(widths, major, dtypes)]
    return pl.pallas_call(
        _inproj_kernel,
        grid=(nt,),
        in_specs=[
            pl.BlockSpec((tm, x2.shape[1]), row),
            pl.BlockSpec(g.shape, const),
            pl.BlockSpec((None,) + w1.shape[1:], slab),
            pl.BlockSpec((tm, LANES), tab), pl.BlockSpec((tm, LANES), tab),
            pl.BlockSpec((tm, LANES), tab), pl.BlockSpec((tm, LANES), tab),
            pl.BlockSpec(qg.shape, const), pl.BlockSpec((None,) + wuq.shape[1:], slab),
            pl.BlockSpec(kvg.shape, const), pl.BlockSpec((None,) + wukv.shape[1:], slab),
        ],
        out_specs=out_specs,
        out_shape=out_shape,
        compiler_params=pltpu.CompilerParams(
            dimension_semantics=("arbitrary",), vmem_limit_bytes=VMEM_LIMIT),
        name="inproj",
    )(x2, g, w1, *tabs, qg, wuq, kvg, wukv)


def _dilated_kernel(q_ref, k_ref, kp_ref, v_ref, vp_ref, o_ref,
                    q4_ref, k4_ref, kp4_ref, v4_ref, vp4_ref,
                    n4_ref, d4_ref, m4_ref, nt_ref, dt_ref, mt_ref, *, tile, nt):
    quarter = tile // 4
    t = pl.program_id(0) % nt
    lane = lax.broadcasted_iota(jnp.int32, (A_SPAN, LANES), 1)
    low = lane < A_DIM
    qi = lax.broadcasted_iota(jnp.int32, (2 * A_SPAN, 2 * A_SPAN), 0) & (A_SPAN - 1)
    ki = lax.broadcasted_iota(jnp.int32, (2 * A_SPAN, 2 * A_SPAN), 1)
    dist = qi + A_SPAN - ki
    band = (dist >= 0) & (dist <= A_SPAN)
    band0 = band & (ki >= jnp.where(t > 0, 0, A_SPAN))
    ones = jnp.ones((2 * A_SPAN, LANES), BF16)
    zero = jnp.zeros((), BF16)

    for src, dst in ((q_ref, q4_ref), (k_ref, k4_ref), (kp_ref, kp4_ref),
                     (v_ref, v4_ref), (vp_ref, vp4_ref)):
        for c in range(4):
            dst[c * quarter:(c + 1) * quarter, :] = src[pl.ds(c, quarter, stride=4), :]

    def attend(qg, k_prev, k_cur, v_prev, v_cur, valid):
        qg = qg.astype(BF16)
        qs = jnp.concatenate([jnp.where(low, qg, zero), jnp.where(low, zero, qg)], axis=0)
        kk = jnp.concatenate([k_prev, k_cur], axis=0).astype(BF16)
        vv = jnp.concatenate([v_prev, v_cur], axis=0).astype(BF16)
        s = jnp.where(valid, _nt_dot(qs, kk), NEG)
        m = jnp.max(s, axis=1, keepdims=True)
        p = jnp.exp2(s - m).astype(BF16)
        pv = jnp.dot(p, jnp.concatenate([vv, ones], axis=1), preferred_element_type=F32)
        num = jnp.where(low, pv[:A_SPAN, :LANES], pv[A_SPAN:, :LANES])
        den = jnp.where(low, pv[:A_SPAN, LANES:], pv[A_SPAN:, LANES:])
        return num, den, jnp.where(low, m[:A_SPAN], m[A_SPAN:])

    def merge(a, b):
        (num_a, den_a, m_a), (num_b, den_b, m_b) = a, b
        mx = jnp.maximum(m_a, m_b)
        wa = jnp.exp2(m_a - mx)
        wb = jnp.exp2(m_b - mx)
        return wa * num_a + wb * num_b, wa * den_a + wb * den_b, mx

    def load(refs, rows):
        return tuple(r[rows, :] for r in refs)

    def store(refs, rows, state):
        for r, x in zip(refs, state):
            r[rows, :] = x

    state4 = (n4_ref, d4_ref, m4_ref)
    state_t = (nt_ref, dt_ref, mt_ref)

    for c in range(4):
        for g in range(quarter // A_SPAN):
            cur = pl.ds(c * quarter + g * A_SPAN, A_SPAN)
            if g == 0:
                prev = pl.ds((c + 1) * quarter - A_SPAN, A_SPAN)
                k_prev, v_prev, valid = kp4_ref[prev, :], vp4_ref[prev, :], band0
            else:
                prev = pl.ds(c * quarter + (g - 1) * A_SPAN, A_SPAN)
                k_prev, v_prev, valid = k4_ref[prev, :], v4_ref[prev, :], band
            store(state4, cur,
                  attend(q4_ref[cur, :], k_prev, k4_ref[cur, :], v_prev, v4_ref[cur, :], valid))

    assert tile == A_SPAN * 16
    for c in range(4):
        for a in range(4):
            cur = pl.ds(c * quarter + a, A_SPAN, stride=4)
            new = attend(q4_ref[cur, :], kp4_ref[cur, :], k4_ref[cur, :],
                         vp4_ref[cur, :], v4_ref[cur, :], band0)
            store(state4, cur, merge(new, load(state4, cur)))

    for c in range(4):
        for r4, rt in zip(state4, state_t):
            rt[pl.ds(c, quarter, stride=4), :] = r4[c * quarter:(c + 1) * quarter, :]

    for g in range(tile // A_SPAN):
        cur = pl.ds(g * A_SPAN, A_SPAN)
        if g == 0:
            prev = pl.ds(tile - A_SPAN, A_SPAN)
            k_prev, v_prev, valid = kp_ref[prev, :], vp_ref[prev, :], band0
        else:
            prev = pl.ds((g - 1) * A_SPAN, A_SPAN)
            k_prev, v_prev, valid = k_ref[prev, :], v_ref[prev, :], band
        new = attend(q_ref[cur, :], k_prev, k_ref[cur, :], v_prev, v_ref[cur, :], valid)
        num, den, _ = merge(new, load(state_t, cur))
        o_ref[cur, :] = (num / den).astype(o_ref.dtype)


def _dilated_mixer(q, k, v, *, seq, tile):
    n, width = q.shape
    nt = seq // tile
    for window, dilation in A_CONFIGS:
        assert window // dilation == A_SPAN and tile % window == 0
    cur = lambda i, p: (i, p)
    prev = lambda i, p: (jnp.where(i % nt == 0, i, i - 1), p)
    blk = pl.BlockSpec((tile, LANES), cur)
    pblk = pl.BlockSpec((tile, LANES), prev)
    return pl.pallas_call(
        functools.partial(_dilated_kernel, tile=tile, nt=nt),
        grid=(n // tile, width // LANES),
        in_specs=[blk, blk, pblk, blk, pblk],
        out_specs=blk,
        out_shape=jax.ShapeDtypeStruct((n, width), BF16),
        scratch_shapes=[pltpu.VMEM((tile, LANES), F32)] * 11,
        compiler_params=pltpu.CompilerParams(
            dimension_semantics=("arbitrary",) * 2, vmem_limit_bytes=VMEM_LIMIT),
        name="dilated",
    )(q, k, k, v, v)


def _flash_scratch(rows, tq):
    tk = tq // 2
    score_bufs = [pltpu.VMEM((tk, rows), F32), pltpu.VMEM((1, rows), F32),
                  pltpu.VMEM((tk, rows), BF16)]
    return [pltpu.VMEM((1, rows), F32), pltpu.VMEM((ACC_ROWS, rows), F32)] + 2 * score_bufs


def _flash_cols(qst, k_block, vt_block, scratch, *, qi, tq, first_block):
    m_ref, acc_ref, st_a, cm_a, pt_a, st_b, cm_b, pt_b = scratch
    tk = tq // 2
    segs = range(qst.shape[1] // tq)
    ones = jnp.ones((ACC_ROWS - HEAD_V, tk), BF16)
    tri = (lax.broadcasted_iota(jnp.int32, (tk, tk), 0)
           <= lax.broadcasted_iota(jnp.int32, (tk, tk), 1))

    def scores(blk, st_ref, cm_ref):
        st = jnp.dot(k_block(blk), qst, preferred_element_type=F32)
        st_ref[...] = st
        cm_ref[...] = jnp.max(st, axis=0, keepdims=True)

    def softmax(st_ref, cm_ref, pt_ref):
        m_prev = m_ref[...]
        m_new = jnp.maximum(m_prev, cm_ref[...])
        m_ref[...] = m_new
        pt_ref[...] = jnp.exp2(st_ref[...] - m_new).astype(BF16)
        return jnp.exp2(m_prev - m_new)

    def values(blk, pt_ref):
        vt = vt_block(blk)
        half = qst.shape[1] // 2
        outs = []
        for h in range(2):
            lhs = jnp.concatenate([vt[h * HEAD_V:(h + 1) * HEAD_V], ones], axis=0)
            outs.append(jnp.dot(lhs, pt_ref[:, h * half:(h + 1) * half],
                                preferred_element_type=F32))
        return jnp.concatenate(outs, axis=1)

    diag0 = 2 * qi

    s_full = jnp.dot(k_block(diag0), qst, preferred_element_type=F32)
    st = jnp.concatenate(
        [x for s in segs for x in (jnp.where(tri, s_full[:, s * tq:s * tq + tk], NEG),
                                   s_full[:, s * tq + tk:(s + 1) * tq])], axis=1)
    cm = jnp.max(st, axis=0, keepdims=True)
    m_ref[...] = cm
    pt_a[...] = jnp.exp2(st - cm).astype(BF16)
    acc_ref[...] = jnp.zeros(acc_ref.shape, F32)

    def body(j, carry):
        alpha = softmax(st_b, cm_b, pt_b)
        scores(2 * j + 1, st_a, cm_a)
        acc_ref[...] = alpha * (acc_ref[...] + values(jnp.where(j == 0, diag0, 2 * j - 1), pt_a))
        alpha = softmax(st_a, cm_a, pt_a)
        scores(2 * j + 2, st_b, cm_b)
        acc_ref[...] = alpha * (acc_ref[...] + values(2 * j, pt_b))
        return carry

    if first_block:
        before_last = diag0
    else:
        scores(0, st_b, cm_b)
        lax.fori_loop(0, qi - 1, body, 0)
        last = qi - 1
        alpha = softmax(st_b, cm_b, pt_b)
        scores(2 * last + 1, st_a, cm_a)
        acc_ref[...] = alpha * (acc_ref[...]
                                + values(jnp.where(last == 0, diag0, 2 * last - 1), pt_a))
        alpha = softmax(st_a, cm_a, pt_a)
        acc_ref[...] = alpha * (acc_ref[...] + values(2 * last, pt_b))
        before_last = 2 * last + 1

    q_late = jnp.concatenate([qst[:, s * tq + tk:(s + 1) * tq] for s in segs], axis=1)
    s_late = jnp.dot(k_block(diag0 + 1), q_late, preferred_element_type=F32)
    alphas = []
    for s in segs:
        early = slice(s * tq, s * tq + tk)
        late = slice(s * tq + tk, (s + 1) * tq)
        st = jnp.where(tri, s_late[:, s * tk:(s + 1) * tk], NEG)
        m_prev = m_ref[:, late]
        m_new = jnp.maximum(m_prev, jnp.max(st, axis=0, keepdims=True))
        pt_b[:, early] = jnp.zeros((tk, tk), BF16)
        pt_b[:, late] = jnp.exp2(st - m_new).astype(BF16)
        alphas += [jnp.ones((1, tk), F32), jnp.exp2(m_prev - m_new)]
    alpha = jnp.concatenate(alphas, axis=1)
    acc = alpha * (acc_ref[...] + values(before_last, pt_a)) + values(diag0 + 1, pt_b)
    return acc[:HEAD_V, :] / acc[HEAD_V:HEAD_V + 1, :]


def _mla_kernel(qn_ref, qr_ref, kn_ref, kr_ref, vt_ref, o_ref, *scratch, tq):
    tk = tq // 2
    row = lax.broadcasted_iota(jnp.int32, (LANES, tq), 0)
    zero = jnp.zeros((), BF16)

    def k_block(i):
        rows = pl.ds(pl.multiple_of(i * tk, tk), tk)
        return jnp.concatenate([kn_ref[rows, :], kr_ref[rows, :]], axis=1)

    def vt_block(i):
        return vt_ref[:, pl.ds(pl.multiple_of(i * tk, tk), tk)]

    def query_block(qi, carry):
        span = pl.ds(pl.multiple_of(qi * tq, tq), tq)
        qn, qr = qn_ref[:, span], qr_ref[:, span]
        q0 = jnp.concatenate([jnp.where(row < B_NOPE, qn, zero),
                              jnp.where(row < B_ROPE, qr, zero)], axis=0)
        q1 = jnp.concatenate([jnp.where(row >= B_NOPE, qn, zero),
                              jnp.where((row >= B_ROPE) & (row < 2 * B_ROPE), qr, zero)], axis=0)
        qst = jnp.concatenate([q0, q1], axis=1)
        ot = _flash_cols(qst, k_block, vt_block, scratch, qi=qi, tq=tq,
                         first_block=isinstance(qi, int))
        pair_t = jnp.concatenate([ot[:, :tq], ot[:, tq:]], axis=0)
        o_ref[span, :] = pair_t.T.astype(o_ref.dtype)
        return carry

    query_block(0, 0)
    lax.fori_loop(1, o_ref.shape[0] // tq, query_block, 0)


def _mla_attn(qnt, qrt, kn, kr, vt, *, tq):
    bsz, seq, width = kn.shape
    pairs = width // LANES
    tblk = pl.BlockSpec((None, LANES, seq), lambda b, p: (b, p, 0))
    kblk = pl.BlockSpec((None, seq, LANES), lambda b, p: (b, 0, p))
    krblk = pl.BlockSpec((None, seq, LANES), lambda b, p: (b, 0, 0))
    return pl.pallas_call(
        functools.partial(_mla_kernel, tq=tq),
        grid=(bsz, pairs),
        in_specs=[tblk, tblk, kblk, krblk, tblk],
        out_specs=kblk,
        out_shape=jax.ShapeDtypeStruct((bsz, seq, width), BF16),
        scratch_shapes=_flash_scratch(2 * tq, tq),
        compiler_params=pltpu.CompilerParams(
            dimension_semantics=("arbitrary",) * 2, vmem_limit_bytes=VMEM_LIMIT),
        name="mla_attn",
    )(qnt, qrt, kn, kr, vt)


def _diff_kernel(lam_ref, g_ref, q_ref, k_ref, vt_ref, o_ref, *scratch, tq, lam_init):
    tk = tq // 2
    lane = lax.broadcasted_iota(jnp.int32, (tq, LANES), 1)
    row = lax.broadcasted_iota(jnp.int32, (LANES, tq), 0)
    low = lane < C_V
    zero = jnp.zeros((), BF16)
    lp = lam_ref[...]
    lam = (jnp.exp(jnp.sum(lp[0:1] * lp[1:2], axis=1, keepdims=True))
           - jnp.exp(jnp.sum(lp[2:3] * lp[3:4], axis=1, keepdims=True)) + lam_init)

    def k_block(i):
        return k_ref[pl.ds(pl.multiple_of(i * tk, tk), tk), :]

    def vt_block(i):
        return vt_ref[:, pl.ds(pl.multiple_of(i * tk, tk), tk)]

    def query_block(qi, carry):
        span = pl.ds(pl.multiple_of(qi * tq, tq), tq)
        q = q_ref[:, span]
        qst = jnp.concatenate(
            [jnp.where((row >= j * C_QK) & (row < (j + 1) * C_QK), q, zero) for j in range(4)],
            axis=1)
        ot = _flash_cols(qst, k_block, vt_block, scratch, qi=qi, tq=tq,
                         first_block=isinstance(qi, int))
        d0 = ot[:, 0:tq] - lam * ot[:, tq:2 * tq]
        d1 = ot[:, 2 * tq:3 * tq] - lam * ot[:, 3 * tq:]
        d = jnp.concatenate([d0, d1], axis=0).T
        sq = d * d
        ms_lo = jnp.sum(jnp.where(low, sq, 0.0), axis=1, keepdims=True) * (1.0 / C_V)
        ms_hi = jnp.sum(jnp.where(low, 0.0, sq), axis=1, keepdims=True) * (1.0 / C_V)
        rs = jnp.where(low, lax.rsqrt(ms_lo + SUBLN_EPS), lax.rsqrt(ms_hi + SUBLN_EPS))
        o_ref[span, :] = ((d * rs * g_ref[...]) * (1.0 - lam_init)).astype(o_ref.dtype)
        return carry

    query_block(0, 0)
    lax.fori_loop(1, o_ref.shape[0] // tq, query_block, 0)


def _diff_attn(lam_params, g2, qt, k, vt, *, tq, lam_init):
    bsz, seq, width = k.shape
    pairs = width // LANES
    tblk = pl.BlockSpec((None, LANES, seq), lambda b, p: (b, p, 0))
    kblk = pl.BlockSpec((None, seq, LANES), lambda b, p: (b, 0, p))
    const = lambda b, p: (0, 0)
    return pl.pallas_call(
        functools.partial(_diff_kernel, tq=tq, lam_init=lam_init),
        grid=(bsz, pairs),
        in_specs=[pl.BlockSpec(lam_params.shape, const), pl.BlockSpec(g2.shape, const),
                  tblk, kblk, tblk],
        out_specs=kblk,
        out_shape=jax.ShapeDtypeStruct((bsz, seq, width), BF16),
        scratch_shapes=_flash_scratch(4 * tq, tq),
        compiler_params=pltpu.CompilerParams(
            dimension_semantics=("arbitrary",) * 2, vmem_limit_bytes=VMEM_LIMIT),
        name="diff_attn",
    )(lam_params, g2, qt, k, vt)


def _post_kernel(x_ref, oa_ref, ob_ref, oc_ref, wo_ref, g2_ref, wup_ref, wdn_ref, gf_ref,
                 y_ref, *, final):
    mixed = jnp.concatenate([oa_ref[...], ob_ref[...], oc_ref[...]], axis=1)
    x1 = x_ref[...] + jnp.dot(mixed, wo_ref[...].astype(BF16), preferred_element_type=F32)
    hn = _rms(x1, g2_ref[...], NORM_EPS).astype(BF16)
    y_ref[...] = x1
    for c in range(wup_ref.shape[1] // D_FF_CHUNK):
        cs = slice(c * D_FF_CHUNK, (c + 1) * D_FF_CHUNK)
        u = jnp.dot(hn, wup_ref[:, cs].astype(BF16), preferred_element_type=F32)
        a = jnp.square(jnp.maximum(u, 0.0)).astype(BF16)
        y_ref[...] += jnp.dot(a, wdn_ref[cs, :].astype(BF16), preferred_element_type=F32)
    if final:
        y_ref[...] = _rms(y_ref[...], gf_ref[...], NORM_EPS)


def _post(x2, oa, ob, oc, wo, g2, wup, wdn, gf, *, layer, tm, final):
    n, d = x2.shape
    row = lambda i: (i, 0)
    const = lambda i: (0, 0)
    slab = lambda i: (layer, 0, 0)
    return pl.pallas_call(
        functools.partial(_post_kernel, final=final),
        grid=(n // tm,),
        in_specs=[pl.BlockSpec((tm, d), row),
                  pl.BlockSpec((tm, oa.shape[1]), row),
                  pl.BlockSpec((tm, ob.shape[1]), row),
                  pl.BlockSpec((tm, oc.shape[1]), row),
                  pl.BlockSpec((None,) + wo.shape[1:], slab), pl.BlockSpec(g2.shape, const),
                  pl.BlockSpec((None,) + wup.shape[1:], slab),
                  pl.BlockSpec((None,) + wdn.shape[1:], slab),
                  pl.BlockSpec(gf.shape, const)],
        out_specs=pl.BlockSpec((tm, d), row),
        out_shape=jax.ShapeDtypeStruct((n, d), F32),
        compiler_params=pltpu.CompilerParams(
            dimension_semantics=("arbitrary",), vmem_limit_bytes=VMEM_LIMIT),
        name="post_final" if final else "post",
    )(x2, oa, ob, oc, wo, g2, wup, wdn, gf)


def _rope_tables(seq):
    def tables(dim):
        half = dim // 2
        inv = 1.0 / (ROPE_THETA ** (jnp.arange(half, dtype=F32) / half))
        ang = jnp.arange(seq, dtype=F32)[:, None] * inv[None, :]
        cos, sin = jnp.cos(ang), jnp.sin(ang)
        reps = LANES // dim
        cos_t = jnp.tile(jnp.concatenate([cos, cos], axis=1), (1, reps))
        sin_t = jnp.tile(jnp.concatenate([-sin, sin], axis=1), (1, reps))
        return cos_t, sin_t
    cosa, sina = tables(A_DIM)
    cos32, sin32 = tables(C_QK)
    return cosa, sina, cos32, sin32


def _prep_weights(w_in, w_uq, w_ukv):
    depth = w_in.shape[0]
    cols = jnp.split(w_in, [384, 768, 1152, 1536, 1664, 1696, 1952, 2208], axis=2)
    qa, ka, va, cq, ckv, kr, qc, kc, vc = cols
    w1 = jnp.concatenate([qa, ka, va, cq, ckv, jnp.tile(kr, (1, 1, LANES // B_ROPE)), qc, kc, vc],
                         axis=2).astype(BF16)
    uq = w_uq.reshape(depth, B_Q_LORA, B_HEADS, B_NOPE + B_ROPE)
    nope = uq[..., :B_NOPE].reshape(depth, B_Q_LORA, B_HEADS * B_NOPE)
    rope = uq[..., B_NOPE:].reshape(depth, B_Q_LORA, B_HEADS // 2, 2 * B_ROPE)
    rope = jnp.pad(rope, ((0, 0), (0, 0), (0, 0), (0, LANES - 2 * B_ROPE)))
    wuq = jnp.concatenate([nope, rope.reshape(depth, B_Q_LORA, -1)], axis=2).astype(BF16)
    ukv = w_ukv.reshape(depth, B_KV_LORA, B_HEADS, B_NOPE + B_V)
    wukv = jnp.concatenate([ukv[..., :B_NOPE].reshape(depth, B_KV_LORA, -1),
                            ukv[..., B_NOPE:].reshape(depth, B_KV_LORA, -1)],
                           axis=2).astype(BF16)
    return w1, wuq, wukv


def _tiles(seq):
    dilated = max(window for window, _ in A_CONFIGS)
    causal = min(1024, seq)
    assert seq % dilated == 0 and seq % causal == 0
    return 1024, 512, causal, dilated


def kernel(x, ln1_g, w_in, mla_q_norm_g, mla_w_uq, mla_kv_norm_g, mla_w_ukv, diff_lambda,
           diff_subln_g, w_o, ln2_g, w_up, w_down, final_g):
    bsz, seq, d_model = x.shape
    depth = w_in.shape[0]
    tm_in, tm_post, tq, tile_a = _tiles(seq)
    tabs = _rope_tables(seq)
    x2 = x.reshape(bsz * seq, d_model)
    gf = final_g.reshape(1, d_model)
    sh = lambda t, w: t.reshape(bsz, seq, w)
    w1, wuq, wukv = _prep_weights(w_in, mla_w_uq, mla_w_ukv)
    wo, wup, wdn = w_o, w_up, w_down

    for layer in range(depth):
        lam_init = 0.8 - 0.6 * math.exp(-0.3 * layer)
        qa, ka, va, qn, qr, kn, vb, kr, qc, kc, vc = _inproj(
            x2, ln1_g[layer].reshape(1, -1), w1, tabs,
            mla_q_norm_g[layer].reshape(1, -1), wuq,
            mla_kv_norm_g[layer].reshape(1, -1), wukv, layer=layer, seq=seq, tm=tm_in)

        out_a = _dilated_mixer(qa, ka, va, seq=seq, tile=tile_a)
        out_b = _mla_attn(qn, qr, sh(kn, B_WIDTH), sh(kr, LANES), vb, tq=tq)
        g_sub = jnp.tile(diff_subln_g[layer], LANES // C_V).reshape(1, LANES)
        out_c = _diff_attn(diff_lambda[layer], g_sub, qc, sh(kc, C_WIDTH), vc,
                           tq=tq, lam_init=lam_init)

        x2 = _post(x2, out_a, out_b.reshape(-1, B_WIDTH), out_c.reshape(-1, C_WIDTH), wo,
                   ln2_g[layer].reshape(1, -1), wup, wdn, gf,
                   layer=layer, tm=tm_post, final=layer == depth - 1)

    return x2.reshape(bsz, seq, d_model)
```

```python
import functools
import math

import jax
import jax.numpy as jnp
from jax import lax
from jax.experimental import pallas as pl
from jax.experimental.pallas import tpu as pltpu

F32 = jnp.float32
BF16 = jnp.bfloat16

LANES = 128
HEAD_V = 64
ACC_ROWS = HEAD_V + 16
VMEM_LIMIT = 56 * 1024 * 1024

ROPE_THETA = 10000.0
NORM_EPS = 1e-6
SUBLN_EPS = 1e-5
NEG = -1e30
LOG2E = math.log2(math.e)

A_HEADS, A_DIM = 6, 64
A_CONFIGS = ((128, 1), (512, 4), (2048, 16))
A_SPAN = 128
B_HEADS, B_NOPE, B_ROPE, B_V = 6, 64, 32, 64
B_Q_LORA, B_KV_LORA = 384, 128
C_HEADS, C_QK = 4, 32
C_V = 2 * C_QK
A_WIDTH = A_HEADS * A_DIM
B_WIDTH = B_HEADS * B_V
C_WIDTH = C_HEADS * C_V
D_FF_CHUNK = 1024
INPROJ_SUB = 512


def _rms(x, g, eps):
    return x * lax.rsqrt(jnp.mean(x * x, axis=-1, keepdims=True) + eps) * g


def _rope(x, cos, sin_signed, half):
    lane = lax.broadcasted_iota(jnp.int32, (x.shape[0], LANES), 1)
    first = (lane & (2 * half - 1)) < half
    outs = []
    for c in range(x.shape[1] // LANES):
        xc = x[:, c * LANES:(c + 1) * LANES]
        partner = jnp.where(first, pltpu.roll(xc, LANES - half, 1), pltpu.roll(xc, half, 1))
        outs.append(xc * cos + partner * sin_signed)
    return outs[0] if len(outs) == 1 else jnp.concatenate(outs, axis=1)


def _nt_dot(a, b):
    return lax.dot_general(a, b, (((1,), (1,)), ((), ())), preferred_element_type=F32)


def _inproj_kernel(x_ref, g_ref, w1_ref, cosa_ref, sina_ref, cos32_ref, sin32_ref,
                   qg_ref, wuq_ref, kvg_ref, wukv_ref,
                   qa_ref, ka_ref, va_ref, qn_ref, qr_ref, kn_ref, vb_ref, kr_ref,
                   qc_ref, kc_ref, vc_ref):
    scale_a = A_DIM ** -0.5 * LOG2E
    scale_b = (B_NOPE + B_ROPE) ** -0.5 * LOG2E
    scale_c = C_QK ** -0.5 * LOG2E

    for r0 in range(0, x_ref.shape[0], INPROJ_SUB):
        rows = slice(r0, r0 + INPROJ_SUB)
        h = _rms(x_ref[rows, :], g_ref[...], NORM_EPS).astype(BF16)
        p_all = jnp.dot(h, w1_ref[...], preferred_element_type=F32)

        def proj(lo, width):
            return p_all[:, lo:lo + width]

        cosa, sina = cosa_ref[rows, :], sina_ref[rows, :]
        cos32, sin32 = cos32_ref[rows, :], sin32_ref[rows, :]

        qa_ref[rows, :] = _rope(proj(0, 384), cosa, sina, 32) * scale_a
        ka_ref[rows, :] = _rope(proj(384, 384), cosa, sina, 32)
        va_ref[rows, :] = proj(768, 384)

        cq = _rms(proj(1152, 384), qg_ref[...], NORM_EPS).astype(BF16)
        qf = jnp.dot(cq, wuq_ref[...], preferred_element_type=F32)
        qn_ref[:, rows] = (qf[:, :384] * scale_b).T.astype(BF16)
        qr_ref[:, rows] = (_rope(qf[:, 384:], cos32, sin32, 16) * scale_b).T.astype(BF16)

        ckv = _rms(proj(1536, 128), kvg_ref[...], NORM_EPS).astype(BF16)
        kvf = jnp.dot(ckv, wukv_ref[...], preferred_element_type=F32)
        kn_ref[rows, :] = kvf[:, :384].astype(BF16)
        vb_ref[:, rows] = kvf[:, 384:].T.astype(BF16)
        kr_ref[rows, :] = _rope(proj(1664, 128), cos32, sin32, 16).astype(BF16)

        qc_ref[:, rows] = (_rope(proj(1792, 256), cos32, sin32, 16) * scale_c).T.astype(BF16)
        kc_ref[rows, :] = _rope(proj(2048, 256), cos32, sin32, 16).astype(BF16)
        vc_ref[:, rows] = proj(2304, 256).T.astype(BF16)


def _inproj(x2, g, w1, tabs, qg, wuq, kvg, wukv, *, layer, seq, tm):
    n = x2.shape[0]
    nt = n // tm
    st = seq // tm
    row = lambda i: (i, 0)
    const = lambda i: (0, 0)
    slab = lambda i: (layer, 0, 0)
    tab = lambda i: (i % st, 0)
    widths = (384, 384, 384, 384, 384, 384, 384, 128, 256, 256, 256)
    major = (False, False, False, True, True, False, True, False, True, False, True)
    batched = (False,) * 5 + (True, False, True, False, True, False)
    bsz = n // seq
    col = lambda i: (i // st, 0, i % st)
    brow = lambda i: (i // st, i % st, 0)

    def spec(w, t, b):
        if t:
            return pl.BlockSpec((None, w, tm), col)
        return pl.BlockSpec((None, tm, w), brow) if b else pl.BlockSpec((tm, w), row)

    def shape(w, t, b):
        return (bsz, w, seq) if t else ((bsz, seq, w) if b else (n, w))

    out_specs = [spec(w, t, b) for w, t, b in zip(widths, major, batched)]
    dtypes = (F32,) * 3 + (BF16,) * 8
    out_shape = [jax.ShapeDtypeStruct(shape(w, t, b), dt)
                 for w, t, b, dt in zip(widths, major, batched, dtypes)]
    return pl.pallas_call(
        _inproj_kernel,
        grid=(nt,),
        in_specs=[
            pl.BlockSpec((tm, x2.shape[1]), row),
            pl.BlockSpec(g.shape, const),
            pl.BlockSpec((None,) + w1.shape[1:], slab),
            pl.BlockSpec((tm, LANES), tab), pl.BlockSpec((tm, LANES), tab),
            pl.BlockSpec((tm, LANES), tab), pl.BlockSpec((tm, LANES), tab),
            pl.BlockSpec(qg.shape, const), pl.BlockSpec((None,) + wuq.shape[1:], slab),
            pl.BlockSpec(kvg.shape, const), pl.BlockSpec((None,) + wukv.shape[1:], slab),
        ],
        out_specs=out_specs,
        out_shape=out_shape,
        compiler_params=pltpu.CompilerParams(
            dimension_semantics=("arbitrary",), vmem_limit_bytes=VMEM_LIMIT),
        name="inproj",
    )(x2, g, w1, *tabs, qg, wuq, kvg, wukv)


def _dilated_kernel(q_ref, k_ref, kp_ref, v_ref, vp_ref, o_ref,
                    q4_ref, k4_ref, kp4_ref, v4_ref, vp4_ref,
                    n4_ref, d4_ref, m4_ref, nt_ref, dt_ref, mt_ref, *, tile, nt):
    quarter = tile // 4
    t = pl.program_id(0) % nt
    lane = lax.broadcasted_iota(jnp.int32, (A_SPAN, LANES), 1)
    low = lane < A_DIM
    qi = lax.broadcasted_iota(jnp.int32, (2 * A_SPAN, 2 * A_SPAN), 0) & (A_SPAN - 1)
    ki = lax.broadcasted_iota(jnp.int32, (2 * A_SPAN, 2 * A_SPAN), 1)
    dist = qi + A_SPAN - ki
    band = (dist >= 0) & (dist <= A_SPAN)
    band0 = band & (ki >= jnp.where(t > 0, 0, A_SPAN))
    ones = jnp.ones((2 * A_SPAN, LANES), BF16)
    zero = jnp.zeros((), BF16)

    for src, dst in ((q_ref, q4_ref), (k_ref, k4_ref), (kp_ref, kp4_ref),
                     (v_ref, v4_ref), (vp_ref, vp4_ref)):
        for c in range(4):
            dst[c * quarter:(c + 1) * quarter, :] = src[pl.ds(c, quarter, stride=4), :]

    def attend(qg, k_prev, k_cur, v_prev, v_cur, valid):
        qg = qg.astype(BF16)
        qs = jnp.concatenate([jnp.where(low, qg, zero), jnp.where(low, zero, qg)], axis=0)
        kk = jnp.concatenate([k_prev, k_cur], axis=0).astype(BF16)
        vv = jnp.concatenate([v_prev, v_cur], axis=0).astype(BF16)
        s = jnp.where(valid, _nt_dot(qs, kk), NEG)
        m = jnp.max(s, axis=1, keepdims=True)
        p = jnp.exp2(s - m).astype(BF16)
        pv = jnp.dot(p, jnp.concatenate([vv, ones], axis=1), preferred_element_type=F32)
        num = jnp.where(low, pv[:A_SPAN, :LANES], pv[A_SPAN:, :LANES])
        den = jnp.where(low, pv[:A_SPAN, LANES:], pv[A_SPAN:, LANES:])
        return num, den, jnp.where(low, m[:A_SPAN], m[A_SPAN:])

    def merge(a, b):
        (num_a, den_a, m_a), (num_b, den_b, m_b) = a, b
        mx = jnp.maximum(m_a, m_b)
        wa = jnp.exp2(m_a - mx)
        wb = jnp.exp2(m_b - mx)
        return wa * num_a + wb * num_b, wa * den_a + wb * den_b, mx

    def load(refs, rows):
        return tuple(r[rows, :] for r in refs)

    def store(refs, rows, state):
        for r, x in zip(refs, state):
            r[rows, :] = x

    state4 = (n4_ref, d4_ref, m4_ref)
    state_t = (nt_ref, dt_ref, mt_ref)

    for c in range(4):
        for g in range(quarter // A_SPAN):
            cur = pl.ds(c * quarter + g * A_SPAN, A_SPAN)
            if g == 0:
                prev = pl.ds((c + 1) * quarter - A_SPAN, A_SPAN)
                k_prev, v_prev, valid = kp4_ref[prev, :], vp4_ref[prev, :], band0
            else:
                prev = pl.ds(c * quarter + (g - 1) * A_SPAN, A_SPAN)
                k_prev, v_prev, valid = k4_ref[prev, :], v4_ref[prev, :], band
            store(state4, cur,
                  attend(q4_ref[cur, :], k_prev, k4_ref[cur, :], v_prev, v4_ref[cur, :], valid))

    assert tile == A_SPAN * 16
    for c in range(4):
        for a in range(4):
            cur = pl.ds(c * quarter + a, A_SPAN, stride=4)
            new = attend(q4_ref[cur, :], kp4_ref[cur, :], k4_ref[cur, :],
                         vp4_ref[cur, :], v4_ref[cur, :], band0)
            store(state4, cur, merge(new, load(state4, cur)))

    for c in range(4):
        for r4, rt in zip(state4, state_t):
            rt[pl.ds(c, quarter, stride=4), :] = r4[c * quarter:(c + 1) * quarter, :]

    for g in range(tile // A_SPAN):
        cur = pl.ds(g * A_SPAN, A_SPAN)
        if g == 0:
            prev = pl.ds(tile - A_SPAN, A_SPAN)
            k_prev, v_prev, valid = kp_ref[prev, :], vp_ref[prev, :], band0
        else:
            prev = pl.ds((g - 1) * A_SPAN, A_SPAN)
            k_prev, v_prev, valid = k_ref[prev, :], v_ref[prev, :], band
        new = attend(q_ref[cur, :], k_prev, k_ref[cur, :], v_prev, v_ref[cur, :], valid)
        num, den, _ = merge(new, load(state_t, cur))
        o_ref[cur, :] = (num / den).astype(o_ref.dtype)


def _dilated_mixer(q, k, v, *, seq, tile):
    n, width = q.shape
    nt = seq // tile
    for window, dilation in A_CONFIGS:
        assert window // dilation == A_SPAN and tile % window == 0
    cur = lambda i, p: (i, p)
    prev = lambda i, p: (jnp.where(i % nt == 0, i, i - 1), p)
    blk = pl.BlockSpec((tile, LANES), cur)
    pblk = pl.BlockSpec((tile, LANES), prev)
    return pl.pallas_call(
        functools.partial(_dilated_kernel, tile=tile, nt=nt),
        grid=(n // tile, width // LANES),
        in_specs=[blk, blk, pblk, blk, pblk],
        out_specs=blk,
        out_shape=jax.ShapeDtypeStruct((n, width), BF16),
        scratch_shapes=[pltpu.VMEM((tile, LANES), F32)] * 11,
        compiler_params=pltpu.CompilerParams(
            dimension_semantics=("arbitrary",) * 2, vmem_limit_bytes=VMEM_LIMIT),
        name="dilated",
    )(q, k, k, v, v)


def _flash_scratch(rows, tq):
    tk = tq // 2
    score_bufs = [pltpu.VMEM((tk, rows), F32), pltpu.VMEM((1, rows), F32),
                  pltpu.VMEM((tk, rows), BF16)]
    return [pltpu.VMEM((1, rows), F32), pltpu.VMEM((ACC_ROWS, rows), F32)] + 2 * score_bufs


def _flash_cols(qst, k_block, vt_block, scratch, *, qi, tq):
    m_ref, acc_ref, st_a, cm_a, pt_a, st_b, cm_b, pt_b = scratch
    tk = tq // 2
    segs = range(qst.shape[1] // tq)
    ones = jnp.ones((ACC_ROWS - HEAD_V, tk), BF16)
    tri = (lax.broadcasted_iota(jnp.int32, (tk, tk), 0)
           <= lax.broadcasted_iota(jnp.int32, (tk, tk), 1))

    def scores(blk, st_ref, cm_ref):
        st = jnp.dot(k_block(blk), qst, preferred_element_type=F32)
        st_ref[...] = st
        cm_ref[...] = jnp.max(st, axis=0, keepdims=True)

    def softmax(st_ref, cm_ref, pt_ref):
        m_prev = m_ref[...]
        m_new = jnp.maximum(m_prev, cm_ref[...])
        m_ref[...] = m_new
        pt_ref[...] = jnp.exp2(st_ref[...] - m_new).astype(BF16)
        return jnp.exp2(m_prev - m_new)

    def values(blk, pt_ref):
        vt = vt_block(blk)
        half = qst.shape[1] // 2
        outs = []
        for h in range(2):
            lhs = jnp.concatenate([vt[h * HEAD_V:(h + 1) * HEAD_V], ones], axis=0)
            outs.append(jnp.dot(lhs, pt_ref[:, h * half:(h + 1) * half],
                                preferred_element_type=F32))
        return jnp.concatenate(outs, axis=1)

    diag0 = 2 * qi
    q_late = jnp.concatenate([qst[:, s * tq + tk:(s + 1) * tq] for s in segs], axis=1)
    s_late = jnp.dot(k_block(diag0 + 1), q_late, preferred_element_type=F32)
    for s in segs:
        early = slice(s * tq, s * tq + tk)
        late = slice(s * tq + tk, (s + 1) * tq)
        st = jnp.where(tri, s_late[:, s * tk:(s + 1) * tk], NEG)
        cm = jnp.max(st, axis=0, keepdims=True)
        m_ref[:, early] = jnp.full((1, tk), NEG, F32)
        m_ref[:, late] = cm
        pt_a[:, early] = jnp.zeros((tk, tk), BF16)
        pt_a[:, late] = jnp.exp2(st - cm).astype(BF16)
    s_full = jnp.dot(k_block(diag0), qst, preferred_element_type=F32)
    st = jnp.concatenate(
        [x for s in segs for x in (jnp.where(tri, s_full[:, s * tq:s * tq + tk], NEG),
                                   s_full[:, s * tq + tk:(s + 1) * tq])], axis=1)
    st_b[...] = st
    cm_b[...] = jnp.max(st, axis=0, keepdims=True)
    acc_ref[...] = jnp.zeros(acc_ref.shape, F32)

    def body(j, carry):
        first = j == 0
        alpha = softmax(st_b, cm_b, pt_b)
        scores(2 * j, st_a, cm_a)
        acc_ref[...] = alpha * (acc_ref[...] + values(jnp.where(first, diag0 + 1, 2 * j - 2), pt_a))
        alpha = softmax(st_a, cm_a, pt_a)
        scores(2 * j + 1, st_b, cm_b)
        acc_ref[...] = alpha * (acc_ref[...] + values(jnp.where(first, diag0, 2 * j - 1), pt_b))
        return carry

    lax.fori_loop(0, qi, body, 0)

    first = qi == 0
    alpha = softmax(st_b, cm_b, pt_b)
    acc_ref[...] = alpha * (acc_ref[...] + values(jnp.where(first, 1, 2 * qi - 2), pt_a))
    acc_ref[...] += values(jnp.where(first, 0, 2 * qi - 1), pt_b)
    return acc_ref[:HEAD_V, :] / acc_ref[HEAD_V:HEAD_V + 1, :]


def _mla_kernel(qn_ref, qr_ref, kn_ref, kr_ref, vt_ref, o_ref, *scratch, tq):
    tk = tq // 2
    row = lax.broadcasted_iota(jnp.int32, (LANES, tq), 0)
    zero = jnp.zeros((), BF16)

    def k_block(i):
        rows = pl.ds(pl.multiple_of(i * tk, tk), tk)
        return jnp.concatenate([kn_ref[rows, :], kr_ref[rows, :]], axis=1)

    def vt_block(i):
        return vt_ref[:, pl.ds(pl.multiple_of(i * tk, tk), tk)]

    def query_block(qi, carry):
        span = pl.ds(pl.multiple_of(qi * tq, tq), tq)
        qn, qr = qn_ref[:, span], qr_ref[:, span]
        q0 = jnp.concatenate([jnp.where(row < B_NOPE, qn, zero),
                              jnp.where(row < B_ROPE, qr, zero)], axis=0)
        q1 = jnp.concatenate([jnp.where(row >= B_NOPE, qn, zero),
                              jnp.where((row >= B_ROPE) & (row < 2 * B_ROPE), qr, zero)], axis=0)
        qst = jnp.concatenate([q0, q1], axis=1)
        ot = _flash_cols(qst, k_block, vt_block, scratch, qi=qi, tq=tq)
        pair_t = jnp.concatenate([ot[:, :tq], ot[:, tq:]], axis=0)
        o_ref[span, :] = pair_t.T.astype(o_ref.dtype)
        return carry

    lax.fori_loop(0, o_ref.shape[0] // tq, query_block, 0)


def _mla_attn(qnt, qrt, kn, kr, vt, *, tq):
    bsz, seq, width = kn.shape
    pairs = width // LANES
    tblk = pl.BlockSpec((None, LANES, seq), lambda b, p: (b, p, 0))
    kblk = pl.BlockSpec((None, seq, LANES), lambda b, p: (b, 0, p))
    krblk = pl.BlockSpec((None, seq, LANES), lambda b, p: (b, 0, 0))
    return pl.pallas_call(
        functools.partial(_mla_kernel, tq=tq),
        grid=(bsz, pairs),
        in_specs=[tblk, tblk, kblk, krblk, tblk],
        out_specs=pl.BlockSpec((seq, LANES), lambda b, p: (b, p)),
        out_shape=jax.ShapeDtypeStruct((bsz * seq, width), BF16),
        scratch_shapes=_flash_scratch(2 * tq, tq),
        compiler_params=pltpu.CompilerParams(
            dimension_semantics=("arbitrary",) * 2, vmem_limit_bytes=VMEM_LIMIT),
        name="mla_attn",
    )(qnt, qrt, kn, kr, vt)


def _diff_kernel(lam_ref, g_ref, q_ref, k_ref, vt_ref, o_ref, *scratch, tq, lam_init):
    tk = tq // 2
    lane = lax.broadcasted_iota(jnp.int32, (tq, LANES), 1)
    row = lax.broadcasted_iota(jnp.int32, (LANES, tq), 0)
    low = lane < C_V
    zero = jnp.zeros((), BF16)
    lp = lam_ref[...]
    lam = (jnp.exp(jnp.sum(lp[0:1] * lp[1:2], axis=1, keepdims=True))
           - jnp.exp(jnp.sum(lp[2:3] * lp[3:4], axis=1, keepdims=True)) + lam_init)

    def k_block(i):
        return k_ref[pl.ds(pl.multiple_of(i * tk, tk), tk), :]

    def vt_block(i):
        return vt_ref[:, pl.ds(pl.multiple_of(i * tk, tk), tk)]

    def query_block(qi, carry):
        span = pl.ds(pl.multiple_of(qi * tq, tq), tq)
        q = q_ref[:, span]
        qst = jnp.concatenate(
            [jnp.where((row >= j * C_QK) & (row < (j + 1) * C_QK), q, zero) for j in range(4)],
            axis=1)
        ot = _flash_cols(qst, k_block, vt_block, scratch, qi=qi, tq=tq)
        d0 = ot[:, 0:tq] - lam * ot[:, tq:2 * tq]
        d1 = ot[:, 2 * tq:3 * tq] - lam * ot[:, 3 * tq:]
        d = jnp.concatenate([d0, d1], axis=0).T
        sq = d * d
        ms_lo = jnp.sum(jnp.where(low, sq, 0.0), axis=1, keepdims=True) * (1.0 / C_V)
        ms_hi = jnp.sum(jnp.where(low, 0.0, sq), axis=1, keepdims=True) * (1.0 / C_V)
        rs = jnp.where(low, lax.rsqrt(ms_lo + SUBLN_EPS), lax.rsqrt(ms_hi + SUBLN_EPS))
        o_ref[span, :] = ((d * rs * g_ref[...]) * (1.0 - lam_init)).astype(o_ref.dtype)
        return carry

    lax.fori_loop(0, o_ref.shape[0] // tq, query_block, 0)


def _diff_attn(lam_params, g2, qt, k, vt, *, tq, lam_init):
    bsz, seq, width = k.shape
    pairs = width // LANES
    tblk = pl.BlockSpec((None, LANES, seq), lambda b, p: (b, p, 0))
    kblk = pl.BlockSpec((None, seq, LANES), lambda b, p: (b, 0, p))
    const = lambda b, p: (0, 0)
    return pl.pallas_call(
        functools.partial(_diff_kernel, tq=tq, lam_init=lam_init),
        grid=(bsz, pairs),
        in_specs=[pl.BlockSpec(lam_params.shape, const), pl.BlockSpec(g2.shape, const),
                  tblk, kblk, tblk],
        out_specs=pl.BlockSpec((seq, LANES), lambda b, p: (b, p)),
        out_shape=jax.ShapeDtypeStruct((bsz * seq, width), BF16),
        scratch_shapes=_flash_scratch(4 * tq, tq),
        compiler_params=pltpu.CompilerParams(
            dimension_semantics=("arbitrary",) * 2, vmem_limit_bytes=VMEM_LIMIT),
        name="diff_attn",
    )(lam_params, g2, qt, k, vt)


def _post_kernel(x_ref, oa_ref, ob_ref, oc_ref, wo_ref, g2_ref, wup_ref, wdn_ref, gf_ref,
                 y_ref, *, final):
    mixed = jnp.concatenate([oa_ref[...], ob_ref[...], oc_ref[...]], axis=1)
    x1 = x_ref[...] + jnp.dot(mixed, wo_ref[...].astype(BF16), preferred_element_type=F32)
    hn = _rms(x1, g2_ref[...], NORM_EPS).astype(BF16)
    y_ref[...] = x1
    for c in range(wup_ref.shape[1] // D_FF_CHUNK):
        cs = slice(c * D_FF_CHUNK, (c + 1) * D_FF_CHUNK)
        u = jnp.dot(hn, wup_ref[:, cs].astype(BF16), preferred_element_type=F32)
        a = jnp.square(jnp.maximum(u, 0.0)).astype(BF16)
        y_ref[...] += jnp.dot(a, wdn_ref[cs, :].astype(BF16), preferred_element_type=F32)
    if final:
        y_ref[...] = _rms(y_ref[...], gf_ref[...], NORM_EPS)


def _post(x2, oa, ob, oc, wo, g2, wup, wdn, gf, *, layer, tm, final):
    n, d = x2.shape
    row = lambda i: (i, 0)
    const = lambda i: (0, 0)
    slab = lambda i: (layer, 0, 0)
    return pl.pallas_call(
        functools.partial(_post_kernel, final=final),
        grid=(n // tm,),
        in_specs=[pl.BlockSpec((tm, d), row),
                  pl.BlockSpec((tm, oa.shape[1]), row),
                  pl.BlockSpec((tm, ob.shape[1]), row),
                  pl.BlockSpec((tm, oc.shape[1]), row),
                  pl.BlockSpec((None,) + wo.shape[1:], slab), pl.BlockSpec(g2.shape, const),
                  pl.BlockSpec((None,) + wup.shape[1:], slab),
                  pl.BlockSpec((None,) + wdn.shape[1:], slab),
                  pl.BlockSpec(gf.shape, const)],
        out_specs=pl.BlockSpec((tm, d), row),
        out_shape=jax.ShapeDtypeStruct((n, d), F32),
        compiler_params=pltpu.CompilerParams(
            dimension_semantics=("arbitrary",), vmem_limit_bytes=VMEM_LIMIT),
        name="post_final" if final else "post",
    )(x2, oa, ob, oc, wo, g2, wup, wdn, gf)


def _rope_tables(seq):
    def tables(dim):
        half = dim // 2
        inv = 1.0 / (ROPE_THETA ** (jnp.arange(half, dtype=F32) / half))
        ang = jnp.arange(seq, dtype=F32)[:, None] * inv[None, :]
        cos, sin = jnp.cos(ang), jnp.sin(ang)
        reps = LANES // dim
        cos_t = jnp.tile(jnp.concatenate([cos, cos], axis=1), (1, reps))
        sin_t = jnp.tile(jnp.concatenate([-sin, sin], axis=1), (1, reps))
        return cos_t, sin_t
    cosa, sina = tables(A_DIM)
    cos32, sin32 = tables(C_QK)
    return cosa, sina, cos32, sin32


def _prep_weights(w_in, w_uq, w_ukv):
    depth = w_in.shape[0]
    cols = jnp.split(w_in, [384, 768, 1152, 1536, 1664, 1696, 1952, 2208], axis=2)
    qa, ka, va, cq, ckv, kr, qc, kc, vc = cols
    w1 = jnp.concatenate([qa, ka, va, cq, ckv, jnp.tile(kr, (1, 1, LANES // B_ROPE)), qc, kc, vc],
                         axis=2).astype(BF16)
    uq = w_uq.reshape(depth, B_Q_LORA, B_HEADS, B_NOPE + B_ROPE)
    nope = uq[..., :B_NOPE].reshape(depth, B_Q_LORA, B_HEADS * B_NOPE)
    rope = uq[..., B_NOPE:].reshape(depth, B_Q_LORA, B_HEADS // 2, 2 * B_ROPE)
    rope = jnp.pad(rope, ((0, 0), (0, 0), (0, 0), (0, LANES - 2 * B_ROPE)))
    wuq = jnp.concatenate([nope, rope.reshape(depth, B_Q_LORA, -1)], axis=2).astype(BF16)
    ukv = w_ukv.reshape(depth, B_KV_LORA, B_HEADS, B_NOPE + B_V)
    wukv = jnp.concatenate([ukv[..., :B_NOPE].reshape(depth, B_KV_LORA, -1),
                            ukv[..., B_NOPE:].reshape(depth, B_KV_LORA, -1)],
                           axis=2).astype(BF16)
    return w1, wuq, wukv


def _tiles(seq):
    dilated = max(window for window, _ in A_CONFIGS)
    causal = min(1024, seq)
    assert seq % dilated == 0 and seq % causal == 0
    return 1024, 512, causal, dilated


def kernel(x, ln1_g, w_in, mla_q_norm_g, mla_w_uq, mla_kv_norm_g, mla_w_ukv, diff_lambda,
           diff_subln_g, w_o, ln2_g, w_up, w_down, final_g):
    bsz, seq, d_model = x.shape
    depth = w_in.shape[0]
    tm_in, tm_post, tq, tile_a = _tiles(seq)
    tabs = _rope_tables(seq)
    x2 = x.reshape(bsz * seq, d_model)
    gf = final_g.reshape(1, d_model)
    w1, wuq, wukv = _prep_weights(w_in, mla_w_uq, mla_w_ukv)
    wo, wup, wdn = w_o, w_up, w_down

    for layer in range(depth):
        lam_init = 0.8 - 0.6 * math.exp(-0.3 * layer)
        qa, ka, va, qn, qr, kn, vb, kr, qc, kc, vc = _inproj(
            x2, ln1_g[layer].reshape(1, -1), w1, tabs,
            mla_q_norm_g[layer].reshape(1, -1), wuq,
            mla_kv_norm_g[layer].reshape(1, -1), wukv, layer=layer, seq=seq, tm=tm_in)

        out_a = _dilated_mixer(qa, ka, va, seq=seq, tile=tile_a)
        out_b = _mla_attn(qn, qr, kn, kr, vb, tq=tq)
        g_sub = jnp.tile(diff_subln_g[layer], LANES // C_V).reshape(1, LANES)
        out_c = _diff_attn(diff_lambda[layer], g_sub, qc, kc, vc, tq=tq, lam_init=lam_init)

        x2 = _post(x2, out_a, out_b, out_c, wo,
                   ln2_g[layer].reshape(1, -1), wup, wdn, gf,
                   layer=layer, tm=tm_post, final=layer == depth - 1)

    return x2.reshape(bsz, seq, d_model)
```
